```python
import jax, jax.numpy as jnp
from jax import lax
import numpy as np

D_MODEL = 1024
BATCH = 8
SEQ = 2048
DEPTH = 2

HEAD_DIM = 64
SWA_Q_HEADS = 8
SWA_KV_HEADS = 2
SWA_WINDOW = 128
SB_HEADS = 8
SB_Q_BLOCK = 128
MOBA_HEADS = 8
MOBA_BLOCK = 256
MOBA_TOPK = 3
MOBA_Q_CHUNK = 16
BRANCH_WIDTH = 512
N_BRANCHES = 3
D_FF = 2816
CONV_WIDTH = 3
ROPE_THETA = 10000.0
RMS_EPS = 1e-6

IN_SIZES = (
    SWA_Q_HEADS * HEAD_DIM, SWA_KV_HEADS * HEAD_DIM, SWA_KV_HEADS * HEAD_DIM,
    SB_HEADS * HEAD_DIM, SB_HEADS * HEAD_DIM, SB_HEADS * HEAD_DIM,
    MOBA_HEADS * HEAD_DIM, MOBA_HEADS * HEAD_DIM, MOBA_HEADS * HEAD_DIM,
    N_BRANCHES * D_MODEL,
)
D_IN = sum(IN_SIZES)

kernel_name = "hybrid_gated_swa_stickbreak_moba_block"


def rmsnorm(x, g):
    xf = x.astype(jnp.float32)
    y = xf * lax.rsqrt(jnp.mean(xf * xf, axis=-1, keepdims=True) + RMS_EPS)
    return (y * g.astype(jnp.float32)).astype(x.dtype)


def rope_tables(seq):
    inv = ROPE_THETA ** (-jnp.arange(0, HEAD_DIM, 2, dtype=jnp.float32) / HEAD_DIM)
    ang = jnp.arange(seq, dtype=jnp.float32)[:, None] * inv[None, :]
    return jnp.cos(ang), jnp.sin(ang)


def apply_rope(x, cos, sin):
    x1, x2 = jnp.split(x.astype(jnp.float32), 2, axis=-1)
    c = cos[None, :, None, :]
    s = sin[None, :, None, :]
    return jnp.concatenate([x1 * c - x2 * s, x2 * c + x1 * s], axis=-1).astype(x.dtype)


def swa_sink_attention(q, k, v, sinks):
    B, S = q.shape[0], q.shape[1]
    W = SWA_WINDOW
    nblk = S // W
    G = SWA_Q_HEADS // SWA_KV_HEADS
    scale = HEAD_DIM ** -0.5
    qb = q.reshape(B, nblk, W, SWA_KV_HEADS, G, HEAD_DIM)
    kb = k.reshape(B, nblk, W, SWA_KV_HEADS, HEAD_DIM)
    vb = v.reshape(B, nblk, W, SWA_KV_HEADS, HEAD_DIM)
    pad = ((0, 0), (1, 0), (0, 0), (0, 0), (0, 0))
    kk = jnp.concatenate([jnp.pad(kb, pad)[:, :-1], kb], axis=2)
    vv = jnp.concatenate([jnp.pad(vb, pad)[:, :-1], vb], axis=2)
    scores = jnp.einsum('bnqhgd,bnkhd->bnhgqk', qb, kk).astype(jnp.float32) * scale
    qpos = jnp.arange(W)[:, None] + W
    kpos = jnp.arange(2 * W)[None, :]
    diff = qpos - kpos
    band = (diff >= 0) & (diff < W)
    blk = jnp.arange(nblk)[:, None, None]
    valid = band[None] & ((blk * W + kpos[None] - W) >= 0)
    scores = jnp.where(valid[None, :, None, None], scores, -jnp.inf)
    sink = sinks.astype(jnp.float32).reshape(SWA_KV_HEADS, G)[None, None, :, :, None, None]
    sink = jnp.broadcast_to(sink, scores.shape[:-1] + (1,))
    probs = jax.nn.softmax(jnp.concatenate([scores, sink], axis=-1), axis=-1)[..., :-1]
    out = jnp.einsum('bnhgqk,bnkhd->bnqhgd', probs.astype(v.dtype), vv)
    return out.reshape(B, S, SWA_Q_HEADS * HEAD_DIM)


def stick_breaking_attention(q, k, v):
    B, S, H, dh = q.shape
    nblk = S // SB_Q_BLOCK
    scale = dh ** -0.5
    q_blocks = q.reshape(B, nblk, SB_Q_BLOCK, H, dh).transpose(1, 0, 2, 3, 4)
    kpos = jnp.arange(S)

    def block(args):
        q_blk, i = args
        z = jnp.einsum('bqhd,bkhd->bhqk', q_blk, k).astype(jnp.float32) * scale
        qpos = i * SB_Q_BLOCK + jnp.arange(SB_Q_BLOCK)
        past = kpos[None, :] < qpos[:, None]
        log_beta = jax.nn.log_sigmoid(z)
        log_one_minus = jnp.where(past, jax.nn.log_sigmoid(-z), 0.0)
        remain = lax.cumsum(log_one_minus, axis=3, reverse=True) - log_one_minus
        w = jnp.where(past, jnp.exp(log_beta + remain), 0.0)
        return jnp.einsum('bhqk,bkhd->bqhd', w.astype(v.dtype), v)

    out = lax.map(block, (q_blocks, jnp.arange(nblk)))
    return out.transpose(1, 0, 2, 3, 4).reshape(B, S, H * dh)


def moba_attention(q, k, v):
    B, S, H, dh = q.shape
    nb = -(-S // MOBA_BLOCK)
    pad = nb * MOBA_BLOCK - S
    ksel = min(MOBA_TOPK, nb)
    scale = dh ** -0.5
    kp = jnp.pad(k, ((0, 0), (0, pad), (0, 0), (0, 0)))
    vp = jnp.pad(v, ((0, 0), (0, pad), (0, 0), (0, 0)))
    kb = kp.reshape(B, nb, MOBA_BLOCK, H, dh)
    vb = vp.reshape(B, nb, MOBA_BLOCK, H, dh)
    k_mean = jnp.mean(kb.astype(jnp.float32), axis=2)
    gate = jnp.einsum('bshd,bnhd->bhsn', q.astype(jnp.float32), k_mean)
    q_blk = jnp.arange(S) // MOBA_BLOCK
    past_blk = jnp.arange(nb)[None, :] < q_blk[:, None]
    gate = jnp.where(past_blk[None, None], gate, -jnp.inf)
    _, sel = lax.top_k(gate, ksel)
    sel_valid = sel < q_blk[None, None, :, None]

    C = MOBA_Q_CHUNK
    nchunk = S // C
    q_c = q.reshape(B, nchunk, C, H, dh).transpose(1, 0, 2, 3, 4)
    sel_c = sel.reshape(B, H, nchunk, C, ksel).transpose(2, 0, 1, 3, 4)
    valid_c = sel_valid.reshape(B, H, nchunk, C, ksel).transpose(2, 0, 1, 3, 4)
    kb_h = kb.transpose(0, 3, 1, 2, 4)
    vb_h = vb.transpose(0, 3, 1, 2, 4)
    gather = jax.vmap(jax.vmap(lambda blocks, idx: blocks[idx]))

    def chunk(args):
        q_blk, sel_blk, valid_blk, i = args
        t0 = i * C
        qpos = t0 + jnp.arange(C)
        own_start = (t0 // MOBA_BLOCK) * MOBA_BLOCK
        k_own = lax.dynamic_slice_in_dim(kp, own_start, MOBA_BLOCK, axis=1)
        v_own = lax.dynamic_slice_in_dim(vp, own_start, MOBA_BLOCK, axis=1)
        k_sel = gather(kb_h, sel_blk)
        v_sel = gather(vb_h, sel_blk)
        s_sel = jnp.einsum('bqhd,bhqnkd->bhqnk', q_blk, k_sel).astype(jnp.float32) * scale
        s_sel = jnp.where(valid_blk[..., None], s_sel, -jnp.inf).reshape(B, H, C, ksel * MOBA_BLOCK)
        s_own = jnp.einsum('bqhd,bkhd->bhqk', q_blk, k_own).astype(jnp.float32) * scale
        kpos_own = own_start + jnp.arange(MOBA_BLOCK)
        s_own = jnp.where(kpos_own[None, :] <= qpos[:, None], s_own, -jnp.inf)
        p = jax.nn.softmax(jnp.concatenate([s_sel, s_own], axis=-1), axis=-1)
        p_sel = p[..., :ksel * MOBA_BLOCK].reshape(B, H, C, ksel, MOBA_BLOCK).astype(v.dtype)
        p_own = p[..., ksel * MOBA_BLOCK:].astype(v.dtype)
        return (jnp.einsum('bhqnk,bhqnkd->bqhd', p_sel, v_sel)
                + jnp.einsum('bhqk,bkhd->bqhd', p_own, v_own))

    out = lax.map(chunk, (q_c, sel_c, valid_c, jnp.arange(nchunk)))
    return out.transpose(1, 0, 2, 3, 4).reshape(B, S, H * dh)


def causal_depthwise_conv(u, w, b):
    S = u.shape[1]
    up = jnp.pad(u, ((0, 0), (CONV_WIDTH - 1, 0), (0, 0)))
    out = b
    for tap in range(CONV_WIDTH):
        out = out + w[tap] * up[:, tap:tap + S]
    return out


def hybrid_layer(x, cos, sin, norm_mix, w_in, b_gate, sinks, w_branch, w_out,
                 norm_ffn, w_up, conv_w, conv_b, w_down):
    B, S, _ = x.shape
    h = rmsnorm(x, norm_mix)
    proj = h @ w_in
    points = np.cumsum(IN_SIZES)[:-1].tolist()
    qa, ka, va, qb, kb, vb, qc, kc, vc, gates = jnp.split(proj, points, axis=-1)
    heads = lambda t, n: t.reshape(B, S, n, HEAD_DIM)
    o_a = swa_sink_attention(apply_rope(heads(qa, SWA_Q_HEADS), cos, sin),
                             apply_rope(heads(ka, SWA_KV_HEADS), cos, sin),
                             heads(va, SWA_KV_HEADS), sinks)
    o_b = stick_breaking_attention(heads(qb, SB_HEADS), heads(kb, SB_HEADS), heads(vb, SB_HEADS))
    o_c = moba_attention(apply_rope(heads(qc, MOBA_HEADS), cos, sin),
                         apply_rope(heads(kc, MOBA_HEADS), cos, sin),
                         heads(vc, MOBA_HEADS))
    g = jax.nn.sigmoid(gates.reshape(B, S, N_BRANCHES, D_MODEL) + b_gate.reshape(N_BRANCHES, D_MODEL))
    merged = (g[:, :, 0] * (o_a @ w_branch[0])
              + g[:, :, 1] * (o_b @ w_branch[1])
              + g[:, :, 2] * (o_c @ w_branch[2]))
    x = x + merged @ w_out
    h = rmsnorm(x, norm_ffn)
    u = causal_depthwise_conv(h @ w_up, conv_w, conv_b)
    u_a, u_v = jnp.split(u, 2, axis=-1)
    return x + (jax.nn.silu(u_a) * u_v) @ w_down


def setup_inputs(seed: int = 0) -> dict:
    key = jax.random.key(seed)
    ks = jax.random.split(key, 13)
    f32 = jnp.float32
    nrm = lambda k, shape, s: jax.random.normal(k, shape, f32) * s
    return {
        "x": nrm(ks[0], (BATCH, SEQ, D_MODEL), 1.0),
        "norm_mix": 1.0 + nrm(ks[1], (DEPTH, D_MODEL), 0.02),
        "w_in": nrm(ks[2], (DEPTH, D_MODEL, D_IN), D_MODEL ** -0.5),
        "b_gate": nrm(ks[3], (DEPTH, N_BRANCHES * D_MODEL), 0.1),
        "sinks": nrm(ks[4], (DEPTH, SWA_Q_HEADS), 0.5),
        "w_branch": nrm(ks[5], (DEPTH, N_BRANCHES, BRANCH_WIDTH, D_MODEL), BRANCH_WIDTH ** -0.5),
        "w_out": nrm(ks[6], (DEPTH, D_MODEL, D_MODEL), D_MODEL ** -0.5),
        "norm_ffn": 1.0 + nrm(ks[7], (DEPTH, D_MODEL), 0.02),
        "w_up": nrm(ks[8], (DEPTH, D_MODEL, 2 * D_FF), D_MODEL ** -0.5),
        "conv_w": nrm(ks[9], (DEPTH, CONV_WIDTH, 2 * D_FF), CONV_WIDTH ** -0.5),
        "conv_b": nrm(ks[10], (DEPTH, 2 * D_FF), 0.01),
        "w_down": nrm(ks[11], (DEPTH, D_FF, D_MODEL), D_FF ** -0.5),
        "norm_final": 1.0 + nrm(ks[12], (D_MODEL,), 0.02),
    }


def reference(x, norm_mix, w_in, b_gate, sinks, w_branch, w_out, norm_ffn,
              w_up, conv_w, conv_b, w_down, norm_final):
    cos, sin = rope_tables(x.shape[1])
    for layer in range(DEPTH):
        x = hybrid_layer(x, cos, sin, norm_mix[layer], w_in[layer], b_gate[layer],
                         sinks[layer], w_branch[layer], w_out[layer], norm_ffn[layer],
                         w_up[layer], conv_w[layer], conv_b[layer], w_down[layer])
    return rmsnorm(x, norm_final)
```

```python
import functools

import jax
import jax.numpy as jnp
from jax import lax
from jax.experimental import pallas as pl
from jax.experimental.pallas import tpu as pltpu

F32 = jnp.float32
BF16 = jnp.bfloat16

HEAD_DIM = 64
LANES = 128
SWA_Q_HEADS = 8
SWA_KV_HEADS = 2
SWA_WINDOW = 128
SB_HEADS = 8
SB_BLOCK = 256
MOBA_HEADS = 8
MOBA_BLOCK = 256
MOBA_TOPK = 3
MOBA_MAX_BLOCKS = 8
N_BRANCHES = 3
BRANCH_WIDTH = 512
ROPE_THETA = 10000.0
RMS_EPS = 1e-6
SCALE = HEAD_DIM ** -0.5
NEG = -1e30

VMEM_LIMIT = 56 * 1024 * 1024

QKV_COLS = 3840
A_Q, A_K, A_V = 0, 4, 5
B_Q, B_K, B_V = 6, 10, 14
C_Q, C_K, C_V = 18, 22, 26


def _params(*sem):
    return pltpu.CompilerParams(dimension_semantics=sem, vmem_limit_bytes=VMEM_LIMIT)


def _dot(a, b):
    return jnp.dot(a, b, preferred_element_type=F32)


def _dot_nt(a, b):
    return lax.dot_general(a, b, (((1,), (1,)), ((), ())), preferred_element_type=F32)


def _low_lanes():
    return lax.broadcasted_iota(jnp.int32, (1, LANES), 1) < HEAD_DIM


def _rope(a, cos, sin_signed):
    out = []
    lane = lax.broadcasted_iota(jnp.int32, (1, LANES), 1)
    first_half = (lane & 32) == 0
    for g in range(a.shape[1] // LANES):
        piece = a[:, g * LANES:(g + 1) * LANES]
        nxt = pltpu.roll(piece, LANES - 32, axis=1)
        prv = pltpu.roll(piece, 32, axis=1)
        out.append(piece * cos + jnp.where(first_half, nxt, prv) * sin_signed)
    return out


def _proj_kernel(x_ref, g_ref, w_ref, cos_ref, sin_ref, o_ref, h_ref, *, rope_cols):
    j = pl.program_id(1)

    @pl.when(j == 0)
    def _():
        x = x_ref[...]
        ms = jnp.mean(x * x, axis=-1, keepdims=True)
        h_ref[...] = (x * lax.rsqrt(ms + RMS_EPS) * g_ref[...]).astype(BF16)

    acc = _dot(h_ref[...], w_ref[...])
    for nrope in sorted(set(rope_cols)):
        tiles = [t for t, n in enumerate(rope_cols) if n == nrope]
        cond = functools.reduce(jnp.logical_or, [j == t for t in tiles])

        @pl.when(cond)
        def _(nrope=nrope):
            if nrope:
                pieces = _rope(acc[:, :nrope], cos_ref[...], sin_ref[...])
                for g, piece in enumerate(pieces):
                    o_ref[:, g * LANES:(g + 1) * LANES] = piece.astype(o_ref.dtype)
                if nrope < acc.shape[1]:
                    o_ref[:, nrope:] = acc[:, nrope:].astype(o_ref.dtype)
            else:
                o_ref[...] = acc.astype(o_ref.dtype)


def _project(x, gain, w, cos, sin_signed, *, seq, tm, tn, rope_cols, out_dtype):
    t, d = x.shape
    n = w.shape[1]
    assert t % tm == 0 and n % tn == 0 and seq % tm == 0 and len(rope_cols) == n // tn
    pos_tiles = seq // tm
    return pl.pallas_call(
        functools.partial(_proj_kernel, rope_cols=rope_cols),
        grid=(t // tm, n // tn),
        in_specs=[
            pl.BlockSpec((tm, d), lambda i, j: (i, 0)),
            pl.BlockSpec((1, d), lambda i, j: (0, 0)),
            pl.BlockSpec((d, tn), lambda i, j: (0, j)),
            pl.BlockSpec((tm, LANES), lambda i, j: (i % pos_tiles, 0)),
            pl.BlockSpec((tm, LANES), lambda i, j: (i % pos_tiles, 0)),
        ],
        out_specs=pl.BlockSpec((tm, tn), lambda i, j: (i, j)),
        out_shape=jax.ShapeDtypeStruct((t, n), out_dtype),
        scratch_shapes=[pltpu.VMEM((tm, d), BF16)],
        compiler_params=_params("arbitrary", "arbitrary"),
    )(x, gain, w, cos, sin_signed)


def _swa_kernel(q_ref, k_ref, v_ref, sink_ref, o_ref):
    n = pl.program_id(1)
    w = SWA_WINDOW
    start = pl.multiple_of(jnp.maximum(n - 1, 0) * w, w)
    k = k_ref[pl.ds(start, 2 * w), :]
    v = v_ref[pl.ds(start, 2 * w), :]
    low = _low_lanes()
    k_sw = pltpu.roll(k, HEAD_DIM, axis=1)
    v_sw = pltpu.roll(v, HEAD_DIM, axis=1)
    k_at = [[jnp.where(low, k, 0.0).astype(BF16), jnp.where(low, 0.0, k_sw).astype(BF16)],
            [jnp.where(low, k_sw, 0.0).astype(BF16), jnp.where(low, 0.0, k).astype(BF16)]]
    v_at = [[jnp.where(low, v, 0.0).astype(BF16), jnp.where(low, 0.0, v_sw).astype(BF16)],
            [jnp.where(low, v_sw, 0.0).astype(BF16), jnp.where(low, 0.0, v).astype(BF16)]]
    qpos = n * w + lax.broadcasted_iota(jnp.int32, (w, 2 * w), 0)
    kpos = start + lax.broadcasted_iota(jnp.int32, (w, 2 * w), 1)
    diff = qpos - kpos
    valid = (diff >= 0) & (diff < w)
    group = SWA_Q_HEADS // SWA_KV_HEADS
    for p in range(SWA_Q_HEADS // 2):
        g = (2 * p) // group
        qp = (q_ref[:, p * LANES:(p + 1) * LANES] * SCALE).astype(BF16)
        probs, denom = [], []
        for half in range(2):
            s = jnp.where(valid, _dot_nt(qp, k_at[g][half]), NEG)
            sink = sink_ref[2 * p + half:2 * p + half + 1, 0:1]
            m = jnp.maximum(jnp.max(s, axis=-1, keepdims=True), sink)
            e = jnp.exp(s - m)
            denom.append(jnp.sum(e, axis=-1, keepdims=True) + jnp.exp(sink - m))
            probs.append(e.astype(BF16))
        o = _dot(probs[0], v_at[g][0]) + _dot(probs[1], v_at[g][1])
        o = o / jnp.where(low, denom[0], denom[1])
        o_ref[:, p * LANES:(p + 1) * LANES] = o.astype(o_ref.dtype)


def _swa(proj, sink_rows, *, batch, seq):
    w = SWA_WINDOW
    nblk = seq // w
    width = SWA_Q_HEADS * HEAD_DIM
    return pl.pallas_call(
        _swa_kernel,
        grid=(batch, nblk),
        in_specs=[
            pl.BlockSpec((w, width), lambda b, n: (b * nblk + n, A_Q)),
            pl.BlockSpec((seq, LANES), lambda b, n: (b, A_K)),
            pl.BlockSpec((seq, LANES), lambda b, n: (b, A_V)),
            pl.BlockSpec((SWA_Q_HEADS, LANES), lambda b, n: (0, 0)),
        ],
        out_specs=pl.BlockSpec((w, width), lambda b, n: (b * nblk + n, 0)),
        out_shape=jax.ShapeDtypeStruct((batch * seq, width), BF16),
        compiler_params=_params("arbitrary", "arbitrary"),
    )(proj, proj, proj, sink_rows)


def _softplus(z):
    return jnp.maximum(z, 0.0) + jnp.log(1.0 + jnp.exp(-jnp.abs(z)))


def _sb_kernel(q_ref, k_ref, v_ref, o_ref):
    i = pl.program_id(2)
    t = SB_BLOCK
    low = _low_lanes()
    q = (q_ref[...] * SCALE).astype(BF16)
    row = lax.broadcasted_iota(jnp.int32, (t, t), 0)
    col = lax.broadcasted_iota(jnp.int32, (t, t), 1)
    suffix = jnp.where(row >= col, 1.0, 0.0).astype(BF16)
    past = col < row

    def block(j, carry, diagonal):
        acc, remain = carry
        off = pl.multiple_of(j * t, t)
        k = k_ref[pl.ds(off, t), :]
        v = v_ref[pl.ds(off, t), :]
        out = acc
        new_remain = []
        for half in range(2):
            mine = low if half == 0 else jnp.logical_not(low)
            z = _dot_nt(q, jnp.where(mine, k, 0.0).astype(BF16))
            sp = _softplus(z)
            if diagonal:
                sp = jnp.where(past, sp, 0.0)
            hi = sp.astype(BF16)
            lo = (sp - hi.astype(F32)).astype(BF16)
            c = _dot(hi, suffix) + _dot(lo, suffix)
            logw = z - c - remain[half]
            if diagonal:
                logw = jnp.where(past, logw, NEG)
            wgt = jnp.exp(logw).astype(BF16)
            out = out + _dot(wgt, jnp.where(mine, v, 0.0).astype(BF16))
            new_remain.append(remain[half] + c[:, 0:1])
        return out, tuple(new_remain)

    zero_col = jnp.zeros((t, 1), F32)
    carry = block(i, (jnp.zeros((t, LANES), F32), (zero_col, zero_col)), True)
    carry = lax.fori_loop(0, i, lambda s, c: block(i - 1 - s, c, False), carry)
    o_ref[...] = carry[0].astype(o_ref.dtype)


def _stick_breaking(proj, *, batch, seq):
    t = SB_BLOCK
    nq = seq // t
    pairs = SB_HEADS // 2
    return pl.pallas_call(
        _sb_kernel,
        grid=(batch, pairs, nq),
        in_specs=[
            pl.BlockSpec((t, LANES), lambda b, p, i: (b * nq + i, B_Q + p)),
            pl.BlockSpec((seq, LANES), lambda b, p, i: (b, B_K + p)),
            pl.BlockSpec((seq, LANES), lambda b, p, i: (b, B_V + p)),
        ],
        out_specs=pl.BlockSpec((t, LANES), lambda b, p, i: (b * nq + i, p)),
        out_shape=jax.ShapeDtypeStruct((batch * seq, SB_HEADS * HEAD_DIM), BF16),
        compiler_params=_params("arbitrary", "arbitrary", "arbitrary"),
    )(proj, proj, proj)


def _split_bf16(x):
    hi = x.astype(BF16)
    return hi, (x - hi.astype(F32)).astype(BF16)


def _moba_kernel(q_ref, k_ref, v_ref, o_ref, kmean_ref, *, nblocks):
    i = pl.program_id(2)
    t = MOBA_BLOCK
    nbp = MOBA_MAX_BLOCKS
    low = _low_lanes()

    @pl.when(i == 0)
    def _():
        rows = [jnp.mean(k_ref[j * t:(j + 1) * t, :], axis=0, keepdims=True) for j in range(nblocks)]
        rows += [jnp.zeros((1, LANES), F32)] * (nbp - nblocks)
        km = jnp.concatenate(rows, axis=0)
        kmean_ref[0:nbp, :] = jnp.where(low, km, 0.0)
        kmean_ref[nbp:2 * nbp, :] = jnp.where(low, 0.0, km)

    q32 = q_ref[...]
    q_hi, q_lo = _split_bf16(q32)
    m_hi, m_lo = _split_bf16(kmean_ref[...])
    gate_t = _dot_nt(m_hi, q_hi) + (_dot_nt(m_hi, q_lo) + _dot_nt(m_lo, q_hi))

    ksel = min(MOBA_TOPK, nblocks)
    blk = lax.broadcasted_iota(jnp.int32, (nbp, t), 0)
    sel_rows = []
    for half in range(2):
        g = jnp.where(blk < i, gate_t[half * nbp:(half + 1) * nbp, :], NEG)
        sel = jnp.zeros((nbp, t), F32)
        for j in range(nblocks):
            gj = g[j:j + 1, :]
            beats = (g > gj) | ((g == gj) & (blk < j))
            count = jnp.sum(jnp.where(beats, 1.0, 0.0), axis=0, keepdims=True)
            chosen = jnp.where((count < ksel) & (j < i), 1.0, 0.0)
            sel = jnp.where(blk == j, chosen, sel)
        sel_rows.append(sel)
    sel_rows.append(jnp.zeros((LANES - 2 * nbp, t), F32))
    sel_t = jnp.concatenate(sel_rows, axis=0).astype(BF16)
    eye = jnp.where(lax.broadcasted_iota(jnp.int32, (t, t), 0) == lax.broadcasted_iota(jnp.int32, (t, t), 1),
                    1.0, 0.0).astype(BF16)
    sel_q = _dot_nt(eye, sel_t)
    lane = lax.broadcasted_iota(jnp.int32, (1, LANES), 1)

    q = (q32 * SCALE).astype(BF16)
    row = lax.broadcasted_iota(jnp.int32, (t, t), 0)
    col = lax.broadcasted_iota(jnp.int32, (t, t), 1)
    causal = col <= row

    def block(j, carry, diagonal):
        acc, stats = carry
        off = pl.multiple_of(j * t, t)
        k = k_ref[pl.ds(off, t), :]
        v = v_ref[pl.ds(off, t), :]
        pv, alphas, new_stats = [], [], []
        for half in range(2):
            mine = low if half == 0 else jnp.logical_not(low)
            s = _dot_nt(q, jnp.where(mine, k, 0.0).astype(BF16))
            if diagonal:
                s = jnp.where(causal, s, NEG)
            else:
                picked = jnp.sum(jnp.where(lane == j + half * nbp, sel_q, 0.0), axis=-1, keepdims=True)
                s = jnp.where(picked > 0.5, s, NEG)
            m_old, l_old = stats[half]
            m_new = jnp.maximum(m_old, jnp.max(s, axis=-1, keepdims=True))
            alpha = jnp.exp(m_old - m_new)
            e = jnp.exp(s - m_new)
            new_stats.append((m_new, alpha * l_old + jnp.sum(e, axis=-1, keepdims=True)))
            alphas.append(alpha)
            pv.append(_dot(e.astype(BF16), jnp.where(mine, v, 0.0).astype(BF16)))
        acc = acc * jnp.where(low, alphas[0], alphas[1]) + (pv[0] + pv[1])
        return acc, tuple(new_stats)

    init = (jnp.full((t, 1), NEG, F32), jnp.zeros((t, 1), F32))
    carry = block(i, (jnp.zeros((t, LANES), F32), (init, init)), True)
    carry = lax.fori_loop(0, i, lambda j, c: block(j, c, False), carry)
    acc, stats = carry
    o_ref[...] = (acc / jnp.where(low, stats[0][1], stats[1][1])).astype(o_ref.dtype)


def _moba(proj, *, batch, seq):
    t = MOBA_BLOCK
    nq = seq // t
    assert seq % t == 0 and nq <= MOBA_MAX_BLOCKS
    pairs = MOBA_HEADS // 2
    return pl.pallas_call(
        functools.partial(_moba_kernel, nblocks=nq),
        grid=(batch, pairs, nq),
        in_specs=[
            pl.BlockSpec((t, LANES), lambda b, p, i: (b * nq + i, C_Q + p)),
            pl.BlockSpec((seq, LANES), lambda b, p, i: (b, C_K + p)),
            pl.BlockSpec((seq, LANES), lambda b, p, i: (b, C_V + p)),
        ],
        out_specs=pl.BlockSpec((t, LANES), lambda b, p, i: (b * nq + i, p)),
        out_shape=jax.ShapeDtypeStruct((batch * seq, MOBA_HEADS * HEAD_DIM), BF16),
        scratch_shapes=[pltpu.VMEM((2 * MOBA_MAX_BLOCKS, LANES), F32)],
        compiler_params=_params("arbitrary", "arbitrary", "arbitrary"),
    )(proj, proj, proj)


def _merge_kernel(oa_ref, ob_ref, oc_ref, gate_ref, bias_ref, wb_ref, wo_ref, x_ref, o_ref):
    d = x_ref.shape[1]
    merged = None
    for n, branch in enumerate((oa_ref, ob_ref, oc_ref)):
        y = _dot(branch[...], wb_ref[n])
        pre = gate_ref[:, n * d:(n + 1) * d] + bias_ref[:, n * d:(n + 1) * d]
        term = y / (1.0 + jnp.exp(-pre))
        merged = term if merged is None else merged + term
    o_ref[...] = x_ref[...] + _dot(merged.astype(BF16), wo_ref[...])


def _merge(oa, ob, oc, gates, bias, wb, wo, x, *, tm):
    t, d = x.shape
    bw = oa.shape[1]
    row = lambda i: (i, 0)
    fixed2 = lambda i: (0, 0)
    return pl.pallas_call(
        _merge_kernel,
        grid=(t // tm,),
        in_specs=[
            pl.BlockSpec((tm, bw), row),
            pl.BlockSpec((tm, bw), row),
            pl.BlockSpec((tm, bw), row),
            pl.BlockSpec((tm, N_BRANCHES * d), row),
            pl.BlockSpec((1, N_BRANCHES * d), fixed2),
            pl.BlockSpec((N_BRANCHES, bw, d), lambda i: (0, 0, 0)),
            pl.BlockSpec((d, d), fixed2),
            pl.BlockSpec((tm, d), row),
        ],
        out_specs=pl.BlockSpec((tm, d), row),
        out_shape=jax.ShapeDtypeStruct((t, d), F32),
        compiler_params=_params("arbitrary"),
    )(oa, ob, oc, gates, bias, wb, wo, x)


def _shift_rows(u, tail, shift):
    rolled = pltpu.roll(u, shift, axis=0)
    head = jnp.where(lax.broadcasted_iota(jnp.int32, tail.shape, 0) < shift,
                     pltpu.roll(tail, shift, axis=0), rolled[:8])
    return jnp.concatenate([head, rolled[8:]], axis=0)


def _ffn_kernel(x_ref, g_ref, wup_ref, cw_ref, cb_ref, wdn_ref, gf_ref, o_ref, tail_ref, acc_ref,
                *, tiles_per_seq, chunk, final_norm):
    i = pl.program_id(0)
    tm = x_ref.shape[0]
    dff = wdn_ref.shape[0]

    @pl.when(i % tiles_per_seq == 0)
    def _():
        tail_ref[...] = jnp.zeros(tail_ref.shape, F32)

    x = x_ref[...]
    ms = jnp.mean(x * x, axis=-1, keepdims=True)
    h = (x * lax.rsqrt(ms + RMS_EPS) * g_ref[...]).astype(BF16)

    def conv(c0):
        u = _dot(h, wup_ref[:, c0:c0 + chunk])
        tail = tail_ref[:, c0:c0 + chunk]
        tail_ref[:, c0:c0 + chunk] = u[tm - 8:, :]
        out = cb_ref[:, c0:c0 + chunk] + cw_ref[0:1, c0:c0 + chunk] * _shift_rows(u, tail, 2)
        out = out + cw_ref[1:2, c0:c0 + chunk] * _shift_rows(u, tail, 1)
        return out + cw_ref[2:3, c0:c0 + chunk] * u

    for c in range(dff // chunk):
        ua = conv(c * chunk)
        uv = conv(dff + c * chunk)
        act = (ua / (1.0 + jnp.exp(-ua)) * uv).astype(BF16)
        part = _dot(act, wdn_ref[c * chunk:(c + 1) * chunk, :])
        if c == 0:
            acc_ref[...] = x + part
        else:
            acc_ref[...] += part

    y = acc_ref[...]
    if final_norm:
        ms = jnp.mean(y * y, axis=-1, keepdims=True)
        y = y * lax.rsqrt(ms + RMS_EPS) * gf_ref[...]
    o_ref[...] = y


def _ffn(x, gain, wup, cw, cb, wdn, gain_final, *, seq, tm, chunk, final_norm):
    t, d = x.shape
    dff = wdn.shape[0]
    assert seq % tm == 0 and dff % chunk == 0
    fixed = lambda i: (0, 0)
    once = pl.Buffered(1)
    return pl.pallas_call(
        functools.partial(_ffn_kernel, tiles_per_seq=seq // tm, chunk=chunk, final_norm=final_norm),
        grid=(t // tm,),
        in_specs=[
            pl.BlockSpec((tm, d), lambda i: (i, 0)),
            pl.BlockSpec((1, d), fixed),
            pl.BlockSpec((d, 2 * dff), fixed, pipeline_mode=once),
            pl.BlockSpec((3, 2 * dff), fixed),
            pl.BlockSpec((1, 2 * dff), fixed),
            pl.BlockSpec((dff, d), fixed, pipeline_mode=once),
            pl.BlockSpec((1, d), fixed),
        ],
        out_specs=pl.BlockSpec((tm, d), lambda i: (i, 0)),
        out_shape=jax.ShapeDtypeStruct((t, d), F32),
        scratch_shapes=[pltpu.VMEM((8, 2 * dff), F32), pltpu.VMEM((tm, d), F32)],
        compiler_params=_params("arbitrary"),
    )(x, gain, wup, cw, cb, wdn, gain_final)


def _rope_tables(seq):
    inv = ROPE_THETA ** (-jnp.arange(0, HEAD_DIM, 2, dtype=F32) / HEAD_DIM)
    ang = jnp.arange(seq, dtype=F32)[:, None] * inv[None, :]
    cos, sin = jnp.cos(ang), jnp.sin(ang)
    reps = LANES // HEAD_DIM
    return jnp.tile(cos, (1, 2 * reps)), jnp.tile(jnp.concatenate([-sin, sin], axis=1), (1, reps))


def kernel(x, norm_mix, w_in, b_gate, sinks, w_branch, w_out, norm_ffn, w_up, conv_w, conv_b, w_down, norm_final):
    batch, seq, d = x.shape
    depth = w_in.shape[0]
    tokens = batch * seq
    cos, sin_signed = _rope_tables(seq)
    xt = x.reshape(tokens, d)
    tm_proj, tm_tok = min(1024, seq), min(512, seq)
    qkv_rope = (640, 0, 0, 768, 256)
    for layer in range(depth):
        w_l = w_in[layer].astype(BF16)
        gain = norm_mix[layer].reshape(1, d)
        qkv = _project(xt, gain, w_l[:, :QKV_COLS], cos, sin_signed, seq=seq, tm=tm_proj, tn=768,
                       rope_cols=qkv_rope, out_dtype=F32)
        gates = _project(xt, gain, w_l[:, QKV_COLS:], cos, sin_signed, seq=seq, tm=tm_proj, tn=768,
                         rope_cols=(0, 0, 0, 0), out_dtype=F32)
        sink_rows = jnp.broadcast_to(sinks[layer].astype(F32)[:, None], (SWA_Q_HEADS, LANES))
        o_a = _swa(qkv, sink_rows, batch=batch, seq=seq)
        o_b = _stick_breaking(qkv, batch=batch, seq=seq)
        o_c = _moba(qkv, batch=batch, seq=seq)
        xt = _merge(o_a, o_b, o_c, gates, b_gate[layer].reshape(1, N_BRANCHES * d),
                    w_branch[layer].astype(BF16), w_out[layer].astype(BF16), xt, tm=tm_tok)
        xt = _ffn(xt, norm_ffn[layer].reshape(1, d), w_up[layer].astype(BF16), conv_w[layer],
                  conv_b[layer].reshape(1, -1), w_down[layer].astype(BF16), norm_final.reshape(1, d),
                  seq=seq, tm=tm_tok, chunk=256, final_norm=(layer == depth - 1))
    return xt.reshape(batch, seq, d)
```

```python
import functools

import jax
import jax.numpy as jnp
from jax import lax
from jax.experimental import pallas as pl
from jax.experimental.pallas import tpu as pltpu

F32 = jnp.float32
BF16 = jnp.bfloat16

HEAD_DIM = 64
LANES = 128
HEADS = 8
PAIRS = HEADS // 2
WIDTH = HEADS * HEAD_DIM
SWA_KV_HEADS = 2
SWA_WINDOW = 128
SB_BLOCK = 256
MOBA_BLOCK = 256
MOBA_TOPK = 3
MOBA_MAX_BLOCKS = 8
N_BRANCHES = 3
ROPE_THETA = 10000.0
RMS_EPS = 1e-6
SCALE = HEAD_DIM ** -0.5
LOG2E = 1.4426950408889634
NEG = -1e30

VMEM_LIMIT = 56 * 1024 * 1024

QA, QB, KB, VB, QC, KC, VC = range(7)
KA, VA = 7 * PAIRS, 7 * PAIRS + 1


def _params(*sem):
    return pltpu.CompilerParams(dimension_semantics=sem, vmem_limit_bytes=VMEM_LIMIT)


def _dot(a, b):
    return jnp.dot(a, b, preferred_element_type=F32)


def _dot_nt(a, b):
    return lax.dot_general(a, b, (((1,), (1,)), ((), ())), preferred_element_type=F32)


def _lane():
    return lax.broadcasted_iota(jnp.int32, (1, LANES), 1)


def _one_hot_lane(index):
    return jnp.where(_lane() == index, 1.0, 0.0).astype(BF16)


def _bf16_zero():
    return jnp.zeros((1, LANES), BF16)


def _rope(piece, cos, sin_signed):
    first_half = (_lane() & 32) == 0
    nxt = pltpu.roll(piece, LANES - 32, axis=1)
    prv = pltpu.roll(piece, 32, axis=1)
    return piece * cos + jnp.where(first_half, nxt, prv) * sin_signed


def _proj_kernel(x_ref, g_ref, w_ref, cos_ref, sin_ref, o_ref, h_ref, *, rope_ranges):
    j = pl.program_id(1)

    @pl.when(j == 0)
    def _():
        x = x_ref[...]
        ms = jnp.mean(x * x, axis=-1, keepdims=True)
        h_ref[...] = (x * lax.rsqrt(ms + RMS_EPS) * g_ref[...]).astype(BF16)

    acc = _dot(h_ref[...], w_ref[...])
    tn = acc.shape[1]
    for ranges in sorted(set(rope_ranges)):
        tiles = [t for t, r in enumerate(rope_ranges) if r == ranges]
        cond = functools.reduce(jnp.logical_or, [j == t for t in tiles])

        @pl.when(cond)
        def _(ranges=ranges):
            pos = 0
            for lo, hi in ranges:
                if lo > pos:
                    o_ref[:, pos:lo] = acc[:, pos:lo].astype(o_ref.dtype)
                for c in range(lo, hi, LANES):
                    o_ref[:, c:c + LANES] = _rope(acc[:, c:c + LANES], cos_ref[...], sin_ref[...]).astype(o_ref.dtype)
                pos = hi
            if pos < tn:
                o_ref[:, pos:] = acc[:, pos:].astype(o_ref.dtype)


def _project(x, gain, w, cos, sin_signed, *, seq, tm, tn, rope_ranges, out_dtype):
    t, d = x.shape
    n = w.shape[1]
    assert t % tm == 0 and n % tn == 0 and seq % tm == 0 and len(rope_ranges) == n // tn
    pos_tiles = seq // tm
    return pl.pallas_call(
        functools.partial(_proj_kernel, rope_ranges=rope_ranges),
        grid=(t // tm, n // tn),
        in_specs=[
            pl.BlockSpec((tm, d), lambda i, j: (i, 0)),
            pl.BlockSpec((1, d), lambda i, j: (0, 0)),
            pl.BlockSpec((d, tn), lambda i, j: (0, j)),
            pl.BlockSpec((tm, LANES), lambda i, j: (i % pos_tiles, 0)),
            pl.BlockSpec((tm, LANES), lambda i, j: (i % pos_tiles, 0)),
        ],
        out_specs=pl.BlockSpec((tm, tn), lambda i, j: (i, j)),
        out_shape=jax.ShapeDtypeStruct((t, n), out_dtype),
        scratch_shapes=[pltpu.VMEM((tm, d), BF16)],
        compiler_params=_params("arbitrary", "arbitrary"),
    )(x, gain, w, cos, sin_signed)


def _swa_kernel(q_ref, k_ref, v_ref, sink_ref, o_ref):
    n = pl.program_id(1)
    w = SWA_WINDOW
    start = pl.multiple_of(jnp.maximum(n - 1, 0) * w, w)
    k = k_ref[pl.ds(start, 2 * w), :].astype(F32)
    v = v_ref[pl.ds(start, 2 * w), :].astype(F32)
    low = _lane() < HEAD_DIM
    k_sw = pltpu.roll(k, HEAD_DIM, axis=1)
    v_sw = pltpu.roll(v, HEAD_DIM, axis=1)
    k_at = [[jnp.where(low, k, 0.0).astype(BF16), jnp.where(low, 0.0, k_sw).astype(BF16)],
            [jnp.where(low, k_sw, 0.0).astype(BF16), jnp.where(low, 0.0, k).astype(BF16)]]
    v_at = [[jnp.where(low, v, 1.0).astype(BF16), jnp.where(low, 1.0, v_sw).astype(BF16)],
            [jnp.where(low, v_sw, 1.0).astype(BF16), jnp.where(low, 1.0, v).astype(BF16)]]
    qpos = n * w + (lax.broadcasted_iota(jnp.int32, (2 * w, 2 * w), 0) & (w - 1))
    kpos = start + lax.broadcasted_iota(jnp.int32, (2 * w, 2 * w), 1)
    diff = qpos - kpos
    valid = (diff >= 0) & (diff < w)
    upper = lax.broadcasted_iota(jnp.int32, (2 * w, LANES), 0) < w
    chains = [(g, side) for g in range(SWA_KV_HEADS) for side in range(2)]
    qs = [jnp.concatenate([q_ref[:, (2 * g) * LANES:(2 * g + 1) * LANES],
                           q_ref[:, (2 * g + 1) * LANES:(2 * g + 2) * LANES]], axis=0) * SCALE
          for g in range(SWA_KV_HEADS)]
    scores = [jnp.where(valid, _dot_nt(qs[g], k_at[g][side]), NEG) for g, side in chains]
    sinks, maxes = [], []
    for (g, side), s in zip(chains, scores):
        h0 = 4 * g + side
        sink = jnp.where(upper, sink_ref[h0:h0 + 1, :], sink_ref[h0 + 2:h0 + 3, :])
        sinks.append(sink)
        maxes.append(jnp.maximum(jnp.broadcast_to(jnp.max(s, axis=-1, keepdims=True), sink.shape), sink))
    probs = [jnp.exp(s - jnp.concatenate([m, m], axis=1)).astype(BF16) for s, m in zip(scores, maxes)]
    outs = [_dot(p, v_at[g][side]) for (g, side), p in zip(chains, probs)]
    for g in range(SWA_KV_HEADS):
        o_lo, o_hi = outs[2 * g], outs[2 * g + 1]
        denom = pltpu.roll(jnp.where(low, o_hi, o_lo), HEAD_DIM, axis=1)
        denom = denom + jnp.where(low, jnp.exp(sinks[2 * g] - maxes[2 * g]), jnp.exp(sinks[2 * g + 1] - maxes[2 * g + 1]))
        o = (jnp.where(low, o_lo, o_hi) / denom).astype(o_ref.dtype)
        o_ref[:, (2 * g) * LANES:(2 * g + 1) * LANES] = o[:w]
        o_ref[:, (2 * g + 1) * LANES:(2 * g + 2) * LANES] = o[w:]


def _swa(qkv, sink_rows, *, batch, seq):
    w = SWA_WINDOW
    nblk = seq // w
    return pl.pallas_call(
        _swa_kernel,
        grid=(batch, nblk),
        in_specs=[
            pl.BlockSpec((w, WIDTH), lambda b, n: (b * nblk + n, QA)),
            pl.BlockSpec((seq, LANES), lambda b, n: (b, KA)),
            pl.BlockSpec((seq, LANES), lambda b, n: (b, VA)),
            pl.BlockSpec((HEADS, LANES), lambda b, n: (0, 0)),
        ],
        out_specs=pl.BlockSpec((w, WIDTH), lambda b, n: (b * nblk + n, 0)),
        out_shape=jax.ShapeDtypeStruct((batch * seq, WIDTH), BF16),
        compiler_params=_params("arbitrary", "arbitrary"),
    )(qkv, qkv, qkv, sink_rows)


def _softplus(z):
    return jnp.maximum(z, 0.0) + jnp.log(1.0 + jnp.exp2(jnp.abs(z) * -LOG2E))


def _sb_kernel(q_ref, k_ref, v_ref, o_ref):
    i = pl.program_id(1)
    t = SB_BLOCK
    low = _lane() < HEAD_DIM
    mine = [low, jnp.logical_not(low)]
    row = lax.broadcasted_iota(jnp.int32, (t, t), 0)
    col = lax.broadcasted_iota(jnp.int32, (t, t), 1)
    suffix = jnp.where(row >= col, 1.0, 0.0).astype(BF16)
    past = col < row
    chains = [(p, side) for p in range(PAIRS) for side in range(2)]
    qs = [q_ref[:, p * LANES:(p + 1) * LANES] * SCALE for p in range(PAIRS)]
    zero = _bf16_zero()

    def block(j, carry, diagonal):
        accs, remains = carry
        off = pl.multiple_of(j * t, t)
        ks = [k_ref[pl.ds(off, t), p * LANES:(p + 1) * LANES] for p in range(PAIRS)]
        vs = [v_ref[pl.ds(off, t), p * LANES:(p + 1) * LANES] for p in range(PAIRS)]
        zs = [_dot_nt(qs[p], jnp.where(mine[side], ks[p], zero)) for p, side in chains]
        sps = []
        for z in zs:
            sp = _softplus(z)
            sps.append(jnp.where(past, sp, 0.0) if diagonal else sp)
        cs = [_dot(sp.astype(BF16), suffix) for sp in sps]
        wgts = []
        for n, (z, c) in enumerate(zip(zs, cs)):
            logw = z - c - remains[n]
            if diagonal:
                logw = jnp.where(past, logw, NEG)
            wgts.append(jnp.exp(logw).astype(BF16))
        pvs = [_dot(wgts[n], jnp.where(mine[side], vs[p], zero)) for n, (p, side) in enumerate(chains)]
        new_accs = tuple(accs[p] + (pvs[2 * p] + pvs[2 * p + 1]) for p in range(PAIRS))
        new_remains = tuple(remains[n] + cs[n][:, 0:1] for n in range(len(chains)))
        return new_accs, new_remains

    init = (tuple(jnp.zeros((t, LANES), F32) for _ in range(PAIRS)),
            tuple(jnp.zeros((t, 1), F32) for _ in chains))
    carry = block(i, init, True)
    carry = lax.fori_loop(0, i, lambda s, c: block(i - 1 - s, c, False), carry)
    for p in range(PAIRS):
        o_ref[:, p * LANES:(p + 1) * LANES] = carry[0][p].astype(o_ref.dtype)


def _stick_breaking(qkv, *, batch, seq):
    t = SB_BLOCK
    nq = seq // t
    return pl.pallas_call(
        _sb_kernel,
        grid=(batch, nq),
        in_specs=[
            pl.BlockSpec((t, WIDTH), lambda b, i: (b * nq + i, QB)),
            pl.BlockSpec((seq, WIDTH), lambda b, i: (b, KB)),
            pl.BlockSpec((seq, WIDTH), lambda b, i: (b, VB)),
        ],
        out_specs=pl.BlockSpec((t, WIDTH), lambda b, i: (b * nq + i, 0)),
        out_shape=jax.ShapeDtypeStruct((batch * seq, WIDTH), BF16),
        compiler_params=_params("arbitrary", "arbitrary"),
    )(qkv, qkv, qkv)


def _moba_kernel(q_ref, k_ref, v_ref, o_ref, kmean_ref, *, nblocks):
    i = pl.program_id(1)
    t = MOBA_BLOCK
    nbp = MOBA_MAX_BLOCKS
    low = _lane() < HEAD_DIM
    mine = [low, jnp.logical_not(low)]
    chains = [(p, side) for p in range(PAIRS) for side in range(2)]

    @pl.when(i == 0)
    def _():
        for p in range(PAIRS):
            rows = [jnp.mean(k_ref[j * t:(j + 1) * t, p * LANES:(p + 1) * LANES].astype(F32), axis=0, keepdims=True)
                    for j in range(nblocks)]
            rows += [jnp.zeros((1, LANES), F32)] * (nbp - nblocks)
            km = jnp.concatenate(rows, axis=0)
            kmean_ref[2 * p * nbp:(2 * p + 1) * nbp, :] = jnp.where(low, km, 0.0)
            kmean_ref[(2 * p + 1) * nbp:(2 * p + 2) * nbp, :] = jnp.where(low, 0.0, km)

    ksel = min(MOBA_TOPK, nblocks)
    blk = lax.broadcasted_iota(jnp.int32, (nbp, t), 0)
    eye = jnp.where(lax.broadcasted_iota(jnp.int32, (t, t), 0) == lax.broadcasted_iota(jnp.int32, (t, t), 1),
                    1.0, 0.0).astype(BF16)
    filler = jnp.zeros((HEAD_DIM - nbp, t), F32)
    q_aug = []
    for p in range(PAIRS):
        q = q_ref[:, p * LANES:(p + 1) * LANES]
        km = kmean_ref[2 * p * nbp:(2 * p + 2) * nbp, :]
        km_hi = km.astype(BF16)
        km_lo = (km - km_hi.astype(F32)).astype(BF16)
        gate_t = _dot_nt(km_hi, q) + _dot_nt(km_lo, q)
        bias_rows = []
        for side in range(2):
            g = jnp.where(blk < i, gate_t[side * nbp:(side + 1) * nbp, :], NEG)
            bias = jnp.zeros((nbp, t), F32)
            for j in range(nblocks):
                gj = g[j:j + 1, :]
                beats = (g > gj) | ((g == gj) & (blk < j))
                count = jnp.sum(jnp.where(beats, 1.0, 0.0), axis=0, keepdims=True)
                chosen = (count < ksel) & (j < i)
                bias = jnp.where(blk == j, jnp.where(chosen, 0.0, NEG), bias)
            bias_rows.append(bias)
        bias_t = jnp.concatenate([bias_rows[1], filler, bias_rows[0], filler], axis=0).astype(BF16)
        bias_q = _dot_nt(eye, bias_t).astype(BF16)
        qs = q * SCALE
        q_aug.append(jnp.where(low, qs, bias_q))
        q_aug.append(jnp.where(low, bias_q, qs))

    row = lax.broadcasted_iota(jnp.int32, (t, t), 0)
    col = lax.broadcasted_iota(jnp.int32, (t, t), 1)
    causal = col <= row
    ones = jnp.ones((1, LANES), BF16)
    zero = _bf16_zero()

    def block(j, carry, diagonal):
        accs, maxes = carry
        off = pl.multiple_of(j * t, t)
        ks = [k_ref[pl.ds(off, t), p * LANES:(p + 1) * LANES] for p in range(PAIRS)]
        vs = [v_ref[pl.ds(off, t), p * LANES:(p + 1) * LANES] for p in range(PAIRS)]
        if diagonal:
            fill = [zero, zero]
        else:
            fill = [_one_hot_lane(HEAD_DIM + j), _one_hot_lane(j)]
        scores = [_dot_nt(q_aug[n], jnp.where(mine[side], ks[p], fill[side])) for n, (p, side) in enumerate(chains)]
        if diagonal:
            scores = [jnp.where(causal, s, NEG) for s in scores]
        new_max = [jnp.maximum(m, jnp.max(s, axis=-1, keepdims=True)) for m, s in zip(maxes, scores)]
        probs = [jnp.exp(s - m).astype(BF16) for s, m in zip(scores, new_max)]
        pvs = [_dot(probs[n], jnp.where(mine[side], vs[p], ones)) for n, (p, side) in enumerate(chains)]
        new_accs = tuple(a * jnp.exp(mo - mn) + pv for a, mo, mn, pv in zip(accs, maxes, new_max, pvs))
        return new_accs, tuple(new_max)

    init = (tuple(jnp.zeros((t, LANES), F32) for _ in chains), tuple(jnp.full((t, 1), NEG, F32) for _ in chains))
    carry = block(i, init, True)
    carry = lax.fori_loop(0, i, lambda j, c: block(j, c, False), carry)
    accs = carry[0]
    for p in range(PAIRS):
        a_lo, a_hi = accs[2 * p], accs[2 * p + 1]
        denom = pltpu.roll(jnp.where(low, a_hi, a_lo), HEAD_DIM, axis=1)
        o = jnp.where(low, a_lo, a_hi) / denom
        o_ref[:, p * LANES:(p + 1) * LANES] = o.astype(o_ref.dtype)


def _moba(qkv, *, batch, seq):
    t = MOBA_BLOCK
    nq = seq // t
    assert seq % t == 0 and nq <= MOBA_MAX_BLOCKS
    return pl.pallas_call(
        functools.partial(_moba_kernel, nblocks=nq),
        grid=(batch, nq),
        in_specs=[
            pl.BlockSpec((t, WIDTH), lambda b, i: (b * nq + i, QC)),
            pl.BlockSpec((seq, WIDTH), lambda b, i: (b, KC)),
            pl.BlockSpec((seq, WIDTH), lambda b, i: (b, VC)),
        ],
        out_specs=pl.BlockSpec((t, WIDTH), lambda b, i: (b * nq + i, 0)),
        out_shape=jax.ShapeDtypeStruct((batch * seq, WIDTH), BF16),
        scratch_shapes=[pltpu.VMEM((HEADS * MOBA_MAX_BLOCKS, LANES), F32)],
        compiler_params=_params("arbitrary", "arbitrary"),
    )(qkv, qkv, qkv)


def _merge_kernel(oa_ref, ob_ref, oc_ref, gate_ref, bias_ref, wb_ref, wo_ref, x_ref, o_ref):
    d = x_ref.shape[1]
    ys = [_dot(branch[...], wb_ref[n]) for n, branch in enumerate((oa_ref, ob_ref, oc_ref))]
    merged = None
    for n, y in enumerate(ys):
        pre = gate_ref[:, n * d:(n + 1) * d] + bias_ref[:, n * d:(n + 1) * d]
        term = y / (1.0 + jnp.exp(-pre))
        merged = term if merged is None else merged + term
    o_ref[...] = x_ref[...] + _dot(merged.astype(BF16), wo_ref[...])


def _merge(oa, ob, oc, gates, bias, wb, wo, x, *, tm):
    t, d = x.shape
    bw = oa.shape[1]
    row = lambda i: (i, 0)
    fixed2 = lambda i: (0, 0)
    return pl.pallas_call(
        _merge_kernel,
        grid=(t // tm,),
        in_specs=[
            pl.BlockSpec((tm, bw), row),
            pl.BlockSpec((tm, bw), row),
            pl.BlockSpec((tm, bw), row),
            pl.BlockSpec((tm, N_BRANCHES * d), row),
            pl.BlockSpec((1, N_BRANCHES * d), fixed2),
            pl.BlockSpec((N_BRANCHES, bw, d), lambda i: (0, 0, 0)),
            pl.BlockSpec((d, d), fixed2),
            pl.BlockSpec((tm, d), row),
        ],
        out_specs=pl.BlockSpec((tm, d), row),
        out_shape=jax.ShapeDtypeStruct((t, d), F32),
        compiler_params=_params("arbitrary"),
    )(oa, ob, oc, gates, bias, wb, wo, x)


def _shift_rows(u, tail, shift):
    rolled = pltpu.roll(u, shift, axis=0)
    head = jnp.where(lax.broadcasted_iota(jnp.int32, tail.shape, 0) < shift,
                     pltpu.roll(tail, shift, axis=0), rolled[:8])
    return jnp.concatenate([head, rolled[8:]], axis=0)


def _ffn_kernel(x_ref, g_ref, wup_ref, cw_ref, cb_ref, wdn_ref, gf_ref, o_ref, tail_ref, acc_ref,
                *, tiles_per_seq, chunk, final_norm):
    i = pl.program_id(0)
    tm = x_ref.shape[0]
    dff = wdn_ref.shape[0]

    @pl.when(i % tiles_per_seq == 0)
    def _():
        tail_ref[...] = jnp.zeros(tail_ref.shape, F32)

    x = x_ref[...]
    ms = jnp.mean(x * x, axis=-1, keepdims=True)
    h = (x * lax.rsqrt(ms + RMS_EPS) * g_ref[...]).astype(BF16)

    def up(c):
        return (_dot(h, wup_ref[:, c * chunk:(c + 1) * chunk]),
                _dot(h, wup_ref[:, dff + c * chunk:dff + (c + 1) * chunk]))

    def conv(u, c0):
        tail = tail_ref[:, c0:c0 + chunk]
        tail_ref[:, c0:c0 + chunk] = u[tm - 8:, :]
        out = cb_ref[:, c0:c0 + chunk] + cw_ref[0:1, c0:c0 + chunk] * _shift_rows(u, tail, 2)
        out = out + cw_ref[1:2, c0:c0 + chunk] * _shift_rows(u, tail, 1)
        return out + cw_ref[2:3, c0:c0 + chunk] * u

    nchunks = dff // chunk
    pre = up(0)
    for c in range(nchunks):
        cur = pre
        if c + 1 < nchunks:
            pre = up(c + 1)
        ua = conv(cur[0], c * chunk)
        uv = conv(cur[1], dff + c * chunk)
        act = (ua / (1.0 + jnp.exp(-ua)) * uv).astype(BF16)
        part = _dot(act, wdn_ref[c * chunk:(c + 1) * chunk, :])
        if c == 0:
            acc_ref[...] = x + part
        else:
            acc_ref[...] += part

    y = acc_ref[...]
    if final_norm:
        ms = jnp.mean(y * y, axis=-1, keepdims=True)
        y = y * lax.rsqrt(ms + RMS_EPS) * gf_ref[...]
    o_ref[...] = y


def _ffn(x, gain, wup, cw, cb, wdn, gain_final, *, seq, tm, chunk, final_norm):
    t, d = x.shape
    dff = wdn.shape[0]
    assert seq % tm == 0 and dff % chunk == 0
    fixed = lambda i: (0, 0)
    once = pl.Buffered(1)
    return pl.pallas_call(
        functools.partial(_ffn_kernel, tiles_per_seq=seq // tm, chunk=chunk, final_norm=final_norm),
        grid=(t // tm,),
        in_specs=[
            pl.BlockSpec((tm, d), lambda i: (i, 0)),
            pl.BlockSpec((1, d), fixed),
            pl.BlockSpec((d, 2 * dff), fixed, pipeline_mode=once),
            pl.BlockSpec((3, 2 * dff), fixed),
            pl.BlockSpec((1, 2 * dff), fixed),
            pl.BlockSpec((dff, d), fixed, pipeline_mode=once),
            pl.BlockSpec((1, d), fixed),
        ],
        out_specs=pl.BlockSpec((tm, d), lambda i: (i, 0)),
        out_shape=jax.ShapeDtypeStruct((t, d), F32),
        scratch_shapes=[pltpu.VMEM((8, 2 * dff), F32), pltpu.VMEM((tm, d), F32)],
        compiler_params=_params("arbitrary"),
    )(x, gain, wup, cw, cb, wdn, gain_final)


def _rope_tables(seq):
    inv = ROPE_THETA ** (-jnp.arange(0, HEAD_DIM, 2, dtype=F32) / HEAD_DIM)
    ang = jnp.arange(seq, dtype=F32)[:, None] * inv[None, :]
    cos, sin = jnp.cos(ang), jnp.sin(ang)
    reps = LANES // HEAD_DIM
    return jnp.tile(cos, (1, 2 * reps)), jnp.tile(jnp.concatenate([-sin, sin], axis=1), (1, reps))


def _qkv_weight(w):
    kv = SWA_KV_HEADS * HEAD_DIM
    a_end = WIDTH + 2 * kv
    return jnp.concatenate([w[:, :WIDTH], w[:, a_end:a_end + 6 * WIDTH], w[:, WIDTH:a_end]], axis=1)


QKV_TILE = 768
QKV_ROPE = (((0, 512),), (), ((512, 768),), ((0, 768),), ((512, 640),))


def kernel(x, norm_mix, w_in, b_gate, sinks, w_branch, w_out, norm_ffn, w_up, conv_w, conv_b, w_down, norm_final):
    batch, seq, d = x.shape
    depth = w_in.shape[0]
    tokens = batch * seq
    cos, sin_signed = _rope_tables(seq)
    xt = x.reshape(tokens, d)
    tm_proj, tm_tok = min(1024, seq), min(512, seq)
    in_qkv = WIDTH + 2 * SWA_KV_HEADS * HEAD_DIM + 6 * WIDTH
    for layer in range(depth):
        gain = norm_mix[layer].reshape(1, d)
        w_qkv = _qkv_weight(w_in[layer][:, :in_qkv]).astype(BF16)
        w_gate = w_in[layer][:, in_qkv:].astype(BF16)
        qkv = _project(xt, gain, w_qkv, cos, sin_signed, seq=seq, tm=tm_proj, tn=QKV_TILE,
                       rope_ranges=QKV_ROPE, out_dtype=BF16)
        gates = _project(xt, gain, w_gate, cos, sin_signed, seq=seq, tm=tm_proj, tn=768,
                         rope_ranges=((),) * (w_gate.shape[1] // 768), out_dtype=F32)
        sink_rows = jnp.broadcast_to(sinks[layer].astype(F32)[:, None], (HEADS, LANES))
        o_a = _swa(qkv, sink_rows, batch=batch, seq=seq)
        o_b = _stick_breaking(qkv, batch=batch, seq=seq)
        o_c = _moba(qkv, batch=batch, seq=seq)
        xt = _merge(o_a, o_b, o_c, gates, b_gate[layer].reshape(1, N_BRANCHES * d),
                    w_branch[layer].astype(BF16), w_out[layer].astype(BF16), xt, tm=tm_tok)
        xt = _ffn(xt, norm_ffn[layer].reshape(1, d), w_up[layer].astype(BF16), conv_w[layer],
                  conv_b[layer].reshape(1, -1), w_down[layer].astype(BF16), norm_final.reshape(1, d),
                  seq=seq, tm=tm_tok, chunk=256, final_norm=(layer == depth - 1))
    return xt.reshape(batch, seq, d)
```

```python
import functools

import jax
import jax.numpy as jnp
from jax import lax
from jax.experimental import pallas as pl
from jax.experimental.pallas import tpu as pltpu

F32 = jnp.float32
BF16 = jnp.bfloat16

HEAD_DIM = 64
LANES = 128
HEADS = 8
PAIRS = HEADS // 2
WIDTH = HEADS * HEAD_DIM
SWA_KV_HEADS = 2
SWA_WINDOW = 128
SB_BLOCK = 256
SB_DONE = 104.0
MOBA_BLOCK = 256
MOBA_TOPK = 3
MOBA_MAX_BLOCKS = 8
N_BRANCHES = 3
ROPE_THETA = 10000.0
RMS_EPS = 1e-6
SCALE = HEAD_DIM ** -0.5
LOG2E = 1.4426950408889634
NEG = -1e30

VMEM_LIMIT = 56 * 1024 * 1024

ROPE_COLS = 3 * WIDTH + LANES
PLAIN_COLS = 4 * WIDTH + LANES
QA, QC, KC = range(3)
KA = 3 * PAIRS
QB, KB, VB, VC = range(4)
VA = 4 * PAIRS


def _params(*sem):
    return pltpu.CompilerParams(dimension_semantics=sem, vmem_limit_bytes=VMEM_LIMIT)


def _dot(a, b):
    return jnp.dot(a, b, preferred_element_type=F32)


def _dot_nt(a, b):
    return lax.dot_general(a, b, (((1,), (1,)), ((), ())), preferred_element_type=F32)


def _lane():
    return lax.broadcasted_iota(jnp.int32, (1, LANES), 1)


def _one_hot_lane(index):
    return jnp.where(_lane() == index, 1.0, 0.0).astype(BF16)


def _bf16_zero():
    return jnp.zeros((1, LANES), BF16)


def _rope(piece, cos, sin_signed):
    first_half = (_lane() & 32) == 0
    nxt = pltpu.roll(piece, LANES - 32, axis=1)
    prv = pltpu.roll(piece, 32, axis=1)
    return piece * cos + jnp.where(first_half, nxt, prv) * sin_signed


def _in_proj_kernel(x_ref, g_ref, w_ref, cos_ref, sin_ref, rope_ref, plain_ref, gate_ref, *, chunk):
    x = x_ref[...]
    ms = jnp.mean(x * x, axis=-1, keepdims=True)
    h = (x * lax.rsqrt(ms + RMS_EPS) * g_ref[...]).astype(BF16)
    cos, sin_signed = cos_ref[...], sin_ref[...]
    base = 0
    for out_ref, rotary in ((rope_ref, True), (plain_ref, False), (gate_ref, False)):
        n = out_ref.shape[1]
        for c0 in range(0, n, chunk):
            c1 = min(c0 + chunk, n)
            acc = _dot(h, w_ref[:, base + c0:base + c1])
            if rotary:
                for c in range(c0, c1, LANES):
                    out_ref[:, c:c + LANES] = _rope(acc[:, c - c0:c - c0 + LANES], cos, sin_signed).astype(out_ref.dtype)
            else:
                out_ref[:, c0:c1] = acc.astype(out_ref.dtype)
        base += n


def _in_project(x, gain, w, cos, sin_signed, *, seq, tm, chunk):
    t, d = x.shape
    n_gate = w.shape[1] - ROPE_COLS - PLAIN_COLS
    assert t % tm == 0 and seq % tm == 0
    pos_tiles = seq // tm
    row = lambda i: (i, 0)
    return pl.pallas_call(
        functools.partial(_in_proj_kernel, chunk=chunk),
        grid=(t // tm,),
        in_specs=[
            pl.BlockSpec((tm, d), row),
            pl.BlockSpec((1, d), lambda i: (0, 0)),
            pl.BlockSpec(w.shape, lambda i: (0, 0), pipeline_mode=pl.Buffered(1)),
            pl.BlockSpec((tm, LANES), lambda i: (i % pos_tiles, 0)),
            pl.BlockSpec((tm, LANES), lambda i: (i % pos_tiles, 0)),
        ],
        out_specs=[pl.BlockSpec((tm, ROPE_COLS), row), pl.BlockSpec((tm, PLAIN_COLS), row),
                   pl.BlockSpec((tm, n_gate), row)],
        out_shape=[jax.ShapeDtypeStruct((t, ROPE_COLS), BF16), jax.ShapeDtypeStruct((t, PLAIN_COLS), BF16),
                   jax.ShapeDtypeStruct((t, n_gate), F32)],
        compiler_params=_params("arbitrary"),
    )(x, gain, w, cos, sin_signed)


def _swa_kernel(q_ref, k_ref, v_ref, sink_ref, o_ref):
    n = pl.program_id(1)
    w = SWA_WINDOW
    start = pl.multiple_of(jnp.maximum(n - 1, 0) * w, w)
    k = k_ref[pl.ds(start, 2 * w), :].astype(F32)
    v = v_ref[pl.ds(start, 2 * w), :].astype(F32)
    low = _lane() < HEAD_DIM
    k_sw = pltpu.roll(k, HEAD_DIM, axis=1)
    v_sw = pltpu.roll(v, HEAD_DIM, axis=1)
    k_at = [[jnp.where(low, k, 0.0).astype(BF16), jnp.where(low, 0.0, k_sw).astype(BF16)],
            [jnp.where(low, k_sw, 0.0).astype(BF16), jnp.where(low, 0.0, k).astype(BF16)]]
    v_at = [[jnp.where(low, v, 1.0).astype(BF16), jnp.where(low, 1.0, v_sw).astype(BF16)],
            [jnp.where(low, v_sw, 1.0).astype(BF16), jnp.where(low, 1.0, v).astype(BF16)]]
    qpos = n * w + (lax.broadcasted_iota(jnp.int32, (2 * w, 2 * w), 0) & (w - 1))
    kpos = start + lax.broadcasted_iota(jnp.int32, (2 * w, 2 * w), 1)
    diff = qpos - kpos
    valid = (diff >= 0) & (diff < w)
    upper = lax.broadcasted_iota(jnp.int32, (2 * w, LANES), 0) < w
    chains = [(g, side) for g in range(SWA_KV_HEADS) for side in range(2)]
    qs = [jnp.concatenate([q_ref[:, (2 * g) * LANES:(2 * g + 1) * LANES],
                           q_ref[:, (2 * g + 1) * LANES:(2 * g + 2) * LANES]], axis=0) * SCALE
          for g in range(SWA_KV_HEADS)]
    scores = [jnp.where(valid, _dot_nt(qs[g], k_at[g][side]), NEG) for g, side in chains]
    sinks, maxes = [], []
    for (g, side), s in zip(chains, scores):
        h0 = 4 * g + side
        sink = jnp.where(upper, sink_ref[h0:h0 + 1, :], sink_ref[h0 + 2:h0 + 3, :])
        sinks.append(sink)
        maxes.append(jnp.maximum(jnp.broadcast_to(jnp.max(s, axis=-1, keepdims=True), sink.shape), sink))
    probs = [jnp.exp(s - jnp.concatenate([m, m], axis=1)).astype(BF16) for s, m in zip(scores, maxes)]
    outs = [_dot(p, v_at[g][side]) for (g, side), p in zip(chains, probs)]
    for g in range(SWA_KV_HEADS):
        o_lo, o_hi = outs[2 * g], outs[2 * g + 1]
        denom = pltpu.roll(jnp.where(low, o_hi, o_lo), HEAD_DIM, axis=1)
        denom = denom + jnp.where(low, jnp.exp(sinks[2 * g] - maxes[2 * g]), jnp.exp(sinks[2 * g + 1] - maxes[2 * g + 1]))
        o = (jnp.where(low, o_lo, o_hi) / denom).astype(o_ref.dtype)
        o_ref[:, (2 * g) * LANES:(2 * g + 1) * LANES] = o[:w]
        o_ref[:, (2 * g + 1) * LANES:(2 * g + 2) * LANES] = o[w:]


def _swa(roped, plain, sink_rows, *, batch, seq):
    w = SWA_WINDOW
    nblk = seq // w
    return pl.pallas_call(
        _swa_kernel,
        grid=(batch, nblk),
        in_specs=[
            pl.BlockSpec((w, WIDTH), lambda b, n: (b * nblk + n, QA)),
            pl.BlockSpec((seq, LANES), lambda b, n: (b, KA)),
            pl.BlockSpec((seq, LANES), lambda b, n: (b, VA)),
            pl.BlockSpec((HEADS, LANES), lambda b, n: (0, 0)),
        ],
        out_specs=pl.BlockSpec((w, WIDTH), lambda b, n: (b * nblk + n, 0)),
        out_shape=jax.ShapeDtypeStruct((batch * seq, WIDTH), BF16),
        compiler_params=_params("arbitrary", "arbitrary"),
    )(roped, roped, plain, sink_rows)


def _softplus(z):
    return jnp.maximum(z, 0.0) + jnp.log(1.0 + jnp.exp2(jnp.abs(z) * -LOG2E))


def _sb_kernel(q_ref, k_ref, v_ref, o_ref):
    i = pl.program_id(1)
    t = SB_BLOCK
    low = _lane() < HEAD_DIM
    mine = [low, jnp.logical_not(low)]
    row = lax.broadcasted_iota(jnp.int32, (t, t), 0)
    col = lax.broadcasted_iota(jnp.int32, (t, t), 1)
    suffix = jnp.where(row >= col, 1.0, 0.0).astype(BF16)
    past = col < row
    chains = [(p, side) for p in range(PAIRS) for side in range(2)]
    qs = [q_ref[:, p * LANES:(p + 1) * LANES] * SCALE for p in range(PAIRS)]
    zero = _bf16_zero()

    def block(j, carry, diagonal):
        accs, remains = carry
        off = pl.multiple_of(j * t, t)
        ks = [k_ref[pl.ds(off, t), p * LANES:(p + 1) * LANES] for p in range(PAIRS)]
        vs = [v_ref[pl.ds(off, t), p * LANES:(p + 1) * LANES] for p in range(PAIRS)]
        zs = [_dot_nt(qs[p], jnp.where(mine[side], ks[p], zero)) for p, side in chains]
        sps = []
        for z in zs:
            sp = _softplus(z)
            sps.append(jnp.where(past, sp, 0.0) if diagonal else sp)
        cs = [_dot(sp.astype(BF16), suffix) for sp in sps]
        wgts = []
        for n, (z, c) in enumerate(zip(zs, cs)):
            logw = z - c - remains[n]
            if diagonal:
                logw = jnp.where(past, logw, NEG)
            wgts.append(jnp.exp(logw).astype(BF16))
        pvs = [_dot(wgts[n], jnp.where(mine[side], vs[p], zero)) for n, (p, side) in enumerate(chains)]
        new_accs = tuple(accs[p] + (pvs[2 * p] + pvs[2 * p + 1]) for p in range(PAIRS))
        new_remains = tuple(remains[n] + cs[n][:, 0:1] for n in range(len(chains)))
        return new_accs, new_remains

    def smallest(remains):
        return jnp.min(functools.reduce(jnp.minimum, remains))

    init = (tuple(jnp.zeros((t, LANES), F32) for _ in range(PAIRS)),
            tuple(jnp.zeros((t, 1), F32) for _ in chains))
    carry = block(i, init, True)

    def cond(state):
        s, floor, _ = state
        return (s < i) & (floor < SB_DONE)

    def body(state):
        s, _, c = state
        c = block(i - 1 - s, c, False)
        return s + 1, smallest(c[1]), c

    carry = lax.while_loop(cond, body, (0, smallest(carry[1]), carry))[2]
    for p in range(PAIRS):
        o_ref[:, p * LANES:(p + 1) * LANES] = carry[0][p].astype(o_ref.dtype)


def _stick_breaking(plain, *, batch, seq):
    t = SB_BLOCK
    nq = seq // t
    return pl.pallas_call(
        _sb_kernel,
        grid=(batch, nq),
        in_specs=[
            pl.BlockSpec((t, WIDTH), lambda b, i: (b * nq + i, QB)),
            pl.BlockSpec((seq, WIDTH), lambda b, i: (b, KB)),
            pl.BlockSpec((seq, WIDTH), lambda b, i: (b, VB)),
        ],
        out_specs=pl.BlockSpec((t, WIDTH), lambda b, i: (b * nq + i, 0)),
        out_shape=jax.ShapeDtypeStruct((batch * seq, WIDTH), BF16),
        compiler_params=_params("arbitrary", "arbitrary"),
    )(plain, plain, plain)


def _moba_kernel(q_ref, k_ref, v_ref, o_ref, kmean_ref, *, nblocks):
    i = pl.program_id(1)
    t = MOBA_BLOCK
    nbp = MOBA_MAX_BLOCKS
    low = _lane() < HEAD_DIM
    mine = [low, jnp.logical_not(low)]
    chains = [(p, side) for p in range(PAIRS) for side in range(2)]

    @pl.when(i == 0)
    def _():
        for p in range(PAIRS):
            rows = [jnp.mean(k_ref[j * t:(j + 1) * t, p * LANES:(p + 1) * LANES].astype(F32), axis=0, keepdims=True)
                    for j in range(nblocks)]
            rows += [jnp.zeros((1, LANES), F32)] * (nbp - nblocks)
            km = jnp.concatenate(rows, axis=0)
            kmean_ref[2 * p * nbp:(2 * p + 1) * nbp, :] = jnp.where(low, km, 0.0)
            kmean_ref[(2 * p + 1) * nbp:(2 * p + 2) * nbp, :] = jnp.where(low, 0.0, km)

    gates = []
    for p in range(PAIRS):
        q = q_ref[:, p * LANES:(p + 1) * LANES]
        km = kmean_ref[2 * p * nbp:(2 * p + 2) * nbp, :]
        km_hi = km.astype(BF16)
        km_lo = (km - km_hi.astype(F32)).astype(BF16)
        gates.append((_dot_nt(km_hi, q) + _dot_nt(km_lo, q)).reshape(2, nbp, t))
    blk = lax.broadcasted_iota(jnp.int32, (HEADS, nbp, t), 1)
    gate = jnp.where(blk < i, jnp.concatenate(gates, axis=0), NEG)
    count = jnp.zeros((HEADS, nbp, t), F32)
    for other in range(nblocks):
        g_other = gate[:, other:other + 1, :]
        beats = (g_other > gate) | ((g_other == gate) & (other < blk))
        count = count + jnp.where(beats, 1.0, 0.0)
    bias = jnp.where((count < min(MOBA_TOPK, nblocks)) & (blk < i), 0.0, NEG)

    eye = jnp.where(lax.broadcasted_iota(jnp.int32, (t, t), 0) == lax.broadcasted_iota(jnp.int32, (t, t), 1),
                    1.0, 0.0).astype(BF16)
    filler = jnp.zeros((HEAD_DIM - nbp, t), F32)
    q_aug = []
    for p in range(PAIRS):
        bias_t = jnp.concatenate([bias[2 * p + 1], filler, bias[2 * p], filler], axis=0).astype(BF16)
        bias_q = _dot_nt(eye, bias_t).astype(BF16)
        qs = q_ref[:, p * LANES:(p + 1) * LANES] * SCALE
        q_aug.append(jnp.where(low, qs, bias_q))
        q_aug.append(jnp.where(low, bias_q, qs))

    row = lax.broadcasted_iota(jnp.int32, (t, t), 0)
    col = lax.broadcasted_iota(jnp.int32, (t, t), 1)
    causal = col <= row
    ones = jnp.ones((1, LANES), BF16)
    zero = _bf16_zero()

    def block(j, carry, diagonal):
        accs, maxes = carry
        off = pl.multiple_of(j * t, t)
        ks = [k_ref[pl.ds(off, t), p * LANES:(p + 1) * LANES] for p in range(PAIRS)]
        vs = [v_ref[pl.ds(off, t), p * LANES:(p + 1) * LANES] for p in range(PAIRS)]
        if diagonal:
            fill = [zero, zero]
        else:
            fill = [_one_hot_lane(HEAD_DIM + j), _one_hot_lane(j)]
        scores = [_dot_nt(q_aug[n], jnp.where(mine[side], ks[p], fill[side])) for n, (p, side) in enumerate(chains)]
        if diagonal:
            scores = [jnp.where(causal, s, NEG) for s in scores]
        new_max = [jnp.maximum(m, jnp.max(s, axis=-1, keepdims=True)) for m, s in zip(maxes, scores)]
        probs = [jnp.exp(s - m).astype(BF16) for s, m in zip(scores, new_max)]
        pvs = [_dot(probs[n], jnp.where(mine[side], vs[p], ones)) for n, (p, side) in enumerate(chains)]
        new_accs = tuple(a * jnp.exp(mo - mn) + pv for a, mo, mn, pv in zip(accs, maxes, new_max, pvs))
        return new_accs, tuple(new_max)

    init = (tuple(jnp.zeros((t, LANES), F32) for _ in chains), tuple(jnp.full((t, 1), NEG, F32) for _ in chains))
    carry = block(i, init, True)
    carry = lax.fori_loop(0, i, lambda j, c: block(j, c, False), carry)
    accs = carry[0]
    for p in range(PAIRS):
        a_lo, a_hi = accs[2 * p], accs[2 * p + 1]
        denom = pltpu.roll(jnp.where(low, a_hi, a_lo), HEAD_DIM, axis=1)
        o = jnp.where(low, a_lo, a_hi) / denom
        o_ref[:, p * LANES:(p + 1) * LANES] = o.astype(o_ref.dtype)


def _moba(roped, plain, *, batch, seq):
    t = MOBA_BLOCK
    nq = seq // t
    assert seq % t == 0 and nq <= MOBA_MAX_BLOCKS
    return pl.pallas_call(
        functools.partial(_moba_kernel, nblocks=nq),
        grid=(batch, nq),
        in_specs=[
            pl.BlockSpec((t, WIDTH), lambda b, i: (b * nq + i, QC)),
            pl.BlockSpec((seq, WIDTH), lambda b, i: (b, KC)),
            pl.BlockSpec((seq, WIDTH), lambda b, i: (b, VC)),
        ],
        out_specs=pl.BlockSpec((t, WIDTH), lambda b, i: (b * nq + i, 0)),
        out_shape=jax.ShapeDtypeStruct((batch * seq, WIDTH), BF16),
        scratch_shapes=[pltpu.VMEM((HEADS * MOBA_MAX_BLOCKS, LANES), F32)],
        compiler_params=_params("arbitrary", "arbitrary"),
    )(roped, roped, plain)


def _merge_kernel(oa_ref, ob_ref, oc_ref, gate_ref, bias_ref, wb_ref, wo_ref, x_ref, o_ref):
    d = x_ref.shape[1]
    ys = [_dot(branch[...], wb_ref[n]) for n, branch in enumerate((oa_ref, ob_ref, oc_ref))]
    merged = None
    for n, y in enumerate(ys):
        pre = gate_ref[:, n * d:(n + 1) * d] + bias_ref[:, n * d:(n + 1) * d]
        term = y / (1.0 + jnp.exp(-pre))
        merged = term if merged is None else merged + term
    o_ref[...] = x_ref[...] + _dot(merged.astype(BF16), wo_ref[...])


def _merge(oa, ob, oc, gates, bias, wb, wo, x, *, tm):
    t, d = x.shape
    bw = oa.shape[1]
    row = lambda i: (i, 0)
    fixed2 = lambda i: (0, 0)
    return pl.pallas_call(
        _merge_kernel,
        grid=(t // tm,),
        in_specs=[
            pl.BlockSpec((tm, bw), row),
            pl.BlockSpec((tm, bw), row),
            pl.BlockSpec((tm, bw), row),
            pl.BlockSpec((tm, N_BRANCHES * d), row),
            pl.BlockSpec((1, N_BRANCHES * d), fixed2),
            pl.BlockSpec((N_BRANCHES, bw, d), lambda i: (0, 0, 0)),
            pl.BlockSpec((d, d), fixed2),
            pl.BlockSpec((tm, d), row),
        ],
        out_specs=pl.BlockSpec((tm, d), row),
        out_shape=jax.ShapeDtypeStruct((t, d), F32),
        compiler_params=_params("arbitrary"),
    )(oa, ob, oc, gates, bias, wb, wo, x)


def _shift_rows(u, tail, shift):
    rolled = pltpu.roll(u, shift, axis=0)
    head = jnp.where(lax.broadcasted_iota(jnp.int32, tail.shape, 0) < shift,
                     pltpu.roll(tail, shift, axis=0), rolled[:8])
    return jnp.concatenate([head, rolled[8:]], axis=0)


def _ffn_kernel(x_ref, g_ref, wup_ref, cw_ref, cb_ref, wdn_ref, gf_ref, o_ref, tail_ref, acc_ref,
                *, tiles_per_seq, chunk, final_norm):
    i = pl.program_id(0)
    tm = x_ref.shape[0]
    dff = wdn_ref.shape[0]

    @pl.when(i % tiles_per_seq == 0)
    def _():
        tail_ref[...] = jnp.zeros(tail_ref.shape, F32)

    x = x_ref[...]
    ms = jnp.mean(x * x, axis=-1, keepdims=True)
    h = (x * lax.rsqrt(ms + RMS_EPS) * g_ref[...]).astype(BF16)

    def up(c0, c1):
        return _dot(h, wup_ref[:, c0:c1]), _dot(h, wup_ref[:, dff + c0:dff + c1])

    def conv(u, c0, c1):
        tail = tail_ref[:, c0:c1]
        tail_ref[:, c0:c1] = u[tm - 8:, :]
        out = cb_ref[:, c0:c1] + cw_ref[0:1, c0:c1] * _shift_rows(u, tail, 2)
        out = out + cw_ref[1:2, c0:c1] * _shift_rows(u, tail, 1)
        return out + cw_ref[2:3, c0:c1] * u

    bounds = [(c0, min(c0 + chunk, dff)) for c0 in range(0, dff, chunk)]
    pre = up(*bounds[0])
    for n, (c0, c1) in enumerate(bounds):
        cur = pre
        if n + 1 < len(bounds):
            pre = up(*bounds[n + 1])
        ua = conv(cur[0], c0, c1)
        uv = conv(cur[1], dff + c0, dff + c1)
        act = (ua / (1.0 + jnp.exp(-ua)) * uv).astype(BF16)
        part = _dot(act, wdn_ref[c0:c1, :])
        if n == 0:
            acc_ref[...] = x + part
        else:
            acc_ref[...] += part

    y = acc_ref[...]
    if final_norm:
        ms = jnp.mean(y * y, axis=-1, keepdims=True)
        y = y * lax.rsqrt(ms + RMS_EPS) * gf_ref[...]
    o_ref[...] = y


def _ffn(x, gain, wup, cw, cb, wdn, gain_final, *, seq, tm, chunk, final_norm):
    t, d = x.shape
    dff = wdn.shape[0]
    assert seq % tm == 0
    fixed = lambda i: (0, 0)
    once = pl.Buffered(1)
    return pl.pallas_call(
        functools.partial(_ffn_kernel, tiles_per_seq=seq // tm, chunk=chunk, final_norm=final_norm),
        grid=(t // tm,),
        in_specs=[
            pl.BlockSpec((tm, d), lambda i: (i, 0)),
            pl.BlockSpec((1, d), fixed),
            pl.BlockSpec((d, 2 * dff), fixed, pipeline_mode=once),
            pl.BlockSpec((3, 2 * dff), fixed),
            pl.BlockSpec((1, 2 * dff), fixed),
            pl.BlockSpec((dff, d), fixed, pipeline_mode=once),
            pl.BlockSpec((1, d), fixed),
        ],
        out_specs=pl.BlockSpec((tm, d), lambda i: (i, 0)),
        out_shape=jax.ShapeDtypeStruct((t, d), F32),
        scratch_shapes=[pltpu.VMEM((8, 2 * dff), F32), pltpu.VMEM((tm, d), F32)],
        compiler_params=_params("arbitrary"),
    )(x, gain, wup, cw, cb, wdn, gain_final)


def _rope_tables(seq):
    inv = ROPE_THETA ** (-jnp.arange(0, HEAD_DIM, 2, dtype=F32) / HEAD_DIM)
    ang = jnp.arange(seq, dtype=F32)[:, None] * inv[None, :]
    cos, sin = jnp.cos(ang), jnp.sin(ang)
    reps = LANES // HEAD_DIM
    return jnp.tile(cos, (1, 2 * reps)), jnp.tile(jnp.concatenate([-sin, sin], axis=1), (1, reps))


def _in_weight(w):
    kv = SWA_KV_HEADS * HEAD_DIM
    a, b = WIDTH + 2 * kv, WIDTH + 2 * kv + 3 * WIDTH
    qa, ka, va = w[:, :WIDTH], w[:, WIDTH:WIDTH + kv], w[:, WIDTH + kv:a]
    qc, kc, vc = w[:, b:b + WIDTH], w[:, b + WIDTH:b + 2 * WIDTH], w[:, b + 2 * WIDTH:b + 3 * WIDTH]
    return jnp.concatenate([qa, qc, kc, ka, w[:, a:b], vc, va, w[:, b + 3 * WIDTH:]], axis=1)


def kernel(x, norm_mix, w_in, b_gate, sinks, w_branch, w_out, norm_ffn, w_up, conv_w, conv_b, w_down, norm_final):
    batch, seq, d = x.shape
    depth = w_in.shape[0]
    tokens = batch * seq
    cos, sin_signed = _rope_tables(seq)
    xt = x.reshape(tokens, d)
    tm_tok = min(512, seq)
    for layer in range(depth):
        roped, plain, gates = _in_project(xt, norm_mix[layer].reshape(1, d), _in_weight(w_in[layer]).astype(BF16),
                                          cos, sin_signed, seq=seq, tm=tm_tok, chunk=512)
        sink_rows = jnp.broadcast_to(sinks[layer].astype(F32)[:, None], (HEADS, LANES))
        o_a = _swa(roped, plain, sink_rows, batch=batch, seq=seq)
        o_b = _stick_breaking(plain, batch=batch, seq=seq)
        o_c = _moba(roped, plain, batch=batch, seq=seq)
        xt = _merge(o_a, o_b, o_c, gates, b_gate[layer].reshape(1, N_BRANCHES * d),
                    w_branch[layer].astype(BF16), w_out[layer].astype(BF16), xt, tm=tm_tok)
        xt = _ffn(xt, norm_ffn[layer].reshape(1, d), w_up[layer].astype(BF16), conv_w[layer],
                  conv_b[layer].reshape(1, -1), w_down[layer].astype(BF16), norm_final.reshape(1, d),
                  seq=seq, tm=tm_tok, chunk=768, final_norm=(layer == depth - 1))
    return xt.reshape(batch, seq, d)
```

```python
import functools

import jax
import jax.numpy as jnp
from jax import lax
from jax.experimental import pallas as pl
from jax.experimental.pallas import tpu as pltpu

F32 = jnp.float32
BF16 = jnp.bfloat16

HEAD_DIM = 64
LANES = 128
HEADS = 8
PAIRS = HEADS // 2
WIDTH = HEADS * HEAD_DIM
SWA_KV_HEADS = 2
SWA_WINDOW = 128
SWA_WINDOWS_PER_STEP = 4
SB_BLOCK = 256
SB_DONE = 104.0
MOBA_BLOCK = 256
MOBA_TOPK = 3
MOBA_MAX_BLOCKS = 8
N_BRANCHES = 3
ROPE_THETA = 10000.0
RMS_EPS = 1e-6
SCALE = HEAD_DIM ** -0.5
LOG2E = 1.4426950408889634
NEG = -1e30

VMEM_LIMIT = 56 * 1024 * 1024

ROPE_COLS = 3 * WIDTH + LANES
PLAIN_COLS = 4 * WIDTH + LANES
QA, QC, KC = range(3)
KA = 3 * PAIRS
QB, KB, VB, VC = range(4)
VA = 4 * PAIRS


def _params(*sem):
    return pltpu.CompilerParams(dimension_semantics=sem, vmem_limit_bytes=VMEM_LIMIT)


def _dot(a, b):
    return jnp.dot(a, b, preferred_element_type=F32)


def _dot_nt(a, b):
    return lax.dot_general(a, b, (((1,), (1,)), ((), ())), preferred_element_type=F32)


def _lane():
    return lax.broadcasted_iota(jnp.int32, (1, LANES), 1)


def _one_hot_lane(index):
    return jnp.where(_lane() == index, 1.0, 0.0).astype(BF16)


def _rope(piece, cos, sin_signed):
    first_half = (_lane() & 32) == 0
    nxt = pltpu.roll(piece, LANES - 32, axis=1)
    prv = pltpu.roll(piece, 32, axis=1)
    return piece * cos + jnp.where(first_half, nxt, prv) * sin_signed


def _in_proj_kernel(x_ref, g_ref, w_ref, cos_ref, sin_ref, bias_ref, rope_ref, plain_ref, gate_ref, *, chunk):
    x = x_ref[...]
    ms = jnp.mean(x * x, axis=-1, keepdims=True)
    h = (x * lax.rsqrt(ms + RMS_EPS) * g_ref[...]).astype(BF16)
    cos, sin_signed = cos_ref[...], sin_ref[...]
    base = 0
    for out_ref in (rope_ref, plain_ref, gate_ref):
        n = out_ref.shape[1]
        for c0 in range(0, n, chunk):
            c1 = min(c0 + chunk, n)
            acc = _dot(h, w_ref[:, base + c0:base + c1])
            if out_ref is rope_ref:
                for c in range(c0, c1, LANES):
                    out_ref[:, c:c + LANES] = _rope(acc[:, c - c0:c - c0 + LANES], cos, sin_signed).astype(out_ref.dtype)
            elif out_ref is gate_ref:
                out_ref[:, c0:c1] = (1.0 / (1.0 + jnp.exp(-(acc + bias_ref[:, c0:c1])))).astype(out_ref.dtype)
            else:
                out_ref[:, c0:c1] = acc.astype(out_ref.dtype)
        base += n


def _in_project(x, gain, w, cos, sin_signed, gate_bias, *, seq, tm, chunk):
    t, d = x.shape
    n_gate = w.shape[1] - ROPE_COLS - PLAIN_COLS
    assert t % tm == 0 and seq % tm == 0
    pos_tiles = seq // tm
    row = lambda i: (i, 0)
    return pl.pallas_call(
        functools.partial(_in_proj_kernel, chunk=chunk),
        grid=(t // tm,),
        in_specs=[
            pl.BlockSpec((tm, d), row),
            pl.BlockSpec((1, d), lambda i: (0, 0)),
            pl.BlockSpec(w.shape, lambda i: (0, 0), pipeline_mode=pl.Buffered(1)),
            pl.BlockSpec((tm, LANES), lambda i: (i % pos_tiles, 0)),
            pl.BlockSpec((tm, LANES), lambda i: (i % pos_tiles, 0)),
            pl.BlockSpec((1, n_gate), lambda i: (0, 0)),
        ],
        out_specs=[pl.BlockSpec((tm, ROPE_COLS), row), pl.BlockSpec((tm, PLAIN_COLS), row),
                   pl.BlockSpec((tm, n_gate), row)],
        out_shape=[jax.ShapeDtypeStruct((t, ROPE_COLS), BF16), jax.ShapeDtypeStruct((t, PLAIN_COLS), BF16),
                   jax.ShapeDtypeStruct((t, n_gate), BF16)],
        compiler_params=_params("arbitrary"),
    )(x, gain, w, cos, sin_signed, gate_bias)


def _swa_kernel(q_ref, k_ref, v_ref, sink_ref, o_ref, kat_ref, vt_ref):
    n = pl.program_id(1)
    w = SWA_WINDOW
    low = _lane() < HEAD_DIM
    chains = [(g, side) for g in range(SWA_KV_HEADS) for side in range(2)]

    @pl.when(n == 0)
    def _():
        k = k_ref[...].astype(F32)
        v = v_ref[...].astype(F32)
        k_sw = pltpu.roll(k, HEAD_DIM, axis=1)
        v_sw = pltpu.roll(v, HEAD_DIM, axis=1)
        k_at = [jnp.where(low, k, 0.0), jnp.where(low, 0.0, k_sw), jnp.where(low, k_sw, 0.0), jnp.where(low, 0.0, k)]
        v_at = [jnp.where(low, v, 1.0), jnp.where(low, 1.0, v_sw), jnp.where(low, v_sw, 1.0), jnp.where(low, 1.0, v)]
        for c in range(len(chains)):
            kat_ref[c] = k_at[c].astype(BF16)
            vt_ref[c] = v_at[c].T.astype(BF16)

    key_i = lax.broadcasted_iota(jnp.int32, (2 * w, 2 * w), 0)
    query_i = lax.broadcasted_iota(jnp.int32, (2 * w, 2 * w), 1) & (w - 1)
    sinks = []
    for g, side in chains:
        h0 = 4 * g + side
        sinks.append(jnp.concatenate([sink_ref[h0:h0 + 1, :], sink_ref[h0 + 2:h0 + 3, :]], axis=1))
    starts, scores = [], []
    for win in range(SWA_WINDOWS_PER_STEP):
        blk = n * SWA_WINDOWS_PER_STEP + win
        start = pl.multiple_of(jnp.maximum(blk - 1, 0) * w, w)
        diff = (blk * w + query_i) - (start + key_i)
        valid = (diff >= 0) & (diff < w)
        rows = slice(win * w, (win + 1) * w)
        qs = [jnp.concatenate([q_ref[rows, (2 * g) * LANES:(2 * g + 1) * LANES],
                               q_ref[rows, (2 * g + 1) * LANES:(2 * g + 2) * LANES]], axis=0) * SCALE
              for g in range(SWA_KV_HEADS)]
        starts.append(start)
        scores.append([jnp.where(valid, _dot_nt(kat_ref[c, pl.ds(start, 2 * w), :], qs[g]), NEG)
                       for c, (g, side) in enumerate(chains)])
    maxes = [[jnp.maximum(jnp.max(s, axis=0, keepdims=True), sink) for s, sink in zip(win_scores, sinks)]
             for win_scores in scores]
    probs = [[jnp.exp(s - m).astype(BF16) for s, m in zip(win_scores, win_maxes)]
             for win_scores, win_maxes in zip(scores, maxes)]
    outs = [[_dot(vt_ref[c, :, pl.ds(start, 2 * w)], p) for c, p in enumerate(win_probs)]
            for start, win_probs in zip(starts, probs)]
    for win in range(SWA_WINDOWS_PER_STEP):
        for g in range(SWA_KV_HEADS):
            o_lo, o_hi = outs[win][2 * g], outs[win][2 * g + 1]
            d_lo = o_lo[HEAD_DIM:] + jnp.exp(sinks[2 * g] - maxes[win][2 * g])
            d_hi = o_hi[:HEAD_DIM] + jnp.exp(sinks[2 * g + 1] - maxes[win][2 * g + 1])
            o = jnp.concatenate([o_lo[:HEAD_DIM] / d_lo, o_hi[HEAD_DIM:] / d_hi], axis=0).T.astype(o_ref.dtype)
            o_ref[win * w:(win + 1) * w, (2 * g) * LANES:(2 * g + 1) * LANES] = o[:w]
            o_ref[win * w:(win + 1) * w, (2 * g + 1) * LANES:(2 * g + 2) * LANES] = o[w:]


def _swa(roped, plain, sink_rows, *, batch, seq):
    tq = SWA_WINDOW * SWA_WINDOWS_PER_STEP
    steps = seq // tq
    return pl.pallas_call(
        _swa_kernel,
        grid=(batch, steps),
        in_specs=[
            pl.BlockSpec((tq, WIDTH), lambda b, n: (b * steps + n, QA)),
            pl.BlockSpec((seq, LANES), lambda b, n: (b, KA)),
            pl.BlockSpec((seq, LANES), lambda b, n: (b, VA)),
            pl.BlockSpec((HEADS, LANES), lambda b, n: (0, 0)),
        ],
        out_specs=pl.BlockSpec((tq, WIDTH), lambda b, n: (b * steps + n, 0)),
        out_shape=jax.ShapeDtypeStruct((batch * seq, WIDTH), BF16),
        scratch_shapes=[pltpu.VMEM((2 * SWA_KV_HEADS, seq, LANES), BF16), pltpu.VMEM((2 * SWA_KV_HEADS, LANES, seq), BF16)],
        compiler_params=_params("arbitrary", "arbitrary"),
    )(roped, roped, plain, sink_rows)


def _softplus(z):
    return jnp.maximum(z, 0.0) + jnp.log(1.0 + jnp.exp2(jnp.abs(z) * -LOG2E))


def _sb_kernel(q_ref, k_ref, v_ref, o_ref, kmask_ref, vt_ref):
    i = pl.program_id(1)
    t = SB_BLOCK
    nblocks = k_ref.shape[0] // t
    low = _lane() < HEAD_DIM
    mine = [low, jnp.logical_not(low)]
    chains = [(p, side) for p in range(PAIRS) for side in range(2)]

    @pl.when(i == 0)
    def _():
        for j in range(nblocks):
            rows = slice(j * t, (j + 1) * t)
            for side in range(2):
                kmask_ref[side, rows, :] = jnp.where(jnp.tile(mine[side], (1, PAIRS)), k_ref[rows, :], jnp.zeros((1, WIDTH), BF16))
                for p in range(PAIRS):
                    v = jnp.where(mine[side], v_ref[rows, p * LANES:(p + 1) * LANES].astype(F32), 0.0)
                    vt_ref[side, p * LANES:(p + 1) * LANES, rows] = v.T.astype(BF16)

    key = lax.broadcasted_iota(jnp.int32, (t, t), 0)
    query = lax.broadcasted_iota(jnp.int32, (t, t), 1)
    suffix = jnp.where(query >= key, 1.0, 0.0).astype(BF16)
    past = key < query
    qs = [q_ref[:, p * LANES:(p + 1) * LANES] * SCALE for p in range(PAIRS)]

    def block(j, carry, diagonal):
        accs, remains = carry
        off = pl.multiple_of(j * t, t)
        zs = [_dot_nt(kmask_ref[side, pl.ds(off, t), p * LANES:(p + 1) * LANES], qs[p]) for p, side in chains]
        sps = []
        for z in zs:
            sp = _softplus(z)
            sps.append(jnp.where(past, sp, 0.0) if diagonal else sp)
        cs = [_dot(suffix, sp.astype(BF16)) for sp in sps]
        wgts = []
        for n, (z, c) in enumerate(zip(zs, cs)):
            logw = z - c - remains[n]
            if diagonal:
                logw = jnp.where(past, logw, NEG)
            wgts.append(jnp.exp(logw).astype(BF16))
        pvs = [_dot(vt_ref[side, p * LANES:(p + 1) * LANES, pl.ds(off, t)], wgts[n]) for n, (p, side) in enumerate(chains)]
        new_accs = tuple(accs[p] + (pvs[2 * p] + pvs[2 * p + 1]) for p in range(PAIRS))
        new_remains = tuple(remains[n] + cs[n][0:1, :] for n in range(len(chains)))
        return new_accs, new_remains

    def smallest(remains):
        return jnp.min(functools.reduce(jnp.minimum, remains))

    init = (tuple(jnp.zeros((LANES, t), F32) for _ in range(PAIRS)),
            tuple(jnp.zeros((1, t), F32) for _ in chains))
    carry = block(i, init, True)

    def cond(state):
        s, floor, _ = state
        return (s < i) & (floor < SB_DONE)

    def body(state):
        s, _, c = state
        c = block(i - 1 - s, c, False)
        return s + 1, smallest(c[1]), c

    carry = lax.while_loop(cond, body, (0, smallest(carry[1]), carry))[2]
    for p in range(PAIRS):
        o_ref[:, p * LANES:(p + 1) * LANES] = carry[0][p].T.astype(o_ref.dtype)


def _stick_breaking(plain, *, batch, seq):
    t = SB_BLOCK
    nq = seq // t
    return pl.pallas_call(
        _sb_kernel,
        grid=(batch, nq),
        in_specs=[
            pl.BlockSpec((t, WIDTH), lambda b, i: (b * nq + i, QB)),
            pl.BlockSpec((seq, WIDTH), lambda b, i: (b, KB)),
            pl.BlockSpec((seq, WIDTH), lambda b, i: (b, VB)),
        ],
        out_specs=pl.BlockSpec((t, WIDTH), lambda b, i: (b * nq + i, 0)),
        out_shape=jax.ShapeDtypeStruct((batch * seq, WIDTH), BF16),
        scratch_shapes=[pltpu.VMEM((2, seq, WIDTH), BF16), pltpu.VMEM((2, WIDTH, seq), BF16)],
        compiler_params=_params("arbitrary", "arbitrary"),
    )(plain, plain, plain)


def _moba_kernel(q_ref, k_ref, v_ref, o_ref, kmean_ref, kaug_ref, vaug_ref, *, nblocks):
    i = pl.program_id(1)
    t = MOBA_BLOCK
    nbp = MOBA_MAX_BLOCKS
    low = _lane() < HEAD_DIM
    mine = [low, jnp.logical_not(low)]
    chains = [(p, side) for p in range(PAIRS) for side in range(2)]

    @pl.when(i == 0)
    def _():
        for p in range(PAIRS):
            rows = [jnp.mean(k_ref[j * t:(j + 1) * t, p * LANES:(p + 1) * LANES].astype(F32), axis=0, keepdims=True)
                    for j in range(nblocks)]
            rows += [jnp.zeros((1, LANES), F32)] * (nbp - nblocks)
            km = jnp.concatenate(rows, axis=0)
            kmean_ref[2 * p * nbp:(2 * p + 1) * nbp, :] = jnp.where(low, km, 0.0)
            kmean_ref[(2 * p + 1) * nbp:(2 * p + 2) * nbp, :] = jnp.where(low, 0.0, km)
        for j in range(nblocks):
            rows = slice(j * t, (j + 1) * t)
            fill = [_one_hot_lane(HEAD_DIM + j), _one_hot_lane(j)]
            for side in range(2):
                kaug_ref[side, rows, :] = jnp.where(jnp.tile(mine[side], (1, PAIRS)), k_ref[rows, :], jnp.tile(fill[side], (1, PAIRS)))
                for p in range(PAIRS):
                    v = jnp.where(mine[side], v_ref[rows, p * LANES:(p + 1) * LANES].astype(F32), 1.0)
                    vaug_ref[side, p * LANES:(p + 1) * LANES, rows] = v.T.astype(BF16)

    gates = []
    for p in range(PAIRS):
        q = q_ref[:, p * LANES:(p + 1) * LANES]
        km = kmean_ref[2 * p * nbp:(2 * p + 2) * nbp, :]
        km_hi = km.astype(BF16)
        km_lo = (km - km_hi.astype(F32)).astype(BF16)
        gates.append((_dot_nt(km_hi, q) + _dot_nt(km_lo, q)).reshape(2, nbp, t))
    blk = lax.broadcasted_iota(jnp.int32, (HEADS, nbp, t), 1)
    gate = jnp.where(blk < i, jnp.concatenate(gates, axis=0), NEG)
    count = jnp.zeros((HEADS, nbp, t), F32)
    for other in range(nblocks):
        g_other = gate[:, other:other + 1, :]
        beats = (g_other > gate) | ((g_other == gate) & (other < blk))
        count = count + jnp.where(beats, 1.0, 0.0)
    bias = jnp.where(((count < min(MOBA_TOPK, nblocks)) & (blk < i)) | (blk == i), 0.0, NEG)

    eye = jnp.where(lax.broadcasted_iota(jnp.int32, (t, t), 0) == lax.broadcasted_iota(jnp.int32, (t, t), 1),
                    1.0, 0.0).astype(BF16)
    filler = jnp.zeros((HEAD_DIM - nbp, t), F32)
    q_aug = []
    for p in range(PAIRS):
        bias_t = jnp.concatenate([bias[2 * p + 1], filler, bias[2 * p], filler], axis=0).astype(BF16)
        bias_q = _dot_nt(eye, bias_t).astype(BF16)
        qs = q_ref[:, p * LANES:(p + 1) * LANES] * SCALE
        q_aug.append(jnp.where(low, qs, bias_q))
        q_aug.append(jnp.where(low, bias_q, qs))

    key = lax.broadcasted_iota(jnp.int32, (t, t), 0)
    query = lax.broadcasted_iota(jnp.int32, (t, t), 1)
    causal = key <= query

    def block(j, carry, diagonal):
        accs, maxes = carry
        off = pl.multiple_of(j * t, t)
        scores = [_dot_nt(kaug_ref[side, pl.ds(off, t), p * LANES:(p + 1) * LANES], q_aug[n])
                  for n, (p, side) in enumerate(chains)]
        if diagonal:
            scores = [jnp.where(causal, s, NEG) for s in scores]
        new_max = [jnp.maximum(m, jnp.max(s, axis=0, keepdims=True)) for m, s in zip(maxes, scores)]
        probs = [jnp.exp(s - m).astype(BF16) for s, m in zip(scores, new_max)]
        pvs = [_dot(vaug_ref[side, p * LANES:(p + 1) * LANES, pl.ds(off, t)], probs[n]) for n, (p, side) in enumerate(chains)]
        new_accs = tuple(a * jnp.exp(mo - mn) + pv for a, mo, mn, pv in zip(accs, maxes, new_max, pvs))
        return new_accs, tuple(new_max)

    init = (tuple(jnp.zeros((LANES, t), F32) for _ in chains), tuple(jnp.full((1, t), NEG, F32) for _ in chains))
    carry = block(i, init, True)
    carry = lax.fori_loop(0, i, lambda j, c: block(j, c, False), carry)
    accs = carry[0]
    for p in range(PAIRS):
        a_lo, a_hi = accs[2 * p], accs[2 * p + 1]
        o_t = jnp.concatenate([a_lo[:HEAD_DIM] / a_lo[HEAD_DIM:], a_hi[HEAD_DIM:] / a_hi[:HEAD_DIM]], axis=0)
        o_ref[:, p * LANES:(p + 1) * LANES] = o_t.T.astype(o_ref.dtype)


def _moba(roped, plain, *, batch, seq):
    t = MOBA_BLOCK
    nq = seq // t
    assert seq % t == 0 and nq <= MOBA_MAX_BLOCKS
    return pl.pallas_call(
        functools.partial(_moba_kernel, nblocks=nq),
        grid=(batch, nq),
        in_specs=[
            pl.BlockSpec((t, WIDTH), lambda b, i: (b * nq + i, QC)),
            pl.BlockSpec((seq, WIDTH), lambda b, i: (b, KC)),
            pl.BlockSpec((seq, WIDTH), lambda b, i: (b, VC)),
        ],
        out_specs=pl.BlockSpec((t, WIDTH), lambda b, i: (b * nq + i, 0)),
        out_shape=jax.ShapeDtypeStruct((batch * seq, WIDTH), BF16),
        scratch_shapes=[pltpu.VMEM((HEADS * MOBA_MAX_BLOCKS, LANES), F32), pltpu.VMEM((2, seq, WIDTH), BF16),
                        pltpu.VMEM((2, WIDTH, seq), BF16)],
        compiler_params=_params("arbitrary", "arbitrary"),
    )(roped, roped, plain)


def _merge_kernel(oa_ref, ob_ref, oc_ref, gate_ref, wb_ref, wo_ref, x_ref, o_ref):
    d = x_ref.shape[1]
    ys = [_dot(branch[...], wb_ref[n]) for n, branch in enumerate((oa_ref, ob_ref, oc_ref))]
    merged = None
    for n, y in enumerate(ys):
        term = y * gate_ref[:, n * d:(n + 1) * d].astype(F32)
        merged = term if merged is None else merged + term
    o_ref[...] = x_ref[...] + _dot(merged.astype(BF16), wo_ref[...])


def _merge(oa, ob, oc, gates, wb, wo, x, *, tm):
    t, d = x.shape
    bw = oa.shape[1]
    row = lambda i: (i, 0)
    fixed2 = lambda i: (0, 0)
    return pl.pallas_call(
        _merge_kernel,
        grid=(t // tm,),
        in_specs=[
            pl.BlockSpec((tm, bw), row),
            pl.BlockSpec((tm, bw), row),
            pl.BlockSpec((tm, bw), row),
            pl.BlockSpec((tm, N_BRANCHES * d), row),
            pl.BlockSpec((N_BRANCHES, bw, d), lambda i: (0, 0, 0)),
            pl.BlockSpec((d, d), fixed2),
            pl.BlockSpec((tm, d), row),
        ],
        out_specs=pl.BlockSpec((tm, d), row),
        out_shape=jax.ShapeDtypeStruct((t, d), F32),
        compiler_params=_params("arbitrary"),
    )(oa, ob, oc, gates, wb, wo, x)


def _shift_rows(u, tail, shift):
    rolled = pltpu.roll(u, shift, axis=0)
    head = jnp.where(lax.broadcasted_iota(jnp.int32, tail.shape, 0) < shift,
                     pltpu.roll(tail, shift, axis=0), rolled[:8])
    return jnp.concatenate([head, rolled[8:]], axis=0)


def _ffn_kernel(x_ref, g_ref, wup_ref, cw_ref, cb_ref, wdn_ref, gf_ref, o_ref, tail_ref, acc_ref,
                *, tiles_per_seq, chunk, final_norm):
    i = pl.program_id(0)
    tm = x_ref.shape[0]
    dff = wdn_ref.shape[0]

    @pl.when(i % tiles_per_seq == 0)
    def _():
        tail_ref[...] = jnp.zeros(tail_ref.shape, F32)

    x = x_ref[...]
    ms = jnp.mean(x * x, axis=-1, keepdims=True)
    h = (x * lax.rsqrt(ms + RMS_EPS) * g_ref[...]).astype(BF16)

    def up(c0, c1):
        return _dot(h, wup_ref[:, c0:c1]), _dot(h, wup_ref[:, dff + c0:dff + c1])

    def conv(u, c0, c1):
        tail = tail_ref[:, c0:c1]
        tail_ref[:, c0:c1] = u[tm - 8:, :]
        out = cb_ref[:, c0:c1] + cw_ref[0:1, c0:c1] * _shift_rows(u, tail, 2)
        out = out + cw_ref[1:2, c0:c1] * _shift_rows(u, tail, 1)
        return out + cw_ref[2:3, c0:c1] * u

    bounds = [(c0, min(c0 + chunk, dff)) for c0 in range(0, dff, chunk)]
    pre = up(*bounds[0])
    for n, (c0, c1) in enumerate(bounds):
        cur = pre
        if n + 1 < len(bounds):
            pre = up(*bounds[n + 1])
        ua = conv(cur[0], c0, c1)
        uv = conv(cur[1], dff + c0, dff + c1)
        act = (ua / (1.0 + jnp.exp(-ua)) * uv).astype(BF16)
        part = _dot(act, wdn_ref[c0:c1, :])
        if n == 0:
            acc_ref[...] = x + part
        else:
            acc_ref[...] += part

    y = acc_ref[...]
    if final_norm:
        ms = jnp.mean(y * y, axis=-1, keepdims=True)
        y = y * lax.rsqrt(ms + RMS_EPS) * gf_ref[...]
    o_ref[...] = y


def _ffn(x, gain, wup, cw, cb, wdn, gain_final, *, seq, tm, chunk, final_norm):
    t, d = x.shape
    dff = wdn.shape[0]
    assert seq % tm == 0
    fixed = lambda i: (0, 0)
    once = pl.Buffered(1)
    return pl.pallas_call(
        functools.partial(_ffn_kernel, tiles_per_seq=seq // tm, chunk=chunk, final_norm=final_norm),
        grid=(t // tm,),
        in_specs=[
            pl.BlockSpec((tm, d), lambda i: (i, 0)),
            pl.BlockSpec((1, d), fixed),
            pl.BlockSpec((d, 2 * dff), fixed, pipeline_mode=once),
            pl.BlockSpec((3, 2 * dff), fixed),
            pl.BlockSpec((1, 2 * dff), fixed),
            pl.BlockSpec((dff, d), fixed, pipeline_mode=once),
            pl.BlockSpec((1, d), fixed),
        ],
        out_specs=pl.BlockSpec((tm, d), lambda i: (i, 0)),
        out_shape=jax.ShapeDtypeStruct((t, d), F32),
        scratch_shapes=[pltpu.VMEM((8, 2 * dff), F32), pltpu.VMEM((tm, d), F32)],
        compiler_params=_params("arbitrary"),
    )(x, gain, wup, cw, cb, wdn, gain_final)


def _rope_tables(seq):
    inv = ROPE_THETA ** (-jnp.arange(0, HEAD_DIM, 2, dtype=F32) / HEAD_DIM)
    ang = jnp.arange(seq, dtype=F32)[:, None] * inv[None, :]
    cos, sin = jnp.cos(ang), jnp.sin(ang)
    reps = LANES // HEAD_DIM
    return jnp.tile(cos, (1, 2 * reps)), jnp.tile(jnp.concatenate([-sin, sin], axis=1), (1, reps))


def _in_weight(w):
    kv = SWA_KV_HEADS * HEAD_DIM
    a, b = WIDTH + 2 * kv, WIDTH + 2 * kv + 3 * WIDTH
    qa, ka, va = w[:, :WIDTH], w[:, WIDTH:WIDTH + kv], w[:, WIDTH + kv:a]
    qc, kc, vc = w[:, b:b + WIDTH], w[:, b + WIDTH:b + 2 * WIDTH], w[:, b + 2 * WIDTH:b + 3 * WIDTH]
    return jnp.concatenate([qa, qc, kc, ka, w[:, a:b], vc, va, w[:, b + 3 * WIDTH:]], axis=1)


def kernel(x, norm_mix, w_in, b_gate, sinks, w_branch, w_out, norm_ffn, w_up, conv_w, conv_b, w_down, norm_final):
    batch, seq, d = x.shape
    depth = w_in.shape[0]
    tokens = batch * seq
    cos, sin_signed = _rope_tables(seq)
    xt = x.reshape(tokens, d)
    tm_tok = min(512, seq)
    for layer in range(depth):
        roped, plain, gates = _in_project(xt, norm_mix[layer].reshape(1, d), _in_weight(w_in[layer]).astype(BF16),
                                          cos, sin_signed, b_gate[layer].reshape(1, N_BRANCHES * d),
                                          seq=seq, tm=tm_tok, chunk=512)
        sink_rows = jnp.broadcast_to(sinks[layer].astype(F32)[:, None], (HEADS, LANES))
        o_a = _swa(roped, plain, sink_rows, batch=batch, seq=seq)
        o_b = _stick_breaking(plain, batch=batch, seq=seq)
        o_c = _moba(roped, plain, batch=batch, seq=seq)
        xt = _merge(o_a, o_b, o_c, gates, w_branch[layer].astype(BF16), w_out[layer].astype(BF16), xt, tm=tm_tok)
        xt = _ffn(xt, norm_ffn[layer].reshape(1, d), w_up[layer].astype(BF16), conv_w[layer],
                  conv_b[layer].reshape(1, -1), w_down[layer].astype(BF16), norm_final.reshape(1, d),
                  seq=seq, tm=tm_tok, chunk=768, final_norm=(layer == depth - 1))
    return xt.reshape(batch, seq, d)
```

```python
import functools

import jax
import jax.numpy as jnp
from jax import lax
from jax.experimental import pallas as pl
from jax.experimental.pallas import tpu as pltpu

F32 = jnp.float32
BF16 = jnp.bfloat16

HEAD_DIM = 64
LANES = 128
HEADS = 8
PAIRS = HEADS // 2
WIDTH = HEADS * HEAD_DIM
SWA_KV_HEADS = 2
SWA_WINDOW = 128
SWA_WINDOWS_PER_STEP = 4
SB_BLOCK = 256
SB_DONE = 104.0
MOBA_BLOCK = 256
MOBA_TOPK = 3
MOBA_MAX_BLOCKS = 8
N_BRANCHES = 3
ROPE_THETA = 10000.0
RMS_EPS = 1e-6
SCALE = HEAD_DIM ** -0.5
LOG2E = 1.4426950408889634
NEG = -1e30

VMEM_LIMIT = 56 * 1024 * 1024

ROPE_COLS = 3 * WIDTH + LANES
PLAIN_COLS = 4 * WIDTH + LANES
QA, QC, KC = range(3)
KA = 3 * PAIRS
QB, KB, VB, VC = range(4)
VA = 4 * PAIRS


def _params(*sem):
    return pltpu.CompilerParams(dimension_semantics=sem, vmem_limit_bytes=VMEM_LIMIT)


def _dot(a, b):
    return jnp.dot(a, b, preferred_element_type=F32)


def _dot_nt(a, b):
    return lax.dot_general(a, b, (((1,), (1,)), ((), ())), preferred_element_type=F32)


def _lane():
    return lax.broadcasted_iota(jnp.int32, (1, LANES), 1)


def _rope(piece, cos, sin_signed):
    first_half = (_lane() & 32) == 0
    nxt = pltpu.roll(piece, LANES - 32, axis=1)
    prv = pltpu.roll(piece, 32, axis=1)
    return piece * cos + jnp.where(first_half, nxt, prv) * sin_signed


def _in_proj_kernel(x_ref, g_ref, w_ref, cos_ref, sin_ref, bias_ref, rope_ref, plain_ref, gate_ref, *, chunk):
    x = x_ref[...]
    ms = jnp.mean(x * x, axis=-1, keepdims=True)
    h = (x * lax.rsqrt(ms + RMS_EPS) * g_ref[...]).astype(BF16)
    cos, sin_signed = cos_ref[...], sin_ref[...]
    base = 0
    for out_ref in (rope_ref, plain_ref, gate_ref):
        n = out_ref.shape[1]
        for c0 in range(0, n, chunk):
            c1 = min(c0 + chunk, n)
            acc = _dot(h, w_ref[:, base + c0:base + c1])
            if out_ref is rope_ref:
                for c in range(c0, c1, LANES):
                    out_ref[:, c:c + LANES] = _rope(acc[:, c - c0:c - c0 + LANES], cos, sin_signed).astype(out_ref.dtype)
            elif out_ref is gate_ref:
                out_ref[:, c0:c1] = (1.0 / (1.0 + jnp.exp(-(acc + bias_ref[:, c0:c1])))).astype(out_ref.dtype)
            else:
                out_ref[:, c0:c1] = acc.astype(out_ref.dtype)
        base += n


def _in_project(x, gain, w, cos, sin_signed, gate_bias, *, seq, tm, chunk):
    t, d = x.shape
    n_gate = w.shape[1] - ROPE_COLS - PLAIN_COLS
    assert t % tm == 0 and seq % tm == 0
    pos_tiles = seq // tm
    row = lambda i: (i, 0)
    return pl.pallas_call(
        functools.partial(_in_proj_kernel, chunk=chunk),
        grid=(t // tm,),
        in_specs=[
            pl.BlockSpec((tm, d), row),
            pl.BlockSpec((1, d), lambda i: (0, 0)),
            pl.BlockSpec(w.shape, lambda i: (0, 0), pipeline_mode=pl.Buffered(1)),
            pl.BlockSpec((tm, LANES), lambda i: (i % pos_tiles, 0)),
            pl.BlockSpec((tm, LANES), lambda i: (i % pos_tiles, 0)),
            pl.BlockSpec((1, n_gate), lambda i: (0, 0)),
        ],
        out_specs=[pl.BlockSpec((tm, ROPE_COLS), row), pl.BlockSpec((tm, PLAIN_COLS), row),
                   pl.BlockSpec((tm, n_gate), row)],
        out_shape=[jax.ShapeDtypeStruct((t, ROPE_COLS), BF16), jax.ShapeDtypeStruct((t, PLAIN_COLS), BF16),
                   jax.ShapeDtypeStruct((t, n_gate), BF16)],
        compiler_params=_params("arbitrary"),
    )(x, gain, w, cos, sin_signed, gate_bias)


def _swa_kernel(q_ref, k_ref, v_ref, sink_ref, o_ref, kat_ref, vt_ref):
    n = pl.program_id(1)
    w = SWA_WINDOW
    low = _lane() < HEAD_DIM
    chains = [(g, side) for g in range(SWA_KV_HEADS) for side in range(2)]

    @pl.when(n == 0)
    def _():
        k = k_ref[...].astype(F32)
        v = v_ref[...].astype(F32)
        k_sw = pltpu.roll(k, HEAD_DIM, axis=1)
        v_sw = pltpu.roll(v, HEAD_DIM, axis=1)
        k_at = [jnp.where(low, k, 0.0), jnp.where(low, 0.0, k_sw), jnp.where(low, k_sw, 0.0), jnp.where(low, 0.0, k)]
        v_at = [jnp.where(low, v, 1.0), jnp.where(low, 1.0, v_sw), jnp.where(low, v_sw, 1.0), jnp.where(low, 1.0, v)]
        for c in range(len(chains)):
            kat_ref[c] = k_at[c].astype(BF16)
            vt_ref[c] = v_at[c].T.astype(BF16)

    key_i = lax.broadcasted_iota(jnp.int32, (2 * w, 2 * w), 0)
    query_i = lax.broadcasted_iota(jnp.int32, (2 * w, 2 * w), 1) & (w - 1)
    sinks = []
    for g, side in chains:
        h0 = 4 * g + side
        sinks.append(jnp.concatenate([sink_ref[h0:h0 + 1, :], sink_ref[h0 + 2:h0 + 3, :]], axis=1))
    starts, scores = [], []
    for win in range(SWA_WINDOWS_PER_STEP):
        blk = n * SWA_WINDOWS_PER_STEP + win
        start = pl.multiple_of(jnp.maximum(blk - 1, 0) * w, w)
        diff = (blk * w + query_i) - (start + key_i)
        valid = (diff >= 0) & (diff < w)
        rows = slice(win * w, (win + 1) * w)
        qs = [jnp.concatenate([q_ref[rows, (2 * g) * LANES:(2 * g + 1) * LANES],
                               q_ref[rows, (2 * g + 1) * LANES:(2 * g + 2) * LANES]], axis=0) * SCALE
              for g in range(SWA_KV_HEADS)]
        starts.append(start)
        scores.append([jnp.where(valid, _dot_nt(kat_ref[c, pl.ds(start, 2 * w), :], qs[g]), NEG)
                       for c, (g, side) in enumerate(chains)])
    maxes = [[jnp.maximum(jnp.max(s, axis=0, keepdims=True), sink) for s, sink in zip(win_scores, sinks)]
             for win_scores in scores]
    probs = [[jnp.exp(s - m).astype(BF16) for s, m in zip(win_scores, win_maxes)]
             for win_scores, win_maxes in zip(scores, maxes)]
    outs = [[_dot(vt_ref[c, :, pl.ds(start, 2 * w)], p) for c, p in enumerate(win_probs)]
            for start, win_probs in zip(starts, probs)]
    for win in range(SWA_WINDOWS_PER_STEP):
        for g in range(SWA_KV_HEADS):
            o_lo, o_hi = outs[win][2 * g], outs[win][2 * g + 1]
            d_lo = o_lo[HEAD_DIM:] + jnp.exp(sinks[2 * g] - maxes[win][2 * g])
            d_hi = o_hi[:HEAD_DIM] + jnp.exp(sinks[2 * g + 1] - maxes[win][2 * g + 1])
            o = jnp.concatenate([o_lo[:HEAD_DIM] / d_lo, o_hi[HEAD_DIM:] / d_hi], axis=0).T.astype(o_ref.dtype)
            o_ref[win * w:(win + 1) * w, (2 * g) * LANES:(2 * g + 1) * LANES] = o[:w]
            o_ref[win * w:(win + 1) * w, (2 * g + 1) * LANES:(2 * g + 2) * LANES] = o[w:]


def _swa(roped, plain, sink_rows, *, batch, seq):
    tq = SWA_WINDOW * SWA_WINDOWS_PER_STEP
    steps = seq // tq
    return pl.pallas_call(
        _swa_kernel,
        grid=(batch, steps),
        in_specs=[
            pl.BlockSpec((tq, WIDTH), lambda b, n: (b * steps + n, QA)),
            pl.BlockSpec((seq, LANES), lambda b, n: (b, KA)),
            pl.BlockSpec((seq, LANES), lambda b, n: (b, VA)),
            pl.BlockSpec((HEADS, LANES), lambda b, n: (0, 0)),
        ],
        out_specs=pl.BlockSpec((tq, WIDTH), lambda b, n: (b * steps + n, 0)),
        out_shape=jax.ShapeDtypeStruct((batch * seq, WIDTH), BF16),
        scratch_shapes=[pltpu.VMEM((2 * SWA_KV_HEADS, seq, LANES), BF16), pltpu.VMEM((2 * SWA_KV_HEADS, LANES, seq), BF16)],
        compiler_params=_params("arbitrary", "arbitrary"),
    )(roped, roped, plain, sink_rows)


def _softplus(z):
    return jnp.maximum(z, 0.0) + jnp.log(1.0 + jnp.exp2(jnp.abs(z) * -LOG2E))


def _sb_kernel(q_ref, k_ref, v_ref, o_ref, kmask_ref, vt_ref, acc_ref):
    i = pl.program_id(1)
    t = SB_BLOCK
    nblocks = k_ref.shape[0] // t
    low = _lane() < HEAD_DIM
    mine = [low, jnp.logical_not(low)]
    chains = [(p, side) for p in range(PAIRS) for side in range(2)]

    @pl.when(i == 0)
    def _():
        for j in range(nblocks):
            rows = slice(j * t, (j + 1) * t)
            for side in range(2):
                kmask_ref[side, rows, :] = jnp.where(jnp.tile(mine[side], (1, PAIRS)), k_ref[rows, :], jnp.zeros((1, WIDTH), BF16))
                for p in range(PAIRS):
                    v = jnp.where(mine[side], v_ref[rows, p * LANES:(p + 1) * LANES].astype(F32), 0.0)
                    vt_ref[side, p * LANES:(p + 1) * LANES, rows] = v.T.astype(BF16)

    key = lax.broadcasted_iota(jnp.int32, (t, t), 0)
    query = lax.broadcasted_iota(jnp.int32, (t, t), 1)
    suffix = jnp.where(query >= key, 1.0, 0.0).astype(BF16)
    past = key < query
    qs = [q_ref[:, p * LANES:(p + 1) * LANES] * SCALE for p in range(PAIRS)]

    def block(j, remains, diagonal):
        off = pl.multiple_of(j * t, t)
        zs = [_dot_nt(kmask_ref[side, pl.ds(off, t), p * LANES:(p + 1) * LANES], qs[p]) for p, side in chains]
        sps = []
        for z in zs:
            sp = _softplus(z)
            sps.append(jnp.where(past, sp, 0.0) if diagonal else sp)
        cs = [_dot(suffix, sp.astype(BF16)) for sp in sps]
        wgts = []
        for n, (z, c) in enumerate(zip(zs, cs)):
            logw = z - c - remains[n]
            if diagonal:
                logw = jnp.where(past, logw, NEG)
            wgts.append(jnp.exp(logw).astype(BF16))
        pvs = [_dot(vt_ref[side, p * LANES:(p + 1) * LANES, pl.ds(off, t)], wgts[n]) for n, (p, side) in enumerate(chains)]
        for p in range(PAIRS):
            both = pvs[2 * p] + pvs[2 * p + 1]
            acc_ref[p] = both if diagonal else acc_ref[p] + both
        return tuple(remains[n] + cs[n][0:1, :] for n in range(len(chains)))

    def smallest(remains):
        return jnp.min(functools.reduce(jnp.minimum, remains))

    remains = block(i, tuple(jnp.zeros((1, t), F32) for _ in chains), True)

    def cond(state):
        s, floor, _ = state
        return (s < i) & (floor < SB_DONE)

    def body(state):
        s, _, remains = state
        remains = block(i - 1 - s, remains, False)
        return s + 1, smallest(remains), remains

    lax.while_loop(cond, body, (0, smallest(remains), remains))
    for p in range(PAIRS):
        o_ref[:, p * LANES:(p + 1) * LANES] = acc_ref[p].T.astype(o_ref.dtype)


def _stick_breaking(plain, *, batch, seq):
    t = SB_BLOCK
    nq = seq // t
    return pl.pallas_call(
        _sb_kernel,
        grid=(batch, nq),
        in_specs=[
            pl.BlockSpec((t, WIDTH), lambda b, i: (b * nq + i, QB)),
            pl.BlockSpec((seq, WIDTH), lambda b, i: (b, KB)),
            pl.BlockSpec((seq, WIDTH), lambda b, i: (b, VB)),
        ],
        out_specs=pl.BlockSpec((t, WIDTH), lambda b, i: (b * nq + i, 0)),
        out_shape=jax.ShapeDtypeStruct((batch * seq, WIDTH), BF16),
        scratch_shapes=[pltpu.VMEM((2, seq, WIDTH), BF16), pltpu.VMEM((2, WIDTH, seq), BF16),
                        pltpu.VMEM((PAIRS, LANES, t), F32)],
        compiler_params=_params("arbitrary", "arbitrary"),
    )(plain, plain, plain)


def _moba_kernel(q_ref, k_ref, v_ref, o_ref, kmean_ref, kmask_ref, vaug_ref, acc_ref, bias_ref, *, nblocks):
    i = pl.program_id(1)
    t = MOBA_BLOCK
    nbp = MOBA_MAX_BLOCKS
    low = _lane() < HEAD_DIM
    mine = [low, jnp.logical_not(low)]
    chains = [(p, side) for p in range(PAIRS) for side in range(2)]

    @pl.when(i == 0)
    def _():
        for p in range(PAIRS):
            rows = [jnp.mean(k_ref[j * t:(j + 1) * t, p * LANES:(p + 1) * LANES].astype(F32), axis=0, keepdims=True)
                    for j in range(nblocks)]
            rows += [jnp.zeros((1, LANES), F32)] * (nbp - nblocks)
            km = jnp.concatenate(rows, axis=0)
            kmean_ref[2 * p * nbp:(2 * p + 1) * nbp, :] = jnp.where(low, km, 0.0)
            kmean_ref[(2 * p + 1) * nbp:(2 * p + 2) * nbp, :] = jnp.where(low, 0.0, km)
        for j in range(nblocks):
            rows = slice(j * t, (j + 1) * t)
            for side in range(2):
                kmask_ref[side, rows, :] = jnp.where(jnp.tile(mine[side], (1, PAIRS)), k_ref[rows, :], jnp.zeros((1, WIDTH), BF16))
                for p in range(PAIRS):
                    v = jnp.where(mine[side], v_ref[rows, p * LANES:(p + 1) * LANES].astype(F32), 1.0)
                    vaug_ref[side, p * LANES:(p + 1) * LANES, rows] = v.T.astype(BF16)

    gates = []
    for p in range(PAIRS):
        q = q_ref[:, p * LANES:(p + 1) * LANES]
        km = kmean_ref[2 * p * nbp:(2 * p + 2) * nbp, :]
        km_hi = km.astype(BF16)
        km_lo = (km - km_hi.astype(F32)).astype(BF16)
        gates.append((_dot_nt(km_hi, q) + _dot_nt(km_lo, q)).reshape(2, nbp, t))
    blk = lax.broadcasted_iota(jnp.int32, (HEADS, nbp, t), 1)
    gate = jnp.where(blk < i, jnp.concatenate(gates, axis=0), NEG)
    count = jnp.zeros((HEADS, nbp, t), F32)
    for other in range(nblocks):
        g_other = gate[:, other:other + 1, :]
        beats = (g_other > gate) | ((g_other == gate) & (other < blk))
        count = count + jnp.where(beats, 1.0, 0.0)
    bias_ref[...] = jnp.where((count < min(MOBA_TOPK, nblocks)) & (blk < i), 0.0, NEG)
    qs = [q_ref[:, p * LANES:(p + 1) * LANES] * SCALE for p in range(PAIRS)]

    key = lax.broadcasted_iota(jnp.int32, (t, t), 0)
    query = lax.broadcasted_iota(jnp.int32, (t, t), 1)
    causal = key <= query

    def block(j, maxes, diagonal):
        off = pl.multiple_of(j * t, t)
        scores = [_dot_nt(kmask_ref[side, pl.ds(off, t), p * LANES:(p + 1) * LANES], qs[p]) for p, side in chains]
        if diagonal:
            scores = [jnp.where(causal, s, NEG) for s in scores]
            shift = [jnp.zeros((1, t), F32)] * len(chains)
        else:
            shift = [bias_ref[n, pl.ds(j, 1), :] for n in range(len(chains))]
        new_max = [jnp.maximum(m, jnp.max(s, axis=0, keepdims=True) + b) for m, s, b in zip(maxes, scores, shift)]
        probs = [jnp.exp(s - (m - b)).astype(BF16) for s, m, b in zip(scores, new_max, shift)]
        pvs = [_dot(vaug_ref[side, p * LANES:(p + 1) * LANES, pl.ds(off, t)], probs[n]) for n, (p, side) in enumerate(chains)]
        for n, pv in enumerate(pvs):
            acc_ref[n] = pv if diagonal else acc_ref[n] * jnp.exp(maxes[n] - new_max[n]) + pv
        return tuple(new_max)

    maxes = block(i, tuple(jnp.full((1, t), NEG, F32) for _ in chains), True)
    lax.fori_loop(0, i, lambda j, m: block(j, m, False), maxes)
    accs = [acc_ref[n] for n in range(len(chains))]
    for p in range(PAIRS):
        a_lo, a_hi = accs[2 * p], accs[2 * p + 1]
        o_t = jnp.concatenate([a_lo[:HEAD_DIM] / a_lo[HEAD_DIM:], a_hi[HEAD_DIM:] / a_hi[:HEAD_DIM]], axis=0)
        o_ref[:, p * LANES:(p + 1) * LANES] = o_t.T.astype(o_ref.dtype)


def _moba(roped, plain, *, batch, seq):
    t = MOBA_BLOCK
    nq = seq // t
    assert seq % t == 0 and nq <= MOBA_MAX_BLOCKS
    return pl.pallas_call(
        functools.partial(_moba_kernel, nblocks=nq),
        grid=(batch, nq),
        in_specs=[
            pl.BlockSpec((t, WIDTH), lambda b, i: (b * nq + i, QC)),
            pl.BlockSpec((seq, WIDTH), lambda b, i: (b, KC)),
            pl.BlockSpec((seq, WIDTH), lambda b, i: (b, VC)),
        ],
        out_specs=pl.BlockSpec((t, WIDTH), lambda b, i: (b * nq + i, 0)),
        out_shape=jax.ShapeDtypeStruct((batch * seq, WIDTH), BF16),
        scratch_shapes=[pltpu.VMEM((HEADS * MOBA_MAX_BLOCKS, LANES), F32), pltpu.VMEM((2, seq, WIDTH), BF16),
                        pltpu.VMEM((2, WIDTH, seq), BF16), pltpu.VMEM((HEADS, LANES, t), F32),
                        pltpu.VMEM((HEADS, MOBA_MAX_BLOCKS, t), F32)],
        compiler_params=_params("arbitrary", "arbitrary"),
    )(roped, roped, plain)


def _merge_kernel(oa_ref, ob_ref, oc_ref, gate_ref, wb_ref, wo_ref, x_ref, o_ref):
    d = x_ref.shape[1]
    ys = [_dot(branch[...], wb_ref[n]) for n, branch in enumerate((oa_ref, ob_ref, oc_ref))]
    merged = None
    for n, y in enumerate(ys):
        term = y * gate_ref[:, n * d:(n + 1) * d].astype(F32)
        merged = term if merged is None else merged + term
    o_ref[...] = x_ref[...] + _dot(merged.astype(BF16), wo_ref[...])


def _merge(oa, ob, oc, gates, wb, wo, x, *, tm):
    t, d = x.shape
    bw = oa.shape[1]
    row = lambda i: (i, 0)
    fixed2 = lambda i: (0, 0)
    return pl.pallas_call(
        _merge_kernel,
        grid=(t // tm,),
        in_specs=[
            pl.BlockSpec((tm, bw), row),
            pl.BlockSpec((tm, bw), row),
            pl.BlockSpec((tm, bw), row),
            pl.BlockSpec((tm, N_BRANCHES * d), row),
            pl.BlockSpec((N_BRANCHES, bw, d), lambda i: (0, 0, 0)),
            pl.BlockSpec((d, d), fixed2),
            pl.BlockSpec((tm, d), row),
        ],
        out_specs=pl.BlockSpec((tm, d), row),
        out_shape=jax.ShapeDtypeStruct((t, d), F32),
        compiler_params=_params("arbitrary"),
    )(oa, ob, oc, gates, wb, wo, x)


def _shift_rows(u, tail, shift):
    rolled = pltpu.roll(u, shift, axis=0)
    head = jnp.where(lax.broadcasted_iota(jnp.int32, tail.shape, 0) < shift,
                     pltpu.roll(tail, shift, axis=0), rolled[:8])
    return jnp.concatenate([head, rolled[8:]], axis=0)


def _ffn_kernel(x_ref, g_ref, wup_ref, cw_ref, cb_ref, wdn_ref, gf_ref, o_ref, tail_ref, acc_ref,
                *, tiles_per_seq, chunk, final_norm):
    i = pl.program_id(0)
    tm = x_ref.shape[0]
    dff = wdn_ref.shape[0]

    @pl.when(i % tiles_per_seq == 0)
    def _():
        tail_ref[...] = jnp.zeros(tail_ref.shape, F32)

    x = x_ref[...]
    ms = jnp.mean(x * x, axis=-1, keepdims=True)
    h = (x * lax.rsqrt(ms + RMS_EPS) * g_ref[...]).astype(BF16)

    def up(c0, c1):
        return _dot(h, wup_ref[:, c0:c1]), _dot(h, wup_ref[:, dff + c0:dff + c1])

    def conv(u, c0, c1):
        tail = tail_ref[:, c0:c1]
        tail_ref[:, c0:c1] = u[tm - 8:, :]
        out = cb_ref[:, c0:c1] + cw_ref[0:1, c0:c1] * _shift_rows(u, tail, 2)
        out = out + cw_ref[1:2, c0:c1] * _shift_rows(u, tail, 1)
        return out + cw_ref[2:3, c0:c1] * u

    bounds = [(c0, min(c0 + chunk, dff)) for c0 in range(0, dff, chunk)]
    pre = up(*bounds[0])
    for n, (c0, c1) in enumerate(bounds):
        cur = pre
        if n + 1 < len(bounds):
            pre = up(*bounds[n + 1])
        ua = conv(cur[0], c0, c1)
        uv = conv(cur[1], dff + c0, dff + c1)
        act = (ua / (1.0 + jnp.exp(-ua)) * uv).astype(BF16)
        part = _dot(act, wdn_ref[c0:c1, :])
        if n == 0:
            acc_ref[...] = x + part
        else:
            acc_ref[...] += part

    y = acc_ref[...]
    if final_norm:
        ms = jnp.mean(y * y, axis=-1, keepdims=True)
        y = y * lax.rsqrt(ms + RMS_EPS) * gf_ref[...]
    o_ref[...] = y


def _ffn(x, gain, wup, cw, cb, wdn, gain_final, *, seq, tm, chunk, final_norm):
    t, d = x.shape
    dff = wdn.shape[0]
    assert seq % tm == 0
    fixed = lambda i: (0, 0)
    once = pl.Buffered(1)
    return pl.pallas_call(
        functools.partial(_ffn_kernel, tiles_per_seq=seq // tm, chunk=chunk, final_norm=final_norm),
        grid=(t // tm,),
        in_specs=[
            pl.BlockSpec((tm, d), lambda i: (i, 0)),
            pl.BlockSpec((1, d), fixed),
            pl.BlockSpec((d, 2 * dff), fixed, pipeline_mode=once),
            pl.BlockSpec((3, 2 * dff), fixed),
            pl.BlockSpec((1, 2 * dff), fixed),
            pl.BlockSpec((dff, d), fixed, pipeline_mode=once),
            pl.BlockSpec((1, d), fixed),
        ],
        out_specs=pl.BlockSpec((tm, d), lambda i: (i, 0)),
        out_shape=jax.ShapeDtypeStruct((t, d), F32),
        scratch_shapes=[pltpu.VMEM((8, 2 * dff), F32), pltpu.VMEM((tm, d), F32)],
        compiler_params=_params("arbitrary"),
    )(x, gain, wup, cw, cb, wdn, gain_final)


def _rope_tables(seq):
    inv = ROPE_THETA ** (-jnp.arange(0, HEAD_DIM, 2, dtype=F32) / HEAD_DIM)
    ang = jnp.arange(seq, dtype=F32)[:, None] * inv[None, :]
    cos, sin = jnp.cos(ang), jnp.sin(ang)
    reps = LANES // HEAD_DIM
    return jnp.tile(cos, (1, 2 * reps)), jnp.tile(jnp.concatenate([-sin, sin], axis=1), (1, reps))


def _in_weight(w):
    kv = SWA_KV_HEADS * HEAD_DIM
    a, b = WIDTH + 2 * kv, WIDTH + 2 * kv + 3 * WIDTH
    qa, ka, va = w[:, :WIDTH], w[:, WIDTH:WIDTH + kv], w[:, WIDTH + kv:a]
    qc, kc, vc = w[:, b:b + WIDTH], w[:, b + WIDTH:b + 2 * WIDTH], w[:, b + 2 * WIDTH:b + 3 * WIDTH]
    return jnp.concatenate([qa, qc, kc, ka, w[:, a:b], vc, va, w[:, b + 3 * WIDTH:]], axis=1)


def kernel(x, norm_mix, w_in, b_gate, sinks, w_branch, w_out, norm_ffn, w_up, conv_w, conv_b, w_down, norm_final):
    batch, seq, d = x.shape
    depth = w_in.shape[0]
    tokens = batch * seq
    cos, sin_signed = _rope_tables(seq)
    xt = x.reshape(tokens, d)
    tm_tok = min(512, seq)
    for layer in range(depth):
        roped, plain, gates = _in_project(xt, norm_mix[layer].reshape(1, d), _in_weight(w_in[layer]).astype(BF16),
                                          cos, sin_signed, b_gate[layer].reshape(1, N_BRANCHES * d),
                                          seq=seq, tm=tm_tok, chunk=512)
        sink_rows = jnp.broadcast_to(sinks[layer].astype(F32)[:, None], (HEADS, LANES))
        o_a = _swa(roped, plain, sink_rows, batch=batch, seq=seq)
        o_b = _stick_breaking(plain, batch=batch, seq=seq)
        o_c = _moba(roped, plain, batch=batch, seq=seq)
        xt = _merge(o_a, o_b, o_c, gates, w_branch[layer].astype(BF16), w_out[layer].astype(BF16), xt, tm=tm_tok)
        xt = _ffn(xt, norm_ffn[layer].reshape(1, d), w_up[layer].astype(BF16), conv_w[layer],
                  conv_b[layer].reshape(1, -1), w_down[layer].astype(BF16), norm_final.reshape(1, d),
                  seq=seq, tm=tm_tok, chunk=768, final_norm=(layer == depth - 1))
    return xt.reshape(batch, seq, d)
```

```python
import functools

import jax
import jax.numpy as jnp
from jax import lax
from jax.experimental import pallas as pl
from jax.experimental.pallas import tpu as pltpu

F32 = jnp.float32
BF16 = jnp.bfloat16

HEAD_DIM = 64
LANES = 128
HEADS = 8
PAIRS = HEADS // 2
WIDTH = HEADS * HEAD_DIM
SWA_KV_HEADS = 2
SWA_WINDOW = 128
SWA_WINDOWS_PER_STEP = 4
SB_BLOCK = 256
SB_DONE = 104.0
MOBA_BLOCK = 256
MOBA_TOPK = 3
MOBA_MAX_BLOCKS = 8
N_BRANCHES = 3
ROPE_THETA = 10000.0
RMS_EPS = 1e-6
SCALE = HEAD_DIM ** -0.5
LOG2E = 1.4426950408889634
NEG = -1e30

VMEM_LIMIT = 56 * 1024 * 1024

ROPE_COLS = 3 * WIDTH + LANES
PLAIN_COLS = 4 * WIDTH + LANES
_KV = SWA_KV_HEADS * HEAD_DIM
_B0, _C0 = WIDTH + 2 * _KV, WIDTH + 2 * _KV + 3 * WIDTH
ROPE_SRC = ((0, WIDTH), (_C0, 2 * WIDTH), (WIDTH, _KV))
PLAIN_SRC = ((_B0, 3 * WIDTH), (_C0 + 2 * WIDTH, WIDTH), (WIDTH + _KV, _KV))
GATE_START = _C0 + 3 * WIDTH
QA, QC, KC = range(3)
KA = 3 * PAIRS
QB, KB, VB, VC = range(4)
VA = 4 * PAIRS


def _params(*sem):
    return pltpu.CompilerParams(dimension_semantics=sem, vmem_limit_bytes=VMEM_LIMIT)


def _dot(a, b):
    return jnp.dot(a, b, preferred_element_type=F32)


def _dot_nt(a, b):
    return lax.dot_general(a, b, (((1,), (1,)), ((), ())), preferred_element_type=F32)


def _lane():
    return lax.broadcasted_iota(jnp.int32, (1, LANES), 1)


def _rope(piece, cos, sin_signed):
    first_half = (_lane() & 32) == 0
    nxt = pltpu.roll(piece, LANES - 32, axis=1)
    prv = pltpu.roll(piece, 32, axis=1)
    return piece * cos + jnp.where(first_half, nxt, prv) * sin_signed


def _in_proj_kernel(x_ref, g_ref, w_ref, cos_ref, sin_ref, bias_ref, rope_ref, plain_ref, gate_ref, *, chunk):
    x = x_ref[...]
    ms = jnp.mean(x * x, axis=-1, keepdims=True)
    h = (x * lax.rsqrt(ms + RMS_EPS) * g_ref[...]).astype(BF16)
    cos, sin_signed = cos_ref[...], sin_ref[...]
    for out_ref, pieces in ((rope_ref, ROPE_SRC), (plain_ref, PLAIN_SRC), (gate_ref, ((GATE_START, gate_ref.shape[1]),))):
        dst = 0
        for src, width in pieces:
            for c0 in range(0, width, chunk):
                n = min(chunk, width - c0)
                acc = _dot(h, w_ref[:, src + c0:src + c0 + n].astype(BF16))
                lo = dst + c0
                if out_ref is rope_ref:
                    for c in range(0, n, LANES):
                        out_ref[:, lo + c:lo + c + LANES] = _rope(acc[:, c:c + LANES], cos, sin_signed).astype(out_ref.dtype)
                elif out_ref is gate_ref:
                    out_ref[:, lo:lo + n] = (1.0 / (1.0 + jnp.exp(-(acc + bias_ref[:, lo:lo + n])))).astype(out_ref.dtype)
                else:
                    out_ref[:, lo:lo + n] = acc.astype(out_ref.dtype)
            dst += width


def _in_project(x, gain, w, cos, sin_signed, gate_bias, *, seq, tm, chunk):
    t, d = x.shape
    n_gate = w.shape[1] - ROPE_COLS - PLAIN_COLS
    assert t % tm == 0 and seq % tm == 0
    pos_tiles = seq // tm
    row = lambda i: (i, 0)
    return pl.pallas_call(
        functools.partial(_in_proj_kernel, chunk=chunk),
        grid=(t // tm,),
        in_specs=[
            pl.BlockSpec((tm, d), row),
            pl.BlockSpec((1, d), lambda i: (0, 0)),
            pl.BlockSpec(w.shape, lambda i: (0, 0), pipeline_mode=pl.Buffered(1)),
            pl.BlockSpec((tm, LANES), lambda i: (i % pos_tiles, 0)),
            pl.BlockSpec((tm, LANES), lambda i: (i % pos_tiles, 0)),
            pl.BlockSpec((1, n_gate), lambda i: (0, 0)),
        ],
        out_specs=[pl.BlockSpec((tm, ROPE_COLS), row), pl.BlockSpec((tm, PLAIN_COLS), row),
                   pl.BlockSpec((tm, n_gate), row)],
        out_shape=[jax.ShapeDtypeStruct((t, ROPE_COLS), BF16), jax.ShapeDtypeStruct((t, PLAIN_COLS), BF16),
                   jax.ShapeDtypeStruct((t, n_gate), BF16)],
        compiler_params=_params("arbitrary"),
    )(x, gain, w, cos, sin_signed, gate_bias)


def _swa_kernel(q_ref, k_ref, v_ref, sink_ref, o_ref, kat_ref, vt_ref):
    n = pl.program_id(1)
    w = SWA_WINDOW
    low = _lane() < HEAD_DIM
    chains = [(g, side) for g in range(SWA_KV_HEADS) for side in range(2)]

    @pl.when(n == 0)
    def _():
        k = k_ref[...].astype(F32)
        v = v_ref[...].astype(F32)
        k_sw = pltpu.roll(k, HEAD_DIM, axis=1)
        v_sw = pltpu.roll(v, HEAD_DIM, axis=1)
        k_at = [jnp.where(low, k, 0.0), jnp.where(low, 0.0, k_sw), jnp.where(low, k_sw, 0.0), jnp.where(low, 0.0, k)]
        v_at = [jnp.where(low, v, 1.0), jnp.where(low, 1.0, v_sw), jnp.where(low, v_sw, 1.0), jnp.where(low, 1.0, v)]
        for c in range(len(chains)):
            kat_ref[c] = k_at[c].astype(BF16)
            vt_ref[c] = v_at[c].T.astype(BF16)

    key_i = lax.broadcasted_iota(jnp.int32, (2 * w, 2 * w), 0)
    query_i = lax.broadcasted_iota(jnp.int32, (2 * w, 2 * w), 1) & (w - 1)
    sinks = []
    for g, side in chains:
        h0 = 4 * g + side
        sinks.append(jnp.concatenate([sink_ref[h0:h0 + 1, :], sink_ref[h0 + 2:h0 + 3, :]], axis=1))
    starts, scores = [], []
    for win in range(SWA_WINDOWS_PER_STEP):
        blk = n * SWA_WINDOWS_PER_STEP + win
        start = pl.multiple_of(jnp.maximum(blk - 1, 0) * w, w)
        diff = (blk * w + query_i) - (start + key_i)
        valid = (diff >= 0) & (diff < w)
        rows = slice(win * w, (win + 1) * w)
        qs = [jnp.concatenate([q_ref[rows, (2 * g) * LANES:(2 * g + 1) * LANES],
                               q_ref[rows, (2 * g + 1) * LANES:(2 * g + 2) * LANES]], axis=0) * SCALE
              for g in range(SWA_KV_HEADS)]
        starts.append(start)
        scores.append([jnp.where(valid, _dot_nt(kat_ref[c, pl.ds(start, 2 * w), :], qs[g]), NEG)
                       for c, (g, side) in enumerate(chains)])
    maxes = [[jnp.maximum(jnp.max(s, axis=0, keepdims=True), sink) for s, sink in zip(win_scores, sinks)]
             for win_scores in scores]
    probs = [[jnp.exp(s - m).astype(BF16) for s, m in zip(win_scores, win_maxes)]
             for win_scores, win_maxes in zip(scores, maxes)]
    outs = [[_dot(vt_ref[c, :, pl.ds(start, 2 * w)], p) for c, p in enumerate(win_probs)]
            for start, win_probs in zip(starts, probs)]
    for win in range(SWA_WINDOWS_PER_STEP):
        for g in range(SWA_KV_HEADS):
            o_lo, o_hi = outs[win][2 * g], outs[win][2 * g + 1]
            d_lo = o_lo[HEAD_DIM:] + jnp.exp(sinks[2 * g] - maxes[win][2 * g])
            d_hi = o_hi[:HEAD_DIM] + jnp.exp(sinks[2 * g + 1] - maxes[win][2 * g + 1])
            o = jnp.concatenate([o_lo[:HEAD_DIM] / d_lo, o_hi[HEAD_DIM:] / d_hi], axis=0).T.astype(o_ref.dtype)
            o_ref[win * w:(win + 1) * w, (2 * g) * LANES:(2 * g + 1) * LANES] = o[:w]
            o_ref[win * w:(win + 1) * w, (2 * g + 1) * LANES:(2 * g + 2) * LANES] = o[w:]


def _swa(roped, plain, sink_rows, *, batch, seq):
    tq = SWA_WINDOW * SWA_WINDOWS_PER_STEP
    steps = seq // tq
    return pl.pallas_call(
        _swa_kernel,
        grid=(batch, steps),
        in_specs=[
            pl.BlockSpec((tq, WIDTH), lambda b, n: (b * steps + n, QA)),
            pl.BlockSpec((seq, LANES), lambda b, n: (b, KA)),
            pl.BlockSpec((seq, LANES), lambda b, n: (b, VA)),
            pl.BlockSpec((HEADS, LANES), lambda b, n: (0, 0)),
        ],
        out_specs=pl.BlockSpec((tq, WIDTH), lambda b, n: (b * steps + n, 0)),
        out_shape=jax.ShapeDtypeStruct((batch * seq, WIDTH), BF16),
        scratch_shapes=[pltpu.VMEM((2 * SWA_KV_HEADS, seq, LANES), BF16), pltpu.VMEM((2 * SWA_KV_HEADS, LANES, seq), BF16)],
        compiler_params=_params("arbitrary", "arbitrary"),
    )(roped, roped, plain, sink_rows)


def _softplus(z):
    return jnp.maximum(z, 0.0) + jnp.log(1.0 + jnp.exp2(jnp.abs(z) * -LOG2E))


def _sb_kernel(q_ref, k_ref, v_ref, o_ref, kmask_ref, vt_ref, acc_ref):
    i = pl.program_id(1)
    t = SB_BLOCK
    nblocks = k_ref.shape[0] // t
    low = _lane() < HEAD_DIM
    mine = [low, jnp.logical_not(low)]
    chains = [(p, side) for p in range(PAIRS) for side in range(2)]

    @pl.when(i == 0)
    def _():
        for j in range(nblocks):
            rows = slice(j * t, (j + 1) * t)
            for side in range(2):
                kmask_ref[side, rows, :] = jnp.where(jnp.tile(mine[side], (1, PAIRS)), k_ref[rows, :], jnp.zeros((1, WIDTH), BF16))
                for p in range(PAIRS):
                    v = jnp.where(mine[side], v_ref[rows, p * LANES:(p + 1) * LANES].astype(F32), 0.0)
                    vt_ref[side, p * LANES:(p + 1) * LANES, rows] = v.T.astype(BF16)

    key = lax.broadcasted_iota(jnp.int32, (t, t), 0)
    query = lax.broadcasted_iota(jnp.int32, (t, t), 1)
    suffix = jnp.where(query >= key, 1.0, 0.0).astype(BF16)
    past = key < query
    qs = [q_ref[:, p * LANES:(p + 1) * LANES] * SCALE for p in range(PAIRS)]

    def block(j, remains, diagonal):
        off = pl.multiple_of(j * t, t)
        zs = [_dot_nt(kmask_ref[side, pl.ds(off, t), p * LANES:(p + 1) * LANES], qs[p]) for p, side in chains]
        sps = []
        for z in zs:
            sp = _softplus(z)
            sps.append(jnp.where(past, sp, 0.0) if diagonal else sp)
        cs = [_dot(suffix, sp.astype(BF16)) for sp in sps]
        wgts = []
        for n, (z, c) in enumerate(zip(zs, cs)):
            logw = z - c - remains[n]
            if diagonal:
                logw = jnp.where(past, logw, NEG)
            wgts.append(jnp.exp(logw).astype(BF16))
        pvs = [_dot(vt_ref[side, p * LANES:(p + 1) * LANES, pl.ds(off, t)], wgts[n]) for n, (p, side) in enumerate(chains)]
        for p in range(PAIRS):
            both = pvs[2 * p] + pvs[2 * p + 1]
            acc_ref[p] = both if diagonal else acc_ref[p] + both
        return tuple(remains[n] + cs[n][0:1, :] for n in range(len(chains)))

    def smallest(remains):
        return jnp.min(functools.reduce(jnp.minimum, remains))

    remains = block(i, tuple(jnp.zeros((1, t), F32) for _ in chains), True)

    def cond(state):
        s, floor, _ = state
        return (s < i) & (floor < SB_DONE)

    def body(state):
        s, _, remains = state
        remains = block(i - 1 - s, remains, False)
        return s + 1, smallest(remains), remains

    lax.while_loop(cond, body, (0, smallest(remains), remains))
    for p in range(PAIRS):
        o_ref[:, p * LANES:(p + 1) * LANES] = acc_ref[p].T.astype(o_ref.dtype)


def _stick_breaking(plain, *, batch, seq):
    t = SB_BLOCK
    nq = seq // t
    return pl.pallas_call(
        _sb_kernel,
        grid=(batch, nq),
        in_specs=[
            pl.BlockSpec((t, WIDTH), lambda b, i: (b * nq + i, QB)),
            pl.BlockSpec((seq, WIDTH), lambda b, i: (b, KB)),
            pl.BlockSpec((seq, WIDTH), lambda b, i: (b, VB)),
        ],
        out_specs=pl.BlockSpec((t, WIDTH), lambda b, i: (b * nq + i, 0)),
        out_shape=jax.ShapeDtypeStruct((batch * seq, WIDTH), BF16),
        scratch_shapes=[pltpu.VMEM((2, seq, WIDTH), BF16), pltpu.VMEM((2, WIDTH, seq), BF16),
                        pltpu.VMEM((PAIRS, LANES, t), F32)],
        compiler_params=_params("arbitrary", "arbitrary"),
    )(plain, plain, plain)


def _moba_kernel(q_ref, k_ref, v_ref, o_ref, kmean_ref, kmask_ref, vaug_ref, acc_ref, bias_ref, prob_ref, *, nblocks):
    i = pl.program_id(1)
    t = MOBA_BLOCK
    nbp = MOBA_MAX_BLOCKS
    low = _lane() < HEAD_DIM
    mine = [low, jnp.logical_not(low)]
    chains = [(p, side) for p in range(PAIRS) for side in range(2)]

    @pl.when(i == 0)
    def _():
        for p in range(PAIRS):
            rows = [jnp.mean(k_ref[j * t:(j + 1) * t, p * LANES:(p + 1) * LANES].astype(F32), axis=0, keepdims=True)
                    for j in range(nblocks)]
            rows += [jnp.zeros((1, LANES), F32)] * (nbp - nblocks)
            km = jnp.concatenate(rows, axis=0)
            kmean_ref[2 * p * nbp:(2 * p + 1) * nbp, :] = jnp.where(low, km, 0.0)
            kmean_ref[(2 * p + 1) * nbp:(2 * p + 2) * nbp, :] = jnp.where(low, 0.0, km)
        for j in range(nblocks):
            rows = slice(j * t, (j + 1) * t)
            for side in range(2):
                kmask_ref[side, rows, :] = jnp.where(jnp.tile(mine[side], (1, PAIRS)), k_ref[rows, :], jnp.zeros((1, WIDTH), BF16))
                for p in range(PAIRS):
                    v = jnp.where(mine[side], v_ref[rows, p * LANES:(p + 1) * LANES].astype(F32), 1.0)
                    vaug_ref[side, p * LANES:(p + 1) * LANES, rows] = v.T.astype(BF16)

    gates = []
    for p in range(PAIRS):
        q = q_ref[:, p * LANES:(p + 1) * LANES]
        km = kmean_ref[2 * p * nbp:(2 * p + 2) * nbp, :]
        km_hi = km.astype(BF16)
        km_lo = (km - km_hi.astype(F32)).astype(BF16)
        gates.append((_dot_nt(km_hi, q) + _dot_nt(km_lo, q)).reshape(2, nbp, t))
    blk = lax.broadcasted_iota(jnp.int32, (HEADS, nbp, t), 1)
    gate = jnp.where(blk < i, jnp.concatenate(gates, axis=0), NEG)
    count = jnp.zeros((HEADS, nbp, t), F32)
    for other in range(nblocks):
        g_other = gate[:, other:other + 1, :]
        beats = (g_other > gate) | ((g_other == gate) & (other < blk))
        count = count + jnp.where(beats, 1.0, 0.0)
    bias_ref[...] = jnp.where((count < min(MOBA_TOPK, nblocks)) & (blk < i), 0.0, NEG)
    qs = [q_ref[:, p * LANES:(p + 1) * LANES] * SCALE for p in range(PAIRS)]

    key = lax.broadcasted_iota(jnp.int32, (t, t), 0)
    query = lax.broadcasted_iota(jnp.int32, (t, t), 1)
    causal = key <= query

    def score(j):
        off = pl.multiple_of(j * t, t)
        return [_dot_nt(kmask_ref[side, pl.ds(off, t), p * LANES:(p + 1) * LANES], qs[p]) for p, side in chains]

    def weigh(scores, shift, maxes):
        new_max = [jnp.maximum(m, jnp.max(s, axis=0, keepdims=True) + b) for m, s, b in zip(maxes, scores, shift)]
        for n, (s, m, b) in enumerate(zip(scores, new_max, shift)):
            prob_ref[n] = jnp.exp(s - (m - b)).astype(BF16)
        return tuple(new_max), tuple(jnp.exp(mo - mn) for mo, mn in zip(maxes, new_max))

    def absorb(j, decay):
        off = pl.multiple_of(j * t, t)
        pvs = [_dot(vaug_ref[side, p * LANES:(p + 1) * LANES, pl.ds(off, t)], prob_ref[n]) for n, (p, side) in enumerate(chains)]
        for n, pv in enumerate(pvs):
            acc_ref[n] = acc_ref[n] * decay[n] + pv

    def step(j, carry):
        prev, maxes, decay = carry
        scores = score(j)
        absorb(prev, decay)
        maxes, decay = weigh(scores, [bias_ref[n, pl.ds(j, 1), :] for n in range(len(chains))], maxes)
        return j, maxes, decay

    acc_ref[...] = jnp.zeros(acc_ref.shape, F32)
    own = [jnp.where(causal, s, NEG) for s in score(i)]
    start = tuple(jnp.full((1, t), NEG, F32) for _ in chains)
    maxes, decay = weigh(own, [jnp.zeros((1, t), F32)] * len(chains), start)
    last, _, decay = lax.fori_loop(0, i, step, (i, maxes, decay))
    absorb(last, decay)
    accs = [acc_ref[n] for n in range(len(chains))]
    for p in range(PAIRS):
        a_lo, a_hi = accs[2 * p], accs[2 * p + 1]
        o_t = jnp.concatenate([a_lo[:HEAD_DIM] / a_lo[HEAD_DIM:], a_hi[HEAD_DIM:] / a_hi[:HEAD_DIM]], axis=0)
        o_ref[:, p * LANES:(p + 1) * LANES] = o_t.T.astype(o_ref.dtype)


def _moba(roped, plain, *, batch, seq):
    t = MOBA_BLOCK
    nq = seq // t
    assert seq % t == 0 and nq <= MOBA_MAX_BLOCKS
    return pl.pallas_call(
        functools.partial(_moba_kernel, nblocks=nq),
        grid=(batch, nq),
        in_specs=[
            pl.BlockSpec((t, WIDTH), lambda b, i: (b * nq + i, QC)),
            pl.BlockSpec((seq, WIDTH), lambda b, i: (b, KC)),
            pl.BlockSpec((seq, WIDTH), lambda b, i: (b, VC)),
        ],
        out_specs=pl.BlockSpec((t, WIDTH), lambda b, i: (b * nq + i, 0)),
        out_shape=jax.ShapeDtypeStruct((batch * seq, WIDTH), BF16),
        scratch_shapes=[pltpu.VMEM((HEADS * MOBA_MAX_BLOCKS, LANES), F32), pltpu.VMEM((2, seq, WIDTH), BF16),
                        pltpu.VMEM((2, WIDTH, seq), BF16), pltpu.VMEM((HEADS, LANES, t), F32),
                        pltpu.VMEM((HEADS, MOBA_MAX_BLOCKS, t), F32), pltpu.VMEM((HEADS, t, t), BF16)],
        compiler_params=_params("arbitrary", "arbitrary"),
    )(roped, roped, plain)


def _merge_kernel(oa_ref, ob_ref, oc_ref, gate_ref, wb_ref, wo_ref, x_ref, o_ref):
    d = x_ref.shape[1]
    ys = [_dot(branch[...], wb_ref[n]) for n, branch in enumerate((oa_ref, ob_ref, oc_ref))]
    merged = None
    for n, y in enumerate(ys):
        term = y * gate_ref[:, n * d:(n + 1) * d].astype(F32)
        merged = term if merged is None else merged + term
    o_ref[...] = x_ref[...] + _dot(merged.astype(BF16), wo_ref[...])


def _merge(oa, ob, oc, gates, wb, wo, x, *, tm):
    t, d = x.shape
    bw = oa.shape[1]
    row = lambda i: (i, 0)
    fixed2 = lambda i: (0, 0)
    return pl.pallas_call(
        _merge_kernel,
        grid=(t // tm,),
        in_specs=[
            pl.BlockSpec((tm, bw), row),
            pl.BlockSpec((tm, bw), row),
            pl.BlockSpec((tm, bw), row),
            pl.BlockSpec((tm, N_BRANCHES * d), row),
            pl.BlockSpec((N_BRANCHES, bw, d), lambda i: (0, 0, 0)),
            pl.BlockSpec((d, d), fixed2),
            pl.BlockSpec((tm, d), row),
        ],
        out_specs=pl.BlockSpec((tm, d), row),
        out_shape=jax.ShapeDtypeStruct((t, d), F32),
        compiler_params=_params("arbitrary"),
    )(oa, ob, oc, gates, wb, wo, x)


def _shift_rows(u, tail, shift):
    rolled = pltpu.roll(u, shift, axis=0)
    head = jnp.where(lax.broadcasted_iota(jnp.int32, tail.shape, 0) < shift,
                     pltpu.roll(tail, shift, axis=0), rolled[:8])
    return jnp.concatenate([head, rolled[8:]], axis=0)


def _ffn_kernel(x_ref, g_ref, wup_ref, cw_ref, cb_ref, wdn_ref, gf_ref, o_ref, tail_ref, acc_ref,
                *, tiles_per_seq, chunk, final_norm):
    i = pl.program_id(0)
    tm = x_ref.shape[0]
    dff = wdn_ref.shape[0]

    @pl.when(i % tiles_per_seq == 0)
    def _():
        tail_ref[...] = jnp.zeros(tail_ref.shape, F32)

    x = x_ref[...]
    ms = jnp.mean(x * x, axis=-1, keepdims=True)
    h = (x * lax.rsqrt(ms + RMS_EPS) * g_ref[...]).astype(BF16)

    def up(c0, c1):
        return _dot(h, wup_ref[:, c0:c1]), _dot(h, wup_ref[:, dff + c0:dff + c1])

    def conv(u, c0, c1):
        tail = tail_ref[:, c0:c1]
        tail_ref[:, c0:c1] = u[tm - 8:, :]
        out = cb_ref[:, c0:c1] + cw_ref[0:1, c0:c1] * _shift_rows(u, tail, 2)
        out = out + cw_ref[1:2, c0:c1] * _shift_rows(u, tail, 1)
        return out + cw_ref[2:3, c0:c1] * u

    bounds = [(c0, min(c0 + chunk, dff)) for c0 in range(0, dff, chunk)]
    pre = up(*bounds[0])
    for n, (c0, c1) in enumerate(bounds):
        cur = pre
        if n + 1 < len(bounds):
            pre = up(*bounds[n + 1])
        ua = conv(cur[0], c0, c1)
        uv = conv(cur[1], dff + c0, dff + c1)
        act = (ua / (1.0 + jnp.exp(-ua)) * uv).astype(BF16)
        part = _dot(act, wdn_ref[c0:c1, :])
        if n == 0:
            acc_ref[...] = x + part
        else:
            acc_ref[...] += part

    y = acc_ref[...]
    if final_norm:
        ms = jnp.mean(y * y, axis=-1, keepdims=True)
        y = y * lax.rsqrt(ms + RMS_EPS) * gf_ref[...]
    o_ref[...] = y


def _ffn(x, gain, wup, cw, cb, wdn, gain_final, *, seq, tm, chunk, final_norm):
    t, d = x.shape
    dff = wdn.shape[0]
    assert seq % tm == 0
    fixed = lambda i: (0, 0)
    once = pl.Buffered(1)
    return pl.pallas_call(
        functools.partial(_ffn_kernel, tiles_per_seq=seq // tm, chunk=chunk, final_norm=final_norm),
        grid=(t // tm,),
        in_specs=[
            pl.BlockSpec((tm, d), lambda i: (i, 0)),
            pl.BlockSpec((1, d), fixed),
            pl.BlockSpec((d, 2 * dff), fixed, pipeline_mode=once),
            pl.BlockSpec((3, 2 * dff), fixed),
            pl.BlockSpec((1, 2 * dff), fixed),
            pl.BlockSpec((dff, d), fixed, pipeline_mode=once),
            pl.BlockSpec((1, d), fixed),
        ],
        out_specs=pl.BlockSpec((tm, d), lambda i: (i, 0)),
        out_shape=jax.ShapeDtypeStruct((t, d), F32),
        scratch_shapes=[pltpu.VMEM((8, 2 * dff), F32), pltpu.VMEM((tm, d), F32)],
        compiler_params=_params("arbitrary"),
    )(x, gain, wup, cw, cb, wdn, gain_final)


def _rope_tables(seq):
    inv = ROPE_THETA ** (-jnp.arange(0, HEAD_DIM, 2, dtype=F32) / HEAD_DIM)
    ang = jnp.arange(seq, dtype=F32)[:, None] * inv[None, :]
    cos, sin = jnp.cos(ang), jnp.sin(ang)
    reps = LANES // HEAD_DIM
    return jnp.tile(cos, (1, 2 * reps)), jnp.tile(jnp.concatenate([-sin, sin], axis=1), (1, reps))


def kernel(x, norm_mix, w_in, b_gate, sinks, w_branch, w_out, norm_ffn, w_up, conv_w, conv_b, w_down, norm_final):
    batch, seq, d = x.shape
    depth = w_in.shape[0]
    tokens = batch * seq
    cos, sin_signed = _rope_tables(seq)
    xt = x.reshape(tokens, d)
    tm_tok = min(512, seq)
    for layer in range(depth):
        roped, plain, gates = _in_project(xt, norm_mix[layer].reshape(1, d), w_in[layer],
                                          cos, sin_signed, b_gate[layer].reshape(1, N_BRANCHES * d),
                                          seq=seq, tm=tm_tok, chunk=512)
        sink_rows = jnp.broadcast_to(sinks[layer].astype(F32)[:, None], (HEADS, LANES))
        o_a = _swa(roped, plain, sink_rows, batch=batch, seq=seq)
        o_b = _stick_breaking(plain, batch=batch, seq=seq)
        o_c = _moba(roped, plain, batch=batch, seq=seq)
        xt = _merge(o_a, o_b, o_c, gates, w_branch[layer].astype(BF16), w_out[layer].astype(BF16), xt, tm=tm_tok)
        xt = _ffn(xt, norm_ffn[layer].reshape(1, d), w_up[layer].astype(BF16), conv_w[layer],
                  conv_b[layer].reshape(1, -1), w_down[layer].astype(BF16), norm_final.reshape(1, d),
                  seq=seq, tm=tm_tok, chunk=768, final_norm=(layer == depth - 1))
    return xt.reshape(batch, seq, d)
```

```python
import functools

import jax
import jax.numpy as jnp
from jax import lax
from jax.experimental import pallas as pl
from jax.experimental.pallas import tpu as pltpu

F32 = jnp.float32
BF16 = jnp.bfloat16

HEAD_DIM = 64
LANES = 128
HEADS = 8
PAIRS = HEADS // 2
WIDTH = HEADS * HEAD_DIM
SWA_KV_HEADS = 2
SWA_WINDOW = 128
SWA_WINDOWS_PER_STEP = 4
SB_BLOCK = 256
SB_DONE = 104.0
MOBA_BLOCK = 256
MOBA_TOPK = 3
MOBA_MAX_BLOCKS = 8
N_BRANCHES = 3
ROPE_THETA = 10000.0
RMS_EPS = 1e-6
SCALE = HEAD_DIM ** -0.5
LOG2E = 1.4426950408889634
NEG = -1e30

VMEM_LIMIT = 56 * 1024 * 1024

ROPE_COLS = 3 * WIDTH + LANES
PLAIN_COLS = 4 * WIDTH + LANES
_KV = SWA_KV_HEADS * HEAD_DIM
_B0, _C0 = WIDTH + 2 * _KV, WIDTH + 2 * _KV + 3 * WIDTH
ROPE_SRC = ((0, WIDTH), (_C0, 2 * WIDTH), (WIDTH, _KV))
PLAIN_SRC = ((_B0, 3 * WIDTH), (_C0 + 2 * WIDTH, WIDTH), (WIDTH + _KV, _KV))
GATE_START = _C0 + 3 * WIDTH
QA, QC, KC = range(3)
KA = 3 * PAIRS
QB, KB, VB, VC = range(4)
VA = 4 * PAIRS


def _params(*sem):
    return pltpu.CompilerParams(dimension_semantics=sem, vmem_limit_bytes=VMEM_LIMIT)


def _dot(a, b):
    return jnp.dot(a, b, preferred_element_type=F32)


def _dot_nt(a, b):
    return lax.dot_general(a, b, (((1,), (1,)), ((), ())), preferred_element_type=F32)


def _lane():
    return lax.broadcasted_iota(jnp.int32, (1, LANES), 1)


def _rope(piece, cos, sin_signed):
    first_half = (_lane() & 32) == 0
    nxt = pltpu.roll(piece, LANES - 32, axis=1)
    prv = pltpu.roll(piece, 32, axis=1)
    return piece * cos + jnp.where(first_half, nxt, prv) * sin_signed


def _in_proj_kernel(x_ref, g_ref, w_ref, cos_ref, sin_ref, bias_ref, rope_ref, plain_ref, gate_ref, *, chunk):
    x = x_ref[...]
    ms = jnp.mean(x * x, axis=-1, keepdims=True)
    h = (x * lax.rsqrt(ms + RMS_EPS) * g_ref[...]).astype(BF16)
    cos, sin_signed = cos_ref[...], sin_ref[...]
    for out_ref, pieces in ((rope_ref, ROPE_SRC), (plain_ref, PLAIN_SRC), (gate_ref, ((GATE_START, gate_ref.shape[1]),))):
        dst = 0
        for src, width in pieces:
            for c0 in range(0, width, chunk):
                n = min(chunk, width - c0)
                acc = _dot(h, w_ref[:, src + c0:src + c0 + n].astype(BF16))
                lo = dst + c0
                if out_ref is rope_ref:
                    for c in range(0, n, LANES):
                        out_ref[:, lo + c:lo + c + LANES] = _rope(acc[:, c:c + LANES], cos, sin_signed).astype(out_ref.dtype)
                elif out_ref is gate_ref:
                    out_ref[:, lo:lo + n] = (1.0 / (1.0 + jnp.exp(-(acc + bias_ref[:, lo:lo + n])))).astype(out_ref.dtype)
                else:
                    out_ref[:, lo:lo + n] = acc.astype(out_ref.dtype)
            dst += width


def _in_project(x, gain, w_layers, layer, cos, sin_signed, gate_bias, *, seq, tm, chunk):
    t, d = x.shape
    n_in = w_layers.shape[2]
    n_gate = n_in - ROPE_COLS - PLAIN_COLS
    assert t % tm == 0 and seq % tm == 0
    pos_tiles = seq // tm
    row = lambda i: (i, 0)
    return pl.pallas_call(
        functools.partial(_in_proj_kernel, chunk=chunk),
        grid=(t // tm,),
        in_specs=[
            pl.BlockSpec((tm, d), row),
            pl.BlockSpec((1, d), lambda i: (0, 0)),
            pl.BlockSpec((None, d, n_in), lambda i: (layer, 0, 0), pipeline_mode=pl.Buffered(1)),
            pl.BlockSpec((tm, LANES), lambda i: (i % pos_tiles, 0)),
            pl.BlockSpec((tm, LANES), lambda i: (i % pos_tiles, 0)),
            pl.BlockSpec((1, n_gate), lambda i: (0, 0)),
        ],
        out_specs=[pl.BlockSpec((tm, ROPE_COLS), row), pl.BlockSpec((tm, PLAIN_COLS), row),
                   pl.BlockSpec((tm, n_gate), row)],
        out_shape=[jax.ShapeDtypeStruct((t, ROPE_COLS), BF16), jax.ShapeDtypeStruct((t, PLAIN_COLS), BF16),
                   jax.ShapeDtypeStruct((t, n_gate), BF16)],
        compiler_params=_params("arbitrary"),
    )(x, gain, w_layers, cos, sin_signed, gate_bias)


def _swa_kernel(q_ref, k_ref, v_ref, sink_ref, o_ref, kat_ref, vt_ref):
    n = pl.program_id(1)
    w = SWA_WINDOW
    low = _lane() < HEAD_DIM
    chains = [(g, side) for g in range(SWA_KV_HEADS) for side in range(2)]

    @pl.when(n == 0)
    def _():
        k = k_ref[...].astype(F32)
        v = v_ref[...].astype(F32)
        k_sw = pltpu.roll(k, HEAD_DIM, axis=1)
        v_sw = pltpu.roll(v, HEAD_DIM, axis=1)
        k_at = [jnp.where(low, k, 0.0), jnp.where(low, 0.0, k_sw), jnp.where(low, k_sw, 0.0), jnp.where(low, 0.0, k)]
        v_at = [jnp.where(low, v, 1.0), jnp.where(low, 1.0, v_sw), jnp.where(low, v_sw, 1.0), jnp.where(low, 1.0, v)]
        for c in range(len(chains)):
            kat_ref[c] = k_at[c].astype(BF16)
            vt_ref[c] = v_at[c].T.astype(BF16)

    key_i = lax.broadcasted_iota(jnp.int32, (2 * w, 2 * w), 0)
    query_i = lax.broadcasted_iota(jnp.int32, (2 * w, 2 * w), 1) & (w - 1)
    sinks = []
    for g, side in chains:
        h0 = 4 * g + side
        sinks.append(jnp.concatenate([sink_ref[h0:h0 + 1, :], sink_ref[h0 + 2:h0 + 3, :]], axis=1))
    starts, scores = [], []
    for win in range(SWA_WINDOWS_PER_STEP):
        blk = n * SWA_WINDOWS_PER_STEP + win
        start = pl.multiple_of(jnp.maximum(blk - 1, 0) * w, w)
        diff = (blk * w + query_i) - (start + key_i)
        valid = (diff >= 0) & (diff < w)
        rows = slice(win * w, (win + 1) * w)
        qs = [jnp.concatenate([q_ref[rows, (2 * g) * LANES:(2 * g + 1) * LANES],
                               q_ref[rows, (2 * g + 1) * LANES:(2 * g + 2) * LANES]], axis=0) * SCALE
              for g in range(SWA_KV_HEADS)]
        starts.append(start)
        scores.append([jnp.where(valid, _dot_nt(kat_ref[c, pl.ds(start, 2 * w), :], qs[g]), NEG)
                       for c, (g, side) in enumerate(chains)])
    maxes = [[jnp.maximum(jnp.max(s, axis=0, keepdims=True), sink) for s, sink in zip(win_scores, sinks)]
             for win_scores in scores]
    probs = [[jnp.exp(s - m).astype(BF16) for s, m in zip(win_scores, win_maxes)]
             for win_scores, win_maxes in zip(scores, maxes)]
    outs = [[_dot(vt_ref[c, :, pl.ds(start, 2 * w)], p) for c, p in enumerate(win_probs)]
            for start, win_probs in zip(starts, probs)]
    for win in range(SWA_WINDOWS_PER_STEP):
        for g in range(SWA_KV_HEADS):
            o_lo, o_hi = outs[win][2 * g], outs[win][2 * g + 1]
            d_lo = o_lo[HEAD_DIM:] + jnp.exp(sinks[2 * g] - maxes[win][2 * g])
            d_hi = o_hi[:HEAD_DIM] + jnp.exp(sinks[2 * g + 1] - maxes[win][2 * g + 1])
            o = jnp.concatenate([o_lo[:HEAD_DIM] / d_lo, o_hi[HEAD_DIM:] / d_hi], axis=0).T.astype(o_ref.dtype)
            o_ref[win * w:(win + 1) * w, (2 * g) * LANES:(2 * g + 1) * LANES] = o[:w]
            o_ref[win * w:(win + 1) * w, (2 * g + 1) * LANES:(2 * g + 2) * LANES] = o[w:]


def _swa(roped, plain, sink_rows, *, batch, seq):
    tq = SWA_WINDOW * SWA_WINDOWS_PER_STEP
    steps = seq // tq
    return pl.pallas_call(
        _swa_kernel,
        grid=(batch, steps),
        in_specs=[
            pl.BlockSpec((tq, WIDTH), lambda b, n: (b * steps + n, QA)),
            pl.BlockSpec((seq, LANES), lambda b, n: (b, KA)),
            pl.BlockSpec((seq, LANES), lambda b, n: (b, VA)),
            pl.BlockSpec((HEADS, LANES), lambda b, n: (0, 0)),
        ],
        out_specs=pl.BlockSpec((tq, WIDTH), lambda b, n: (b * steps + n, 0)),
        out_shape=jax.ShapeDtypeStruct((batch * seq, WIDTH), BF16),
        scratch_shapes=[pltpu.VMEM((2 * SWA_KV_HEADS, seq, LANES), BF16), pltpu.VMEM((2 * SWA_KV_HEADS, LANES, seq), BF16)],
        compiler_params=_params("arbitrary", "arbitrary"),
    )(roped, roped, plain, sink_rows)


def _softplus(z):
    return jnp.maximum(z, 0.0) + jnp.log(1.0 + jnp.exp2(jnp.abs(z) * -LOG2E))


def _sb_kernel(q_ref, k_ref, v_ref, o_ref, kmask_ref, vt_ref, acc_ref):
    i = pl.program_id(1)
    t = SB_BLOCK
    nblocks = k_ref.shape[0] // t
    low = _lane() < HEAD_DIM
    mine = [low, jnp.logical_not(low)]
    chains = [(p, side) for p in range(PAIRS) for side in range(2)]

    @pl.when(i == 0)
    def _():
        for j in range(nblocks):
            rows = slice(j * t, (j + 1) * t)
            for side in range(2):
                kmask_ref[side, rows, :] = jnp.where(jnp.tile(mine[side], (1, PAIRS)), k_ref[rows, :], jnp.zeros((1, WIDTH), BF16))
                for p in range(PAIRS):
                    v = jnp.where(mine[side], v_ref[rows, p * LANES:(p + 1) * LANES].astype(F32), 0.0)
                    vt_ref[side, p * LANES:(p + 1) * LANES, rows] = v.T.astype(BF16)

    key = lax.broadcasted_iota(jnp.int32, (t, t), 0)
    query = lax.broadcasted_iota(jnp.int32, (t, t), 1)
    suffix = jnp.where(query >= key, 1.0, 0.0).astype(BF16)
    past = key < query
    qs = [q_ref[:, p * LANES:(p + 1) * LANES] * SCALE for p in range(PAIRS)]

    def block(j, remains, diagonal):
        off = pl.multiple_of(j * t, t)
        zs = [_dot_nt(kmask_ref[side, pl.ds(off, t), p * LANES:(p + 1) * LANES], qs[p]) for p, side in chains]
        sps = []
        for z in zs:
            sp = _softplus(z)
            sps.append(jnp.where(past, sp, 0.0) if diagonal else sp)
        cs = [_dot(suffix, sp.astype(BF16)) for sp in sps]
        wgts = []
        for n, (z, c) in enumerate(zip(zs, cs)):
            logw = z - c - remains[n]
            if diagonal:
                logw = jnp.where(past, logw, NEG)
            wgts.append(jnp.exp(logw).astype(BF16))
        pvs = [_dot(vt_ref[side, p * LANES:(p + 1) * LANES, pl.ds(off, t)], wgts[n]) for n, (p, side) in enumerate(chains)]
        for p in range(PAIRS):
            both = pvs[2 * p] + pvs[2 * p + 1]
            acc_ref[p] = both if diagonal else acc_ref[p] + both
        return tuple(remains[n] + cs[n][0:1, :] for n in range(len(chains)))

    def smallest(remains):
        return jnp.min(functools.reduce(jnp.minimum, remains))

    remains = block(i, tuple(jnp.zeros((1, t), F32) for _ in chains), True)

    def cond(state):
        s, floor, _ = state
        return (s < i) & (floor < SB_DONE)

    def body(state):
        s, _, remains = state
        remains = block(i - 1 - s, remains, False)
        return s + 1, smallest(remains), remains

    lax.while_loop(cond, body, (0, smallest(remains), remains))
    for p in range(PAIRS):
        o_ref[:, p * LANES:(p + 1) * LANES] = acc_ref[p].T.astype(o_ref.dtype)


def _stick_breaking(plain, *, batch, seq):
    t = SB_BLOCK
    nq = seq // t
    return pl.pallas_call(
        _sb_kernel,
        grid=(batch, nq),
        in_specs=[
            pl.BlockSpec((t, WIDTH), lambda b, i: (b * nq + i, QB)),
            pl.BlockSpec((seq, WIDTH), lambda b, i: (b, KB)),
            pl.BlockSpec((seq, WIDTH), lambda b, i: (b, VB)),
        ],
        out_specs=pl.BlockSpec((t, WIDTH), lambda b, i: (b * nq + i, 0)),
        out_shape=jax.ShapeDtypeStruct((batch * seq, WIDTH), BF16),
        scratch_shapes=[pltpu.VMEM((2, seq, WIDTH), BF16), pltpu.VMEM((2, WIDTH, seq), BF16),
                        pltpu.VMEM((PAIRS, LANES, t), F32)],
        compiler_params=_params("arbitrary", "arbitrary"),
    )(plain, plain, plain)


def _moba_kernel(q_ref, k_ref, v_ref, o_ref, kmean_ref, kmask_ref, vaug_ref, acc_ref, bias_ref, prob_ref, *, nblocks):
    i = pl.program_id(1)
    t = MOBA_BLOCK
    nbp = MOBA_MAX_BLOCKS
    low = _lane() < HEAD_DIM
    mine = [low, jnp.logical_not(low)]
    chains = [(p, side) for p in range(PAIRS) for side in range(2)]

    @pl.when(i == 0)
    def _():
        for p in range(PAIRS):
            rows = [jnp.mean(k_ref[j * t:(j + 1) * t, p * LANES:(p + 1) * LANES].astype(F32), axis=0, keepdims=True)
                    for j in range(nblocks)]
            rows += [jnp.zeros((1, LANES), F32)] * (nbp - nblocks)
            km = jnp.concatenate(rows, axis=0)
            kmean_ref[2 * p * nbp:(2 * p + 1) * nbp, :] = jnp.where(low, km, 0.0)
            kmean_ref[(2 * p + 1) * nbp:(2 * p + 2) * nbp, :] = jnp.where(low, 0.0, km)
        for j in range(nblocks):
            rows = slice(j * t, (j + 1) * t)
            for side in range(2):
                kmask_ref[side, rows, :] = jnp.where(jnp.tile(mine[side], (1, PAIRS)), k_ref[rows, :], jnp.zeros((1, WIDTH), BF16))
                for p in range(PAIRS):
                    v = jnp.where(mine[side], v_ref[rows, p * LANES:(p + 1) * LANES].astype(F32), 1.0)
                    vaug_ref[side, p * LANES:(p + 1) * LANES, rows] = v.T.astype(BF16)

    gates = []
    for p in range(PAIRS):
        q = q_ref[:, p * LANES:(p + 1) * LANES]
        km = kmean_ref[2 * p * nbp:(2 * p + 2) * nbp, :]
        km_hi = km.astype(BF16)
        km_lo = (km - km_hi.astype(F32)).astype(BF16)
        gates.append((_dot_nt(km_hi, q) + _dot_nt(km_lo, q)).reshape(2, nbp, t))
    blk = lax.broadcasted_iota(jnp.int32, (HEADS, nbp, t), 1)
    gate = jnp.where(blk < i, jnp.concatenate(gates, axis=0), NEG)
    count = jnp.zeros((HEADS, nbp, t), F32)
    for other in range(nblocks):
        g_other = gate[:, other:other + 1, :]
        beats = (g_other > gate) | ((g_other == gate) & (other < blk))
        count = count + jnp.where(beats, 1.0, 0.0)
    bias_ref[...] = jnp.where((count < min(MOBA_TOPK, nblocks)) & (blk < i), 0.0, NEG)
    qs = [q_ref[:, p * LANES:(p + 1) * LANES] * SCALE for p in range(PAIRS)]

    key = lax.broadcasted_iota(jnp.int32, (t, t), 0)
    query = lax.broadcasted_iota(jnp.int32, (t, t), 1)
    causal = key <= query

    def score(j):
        off = pl.multiple_of(j * t, t)
        return [_dot_nt(kmask_ref[side, pl.ds(off, t), p * LANES:(p + 1) * LANES], qs[p]) for p, side in chains]

    def weigh(scores, shift, maxes):
        new_max = [jnp.maximum(m, jnp.max(s, axis=0, keepdims=True) + b) for m, s, b in zip(maxes, scores, shift)]
        for n, (s, m, b) in enumerate(zip(scores, new_max, shift)):
            prob_ref[n] = jnp.exp(s - (m - b)).astype(BF16)
        return tuple(new_max), tuple(jnp.exp(mo - mn) for mo, mn in zip(maxes, new_max))

    def absorb(j, decay):
        off = pl.multiple_of(j * t, t)
        pvs = [_dot(vaug_ref[side, p * LANES:(p + 1) * LANES, pl.ds(off, t)], prob_ref[n]) for n, (p, side) in enumerate(chains)]
        for n, pv in enumerate(pvs):
            acc_ref[n] = acc_ref[n] * decay[n] + pv

    def step(j, carry):
        prev, maxes, decay = carry
        scores = score(j)
        absorb(prev, decay)
        maxes, decay = weigh(scores, [bias_ref[n, pl.ds(j, 1), :] for n in range(len(chains))], maxes)
        return j, maxes, decay

    acc_ref[...] = jnp.zeros(acc_ref.shape, F32)
    own = [jnp.where(causal, s, NEG) for s in score(i)]
    start = tuple(jnp.full((1, t), NEG, F32) for _ in chains)
    maxes, decay = weigh(own, [jnp.zeros((1, t), F32)] * len(chains), start)
    last, _, decay = lax.fori_loop(0, i, step, (i, maxes, decay))
    absorb(last, decay)
    accs = [acc_ref[n] for n in range(len(chains))]
    for p in range(PAIRS):
        a_lo, a_hi = accs[2 * p], accs[2 * p + 1]
        o_t = jnp.concatenate([a_lo[:HEAD_DIM] / a_lo[HEAD_DIM:], a_hi[HEAD_DIM:] / a_hi[:HEAD_DIM]], axis=0)
        o_ref[:, p * LANES:(p + 1) * LANES] = o_t.T.astype(o_ref.dtype)


def _moba(roped, plain, *, batch, seq):
    t = MOBA_BLOCK
    nq = seq // t
    assert seq % t == 0 and nq <= MOBA_MAX_BLOCKS
    return pl.pallas_call(
        functools.partial(_moba_kernel, nblocks=nq),
        grid=(batch, nq),
        in_specs=[
            pl.BlockSpec((t, WIDTH), lambda b, i: (b * nq + i, QC)),
            pl.BlockSpec((seq, WIDTH), lambda b, i: (b, KC)),
            pl.BlockSpec((seq, WIDTH), lambda b, i: (b, VC)),
        ],
        out_specs=pl.BlockSpec((t, WIDTH), lambda b, i: (b * nq + i, 0)),
        out_shape=jax.ShapeDtypeStruct((batch * seq, WIDTH), BF16),
        scratch_shapes=[pltpu.VMEM((HEADS * MOBA_MAX_BLOCKS, LANES), F32), pltpu.VMEM((2, seq, WIDTH), BF16),
                        pltpu.VMEM((2, WIDTH, seq), BF16), pltpu.VMEM((HEADS, LANES, t), F32),
                        pltpu.VMEM((HEADS, MOBA_MAX_BLOCKS, t), F32), pltpu.VMEM((HEADS, t, t), BF16)],
        compiler_params=_params("arbitrary", "arbitrary"),
    )(roped, roped, plain)


def _merge_kernel(oa_ref, ob_ref, oc_ref, gate_ref, wb_ref, wo_ref, x_ref, o_ref):
    d = x_ref.shape[1]
    ys = [_dot(branch[...], wb_ref[n]) for n, branch in enumerate((oa_ref, ob_ref, oc_ref))]
    merged = None
    for n, y in enumerate(ys):
        term = y * gate_ref[:, n * d:(n + 1) * d].astype(F32)
        merged = term if merged is None else merged + term
    o_ref[...] = x_ref[...] + _dot(merged.astype(BF16), wo_ref[...])


def _merge(oa, ob, oc, gates, wb_layers, wo_layers, layer, x, *, tm):
    t, d = x.shape
    bw = oa.shape[1]
    row = lambda i: (i, 0)
    return pl.pallas_call(
        _merge_kernel,
        grid=(t // tm,),
        in_specs=[
            pl.BlockSpec((tm, bw), row),
            pl.BlockSpec((tm, bw), row),
            pl.BlockSpec((tm, bw), row),
            pl.BlockSpec((tm, N_BRANCHES * d), row),
            pl.BlockSpec((None, N_BRANCHES, bw, d), lambda i: (layer, 0, 0, 0)),
            pl.BlockSpec((None, d, d), lambda i: (layer, 0, 0)),
            pl.BlockSpec((tm, d), row),
        ],
        out_specs=pl.BlockSpec((tm, d), row),
        out_shape=jax.ShapeDtypeStruct((t, d), F32),
        compiler_params=_params("arbitrary"),
    )(oa, ob, oc, gates, wb_layers, wo_layers, x)


def _shift_rows(u, tail, shift):
    rolled = pltpu.roll(u, shift, axis=0)
    head = jnp.where(lax.broadcasted_iota(jnp.int32, tail.shape, 0) < shift,
                     pltpu.roll(tail, shift, axis=0), rolled[:8])
    return jnp.concatenate([head, rolled[8:]], axis=0)


def _ffn_kernel(x_ref, g_ref, wup_ref, cw_ref, cb_ref, wdn_ref, gf_ref, o_ref, tail_ref, acc_ref,
                *, tiles_per_seq, chunk, final_norm):
    i = pl.program_id(0)
    tm = x_ref.shape[0]
    dff = wdn_ref.shape[0]

    @pl.when(i % tiles_per_seq == 0)
    def _():
        tail_ref[...] = jnp.zeros(tail_ref.shape, F32)

    x = x_ref[...]
    ms = jnp.mean(x * x, axis=-1, keepdims=True)
    h = (x * lax.rsqrt(ms + RMS_EPS) * g_ref[...]).astype(BF16)

    def up(c0, c1):
        return _dot(h, wup_ref[:, c0:c1]), _dot(h, wup_ref[:, dff + c0:dff + c1])

    def conv(u, c0, c1):
        tail = tail_ref[:, c0:c1]
        tail_ref[:, c0:c1] = u[tm - 8:, :]
        out = cb_ref[:, c0:c1] + cw_ref[0:1, c0:c1] * _shift_rows(u, tail, 2)
        out = out + cw_ref[1:2, c0:c1] * _shift_rows(u, tail, 1)
        return out + cw_ref[2:3, c0:c1] * u

    bounds = [(c0, min(c0 + chunk, dff)) for c0 in range(0, dff, chunk)]
    pre = up(*bounds[0])
    for n, (c0, c1) in enumerate(bounds):
        cur = pre
        if n + 1 < len(bounds):
            pre = up(*bounds[n + 1])
        ua = conv(cur[0], c0, c1)
        uv = conv(cur[1], dff + c0, dff + c1)
        act = (ua / (1.0 + jnp.exp(-ua)) * uv).astype(BF16)
        part = _dot(act, wdn_ref[c0:c1, :])
        if n == 0:
            acc_ref[...] = x + part
        else:
            acc_ref[...] += part

    y = acc_ref[...]
    if final_norm:
        ms = jnp.mean(y * y, axis=-1, keepdims=True)
        y = y * lax.rsqrt(ms + RMS_EPS) * gf_ref[...]
    o_ref[...] = y


def _ffn(x, gain, wup_layers, cw, cb, wdn_layers, layer, gain_final, *, seq, tm, chunk, final_norm):
    t, d = x.shape
    dff = wdn_layers.shape[1]
    assert seq % tm == 0
    fixed = lambda i: (0, 0)
    picked = lambda i: (layer, 0, 0)
    once = pl.Buffered(1)
    return pl.pallas_call(
        functools.partial(_ffn_kernel, tiles_per_seq=seq // tm, chunk=chunk, final_norm=final_norm),
        grid=(t // tm,),
        in_specs=[
            pl.BlockSpec((tm, d), lambda i: (i, 0)),
            pl.BlockSpec((1, d), fixed),
            pl.BlockSpec((None, d, 2 * dff), picked, pipeline_mode=once),
            pl.BlockSpec((3, 2 * dff), fixed),
            pl.BlockSpec((1, 2 * dff), fixed),
            pl.BlockSpec((None, dff, d), picked, pipeline_mode=once),
            pl.BlockSpec((1, d), fixed),
        ],
        out_specs=pl.BlockSpec((tm, d), lambda i: (i, 0)),
        out_shape=jax.ShapeDtypeStruct((t, d), F32),
        scratch_shapes=[pltpu.VMEM((8, 2 * dff), F32), pltpu.VMEM((tm, d), F32)],
        compiler_params=_params("arbitrary"),
    )(x, gain, wup_layers, cw, cb, wdn_layers, gain_final)


def _rope_tables(seq):
    inv = ROPE_THETA ** (-jnp.arange(0, HEAD_DIM, 2, dtype=F32) / HEAD_DIM)
    ang = jnp.arange(seq, dtype=F32)[:, None] * inv[None, :]
    cos, sin = jnp.cos(ang), jnp.sin(ang)
    reps = LANES // HEAD_DIM
    return jnp.tile(cos, (1, 2 * reps)), jnp.tile(jnp.concatenate([-sin, sin], axis=1), (1, reps))


def kernel(x, norm_mix, w_in, b_gate, sinks, w_branch, w_out, norm_ffn, w_up, conv_w, conv_b, w_down, norm_final):
    batch, seq, d = x.shape
    depth = w_in.shape[0]
    tokens = batch * seq
    cos, sin_signed = _rope_tables(seq)
    xt = x.reshape(tokens, d)
    tm_tok = min(512, seq)
    wb, wo, wup, wdn = (w.astype(BF16) for w in (w_branch, w_out, w_up, w_down))
    for layer in range(depth):
        roped, plain, gates = _in_project(xt, norm_mix[layer].reshape(1, d), w_in, layer,
                                          cos, sin_signed, b_gate[layer].reshape(1, N_BRANCHES * d),
                                          seq=seq, tm=tm_tok, chunk=512)
        sink_rows = jnp.broadcast_to(sinks[layer].astype(F32)[:, None], (HEADS, LANES))
        o_a = _swa(roped, plain, sink_rows, batch=batch, seq=seq)
        o_b = _stick_breaking(plain, batch=batch, seq=seq)
        o_c = _moba(roped, plain, batch=batch, seq=seq)
        xt = _merge(o_a, o_b, o_c, gates, wb, wo, layer, xt, tm=tm_tok)
        xt = _ffn(xt, norm_ffn[layer].reshape(1, d), wup, conv_w[layer],
                  conv_b[layer].reshape(1, -1), wdn, layer, norm_final.reshape(1, d),
                  seq=seq, tm=tm_tok, chunk=768, final_norm=(layer == depth - 1))
    return xt.reshape(batch, seq, d)
```

```python
import functools
import math

import jax
import jax.numpy as jnp
from jax import lax
from jax.experimental import pallas as pl
from jax.experimental.pallas import tpu as pltpu

F32 = jnp.float32
BF16 = jnp.bfloat16

HEAD_DIM = 64
LANES = 128
HEADS = 8
PAIRS = HEADS // 2
WIDTH = HEADS * HEAD_DIM
SWA_KV_HEADS = 2
SWA_WINDOW = 128
SWA_MAX_WINDOWS_PER_STEP = 8
SB_BLOCK = 256
SB_DONE = 104.0
MOBA_BLOCK = 256
MOBA_TOPK = 3
MOBA_MAX_BLOCKS = 8
N_BRANCHES = 3
ROPE_THETA = 10000.0
RMS_EPS = 1e-6
SCALE = HEAD_DIM ** -0.5
LOG2E = 1.4426950408889634
NEG = -1e30

VMEM_LIMIT = 56 * 1024 * 1024

ROPE_COLS = 3 * WIDTH + LANES
PLAIN_COLS = 4 * WIDTH + LANES
_KV = SWA_KV_HEADS * HEAD_DIM
_B0, _C0 = WIDTH + 2 * _KV, WIDTH + 2 * _KV + 3 * WIDTH
ROPE_SRC = ((0, WIDTH), (_C0, 2 * WIDTH), (WIDTH, _KV))
PLAIN_SRC = ((_B0, 3 * WIDTH), (_C0 + 2 * WIDTH, WIDTH), (WIDTH + _KV, _KV))
GATE_START = _C0 + 3 * WIDTH
QA, QC, KC = range(3)
KA = 3 * PAIRS
QB, KB, VB, VC = range(4)
VA = 4 * PAIRS


def _params(*sem):
    return pltpu.CompilerParams(dimension_semantics=sem, vmem_limit_bytes=VMEM_LIMIT)


def _dot(a, b):
    return jnp.dot(a, b, preferred_element_type=F32)


def _dot_nt(a, b):
    return lax.dot_general(a, b, (((1,), (1,)), ((), ())), preferred_element_type=F32)


def _lane():
    return lax.broadcasted_iota(jnp.int32, (1, LANES), 1)


def _rope(piece, cos, sin_signed):
    first_half = (_lane() & 32) == 0
    nxt = pltpu.roll(piece, LANES - 32, axis=1)
    prv = pltpu.roll(piece, 32, axis=1)
    return piece * cos + jnp.where(first_half, nxt, prv) * sin_signed


def _in_proj_kernel(x_ref, g_ref, w_ref, cos_ref, sin_ref, bias_ref, rope_ref, plain_ref, gate_ref, *, chunk):
    x = x_ref[...]
    ms = jnp.mean(x * x, axis=-1, keepdims=True)
    h = (x * lax.rsqrt(ms + RMS_EPS) * g_ref[...]).astype(BF16)
    cos, sin_signed = cos_ref[...], sin_ref[...]
    for out_ref, pieces in ((rope_ref, ROPE_SRC), (plain_ref, PLAIN_SRC), (gate_ref, ((GATE_START, gate_ref.shape[1]),))):
        dst = 0
        for src, width in pieces:
            for c0 in range(0, width, chunk):
                n = min(chunk, width - c0)
                acc = _dot(h, w_ref[:, src + c0:src + c0 + n].astype(BF16))
                lo = dst + c0
                if out_ref is rope_ref:
                    for c in range(0, n, LANES):
                        out_ref[:, lo + c:lo + c + LANES] = _rope(acc[:, c:c + LANES], cos, sin_signed).astype(out_ref.dtype)
                elif out_ref is gate_ref:
                    out_ref[:, lo:lo + n] = (1.0 / (1.0 + jnp.exp(-(acc + bias_ref[:, lo:lo + n])))).astype(out_ref.dtype)
                else:
                    out_ref[:, lo:lo + n] = acc.astype(out_ref.dtype)
            dst += width


def _in_project(x, gain, w_layers, layer, cos, sin_signed, gate_bias, *, seq, tm, chunk):
    t, d = x.shape
    n_in = w_layers.shape[2]
    n_gate = n_in - ROPE_COLS - PLAIN_COLS
    assert t % tm == 0 and seq % tm == 0
    pos_tiles = seq // tm
    row = lambda i: (i, 0)
    return pl.pallas_call(
        functools.partial(_in_proj_kernel, chunk=chunk),
        grid=(t // tm,),
        in_specs=[
            pl.BlockSpec((tm, d), row),
            pl.BlockSpec((1, d), lambda i: (0, 0)),
            pl.BlockSpec((None, d, n_in), lambda i: (layer, 0, 0), pipeline_mode=pl.Buffered(1)),
            pl.BlockSpec((tm, LANES), lambda i: (i % pos_tiles, 0)),
            pl.BlockSpec((tm, LANES), lambda i: (i % pos_tiles, 0)),
            pl.BlockSpec((1, n_gate), lambda i: (0, 0)),
        ],
        out_specs=[pl.BlockSpec((tm, ROPE_COLS), row), pl.BlockSpec((tm, PLAIN_COLS), row),
                   pl.BlockSpec((tm, n_gate), row)],
        out_shape=[jax.ShapeDtypeStruct((t, ROPE_COLS), BF16), jax.ShapeDtypeStruct((t, PLAIN_COLS), BF16),
                   jax.ShapeDtypeStruct((t, n_gate), BF16)],
        compiler_params=_params("arbitrary"),
    )(x, gain, w_layers, cos, sin_signed, gate_bias)


def _swa_kernel(q_ref, k_ref, v_ref, sink_ref, o_ref, kat_ref, vt_ref):
    n = pl.program_id(1)
    w = SWA_WINDOW
    windows = q_ref.shape[0] // w
    low = _lane() < HEAD_DIM
    chains = [(g, side) for g in range(SWA_KV_HEADS) for side in range(2)]

    @pl.when(n == 0)
    def _():
        k = k_ref[...].astype(F32)
        v = v_ref[...].astype(F32)
        k_sw = pltpu.roll(k, HEAD_DIM, axis=1)
        v_sw = pltpu.roll(v, HEAD_DIM, axis=1)
        k_at = [jnp.where(low, k, 0.0), jnp.where(low, 0.0, k_sw), jnp.where(low, k_sw, 0.0), jnp.where(low, 0.0, k)]
        v_at = [jnp.where(low, v, 1.0), jnp.where(low, 1.0, v_sw), jnp.where(low, v_sw, 1.0), jnp.where(low, 1.0, v)]
        for c in range(len(chains)):
            kat_ref[c] = k_at[c].astype(BF16)
            vt_ref[c] = v_at[c].T.astype(BF16)

    key_i = lax.broadcasted_iota(jnp.int32, (2 * w, 2 * w), 0)
    query_i = lax.broadcasted_iota(jnp.int32, (2 * w, 2 * w), 1) & (w - 1)
    sinks = []
    for g, side in chains:
        h0 = 4 * g + side
        sinks.append(jnp.concatenate([sink_ref[h0:h0 + 1, :], sink_ref[h0 + 2:h0 + 3, :]], axis=1))
    starts, scores = [], []
    for win in range(windows):
        blk = n * windows + win
        start = pl.multiple_of(jnp.maximum(blk - 1, 0) * w, w)
        diff = (blk * w + query_i) - (start + key_i)
        valid = (diff >= 0) & (diff < w)
        rows = slice(win * w, (win + 1) * w)
        qs = [jnp.concatenate([q_ref[rows, (2 * g) * LANES:(2 * g + 1) * LANES],
                               q_ref[rows, (2 * g + 1) * LANES:(2 * g + 2) * LANES]], axis=0) * SCALE
              for g in range(SWA_KV_HEADS)]
        starts.append(start)
        scores.append([jnp.where(valid, _dot_nt(kat_ref[c, pl.ds(start, 2 * w), :], qs[g]), NEG)
                       for c, (g, side) in enumerate(chains)])
    maxes = [[jnp.maximum(jnp.max(s, axis=0, keepdims=True), sink) for s, sink in zip(win_scores, sinks)]
             for win_scores in scores]
    probs = [[jnp.exp(s - m).astype(BF16) for s, m in zip(win_scores, win_maxes)]
             for win_scores, win_maxes in zip(scores, maxes)]
    outs = [[_dot(vt_ref[c, :, pl.ds(start, 2 * w)], p) for c, p in enumerate(win_probs)]
            for start, win_probs in zip(starts, probs)]
    for win in range(windows):
        for g in range(SWA_KV_HEADS):
            o_lo, o_hi = outs[win][2 * g], outs[win][2 * g + 1]
            d_lo = o_lo[HEAD_DIM:] + jnp.exp(sinks[2 * g] - maxes[win][2 * g])
            d_hi = o_hi[:HEAD_DIM] + jnp.exp(sinks[2 * g + 1] - maxes[win][2 * g + 1])
            o = jnp.concatenate([o_lo[:HEAD_DIM] / d_lo, o_hi[HEAD_DIM:] / d_hi], axis=0).T.astype(o_ref.dtype)
            o_ref[win * w:(win + 1) * w, (2 * g) * LANES:(2 * g + 1) * LANES] = o[:w]
            o_ref[win * w:(win + 1) * w, (2 * g + 1) * LANES:(2 * g + 2) * LANES] = o[w:]


def _swa(roped, plain, sink_rows, *, batch, seq):
    tq = SWA_WINDOW * math.gcd(seq // SWA_WINDOW, SWA_MAX_WINDOWS_PER_STEP)
    steps = seq // tq
    return pl.pallas_call(
        _swa_kernel,
        grid=(batch, steps),
        in_specs=[
            pl.BlockSpec((tq, WIDTH), lambda b, n: (b * steps + n, QA)),
            pl.BlockSpec((seq, LANES), lambda b, n: (b, KA)),
            pl.BlockSpec((seq, LANES), lambda b, n: (b, VA)),
            pl.BlockSpec((HEADS, LANES), lambda b, n: (0, 0)),
        ],
        out_specs=pl.BlockSpec((tq, WIDTH), lambda b, n: (b * steps + n, 0)),
        out_shape=jax.ShapeDtypeStruct((batch * seq, WIDTH), BF16),
        scratch_shapes=[pltpu.VMEM((2 * SWA_KV_HEADS, seq, LANES), BF16), pltpu.VMEM((2 * SWA_KV_HEADS, LANES, seq), BF16)],
        compiler_params=_params("arbitrary", "arbitrary"),
    )(roped, roped, plain, sink_rows)


def _softplus(z):
    return jnp.maximum(z, 0.0) + jnp.log(1.0 + jnp.exp2(jnp.abs(z) * -LOG2E))


def _sb_kernel(q_ref, k_ref, v_ref, o_ref, kmask_ref, vt_ref, acc_ref):
    i = pl.program_id(1)
    t = SB_BLOCK
    nblocks = k_ref.shape[0] // t
    low = _lane() < HEAD_DIM
    mine = [low, jnp.logical_not(low)]
    chains = [(p, side) for p in range(PAIRS) for side in range(2)]

    @pl.when(i == 0)
    def _():
        for j in range(nblocks):
            rows = slice(j * t, (j + 1) * t)
            for side in range(2):
                kmask_ref[side, rows, :] = jnp.where(jnp.tile(mine[side], (1, PAIRS)), k_ref[rows, :], jnp.zeros((1, WIDTH), BF16))
                for p in range(PAIRS):
                    v = jnp.where(mine[side], v_ref[rows, p * LANES:(p + 1) * LANES].astype(F32), 0.0)
                    vt_ref[side, p * LANES:(p + 1) * LANES, rows] = v.T.astype(BF16)

    key = lax.broadcasted_iota(jnp.int32, (t, t), 0)
    query = lax.broadcasted_iota(jnp.int32, (t, t), 1)
    suffix = jnp.where(query >= key, 1.0, 0.0).astype(BF16)
    past = key < query
    qs = [q_ref[:, p * LANES:(p + 1) * LANES] * SCALE for p in range(PAIRS)]

    def block(j, remains, diagonal):
        off = pl.multiple_of(j * t, t)
        zs = [_dot_nt(kmask_ref[side, pl.ds(off, t), p * LANES:(p + 1) * LANES], qs[p]) for p, side in chains]
        sps = []
        for z in zs:
            sp = _softplus(z)
            sps.append(jnp.where(past, sp, 0.0) if diagonal else sp)
        cs = [_dot(suffix, sp.astype(BF16)) for sp in sps]
        wgts = []
        for n, (z, c) in enumerate(zip(zs, cs)):
            logw = z - c - remains[n]
            if diagonal:
                logw = jnp.where(past, logw, NEG)
            wgts.append(jnp.exp(logw).astype(BF16))
        pvs = [_dot(vt_ref[side, p * LANES:(p + 1) * LANES, pl.ds(off, t)], wgts[n]) for n, (p, side) in enumerate(chains)]
        for p in range(PAIRS):
            both = pvs[2 * p] + pvs[2 * p + 1]
            acc_ref[p] = both if diagonal else acc_ref[p] + both
        return tuple(remains[n] + cs[n][0:1, :] for n in range(len(chains)))

    def smallest(remains):
        return jnp.min(functools.reduce(jnp.minimum, remains))

    remains = block(i, tuple(jnp.zeros((1, t), F32) for _ in chains), True)

    def cond(state):
        s, floor, _ = state
        return (s < i) & (floor < SB_DONE)

    def body(state):
        s, _, remains = state
        remains = block(i - 1 - s, remains, False)
        return s + 1, smallest(remains), remains

    lax.while_loop(cond, body, (0, smallest(remains), remains))
    for p in range(PAIRS):
        o_ref[:, p * LANES:(p + 1) * LANES] = acc_ref[p].T.astype(o_ref.dtype)


def _stick_breaking(plain, *, batch, seq):
    t = SB_BLOCK
    nq = seq // t
    return pl.pallas_call(
        _sb_kernel,
        grid=(batch, nq),
        in_specs=[
            pl.BlockSpec((t, WIDTH), lambda b, i: (b * nq + i, QB)),
            pl.BlockSpec((seq, WIDTH), lambda b, i: (b, KB)),
            pl.BlockSpec((seq, WIDTH), lambda b, i: (b, VB)),
        ],
        out_specs=pl.BlockSpec((t, WIDTH), lambda b, i: (b * nq + i, 0)),
        out_shape=jax.ShapeDtypeStruct((batch * seq, WIDTH), BF16),
        scratch_shapes=[pltpu.VMEM((2, seq, WIDTH), BF16), pltpu.VMEM((2, WIDTH, seq), BF16),
                        pltpu.VMEM((PAIRS, LANES, t), F32)],
        compiler_params=_params("arbitrary", "arbitrary"),
    )(plain, plain, plain)


def _moba_kernel(q_ref, k_ref, v_ref, o_ref, kmean_ref, kmask_ref, vaug_ref, acc_ref, bias_ref, prob_ref, *, nblocks):
    i = pl.program_id(1)
    t = MOBA_BLOCK
    nbp = MOBA_MAX_BLOCKS
    low = _lane() < HEAD_DIM
    mine = [low, jnp.logical_not(low)]
    chains = [(p, side) for p in range(PAIRS) for side in range(2)]

    @pl.when(i == 0)
    def _():
        for p in range(PAIRS):
            rows = [jnp.mean(k_ref[j * t:(j + 1) * t, p * LANES:(p + 1) * LANES].astype(F32), axis=0, keepdims=True)
                    for j in range(nblocks)]
            rows += [jnp.zeros((1, LANES), F32)] * (nbp - nblocks)
            km = jnp.concatenate(rows, axis=0)
            kmean_ref[2 * p * nbp:(2 * p + 1) * nbp, :] = jnp.where(low, km, 0.0)
            kmean_ref[(2 * p + 1) * nbp:(2 * p + 2) * nbp, :] = jnp.where(low, 0.0, km)
        for j in range(nblocks):
            rows = slice(j * t, (j + 1) * t)
            for side in range(2):
                kmask_ref[side, rows, :] = jnp.where(jnp.tile(mine[side], (1, PAIRS)), k_ref[rows, :], jnp.zeros((1, WIDTH), BF16))
                for p in range(PAIRS):
                    v = jnp.where(mine[side], v_ref[rows, p * LANES:(p + 1) * LANES].astype(F32), 1.0)
                    vaug_ref[side, p * LANES:(p + 1) * LANES, rows] = v.T.astype(BF16)

    gates = []
    for p in range(PAIRS):
        q = q_ref[:, p * LANES:(p + 1) * LANES]
        km = kmean_ref[2 * p * nbp:(2 * p + 2) * nbp, :]
        km_hi = km.astype(BF16)
        km_lo = (km - km_hi.astype(F32)).astype(BF16)
        gates.append((_dot_nt(km_hi, q) + _dot_nt(km_lo, q)).reshape(2, nbp, t))
    blk = lax.broadcasted_iota(jnp.int32, (HEADS, nbp, t), 1)
    gate = jnp.where(blk < i, jnp.concatenate(gates, axis=0), NEG)
    count = jnp.zeros((HEADS, nbp, t), F32)
    for other in range(nblocks):
        g_other = gate[:, other:other + 1, :]
        beats = (g_other > gate) | ((g_other == gate) & (other < blk))
        count = count + jnp.where(beats, 1.0, 0.0)
    bias_ref[...] = jnp.where((count < min(MOBA_TOPK, nblocks)) & (blk < i), 0.0, NEG)
    qs = [q_ref[:, p * LANES:(p + 1) * LANES] * SCALE for p in range(PAIRS)]

    key = lax.broadcasted_iota(jnp.int32, (t, t), 0)
    query = lax.broadcasted_iota(jnp.int32, (t, t), 1)
    causal = key <= query

    def score(j):
        off = pl.multiple_of(j * t, t)
        return [_dot_nt(kmask_ref[side, pl.ds(off, t), p * LANES:(p + 1) * LANES], qs[p]) for p, side in chains]

    def weigh(scores, shift, maxes):
        new_max = [jnp.maximum(m, jnp.max(s, axis=0, keepdims=True) + b) for m, s, b in zip(maxes, scores, shift)]
        for n, (s, m, b) in enumerate(zip(scores, new_max, shift)):
            prob_ref[n] = jnp.exp(s - (m - b)).astype(BF16)
        return tuple(new_max), tuple(jnp.exp(mo - mn) for mo, mn in zip(maxes, new_max))

    def absorb(j, decay):
        off = pl.multiple_of(j * t, t)
        pvs = [_dot(vaug_ref[side, p * LANES:(p + 1) * LANES, pl.ds(off, t)], prob_ref[n]) for n, (p, side) in enumerate(chains)]
        for n, pv in enumerate(pvs):
            acc_ref[n] = acc_ref[n] * decay[n] + pv

    def step(j, carry):
        prev, maxes, decay = carry
        scores = score(j)
        absorb(prev, decay)
        maxes, decay = weigh(scores, [bias_ref[n, pl.ds(j, 1), :] for n in range(len(chains))], maxes)
        return j, maxes, decay

    acc_ref[...] = jnp.zeros(acc_ref.shape, F32)
    own = [jnp.where(causal, s, NEG) for s in score(i)]
    start = tuple(jnp.full((1, t), NEG, F32) for _ in chains)
    maxes, decay = weigh(own, [jnp.zeros((1, t), F32)] * len(chains), start)
    last, _, decay = lax.fori_loop(0, i, step, (i, maxes, decay))
    absorb(last, decay)
    accs = [acc_ref[n] for n in range(len(chains))]
    for p in range(PAIRS):
        a_lo, a_hi = accs[2 * p], accs[2 * p + 1]
        o_t = jnp.concatenate([a_lo[:HEAD_DIM] / a_lo[HEAD_DIM:], a_hi[HEAD_DIM:] / a_hi[:HEAD_DIM]], axis=0)
        o_ref[:, p * LANES:(p + 1) * LANES] = o_t.T.astype(o_ref.dtype)


def _moba(roped, plain, *, batch, seq):
    t = MOBA_BLOCK
    nq = seq // t
    assert seq % t == 0 and nq <= MOBA_MAX_BLOCKS
    return pl.pallas_call(
        functools.partial(_moba_kernel, nblocks=nq),
        grid=(batch, nq),
        in_specs=[
            pl.BlockSpec((t, WIDTH), lambda b, i: (b * nq + i, QC)),
            pl.BlockSpec((seq, WIDTH), lambda b, i: (b, KC)),
            pl.BlockSpec((seq, WIDTH), lambda b, i: (b, VC)),
        ],
        out_specs=pl.BlockSpec((t, WIDTH), lambda b, i: (b * nq + i, 0)),
        out_shape=jax.ShapeDtypeStruct((batch * seq, WIDTH), BF16),
        scratch_shapes=[pltpu.VMEM((HEADS * MOBA_MAX_BLOCKS, LANES), F32), pltpu.VMEM((2, seq, WIDTH), BF16),
                        pltpu.VMEM((2, WIDTH, seq), BF16), pltpu.VMEM((HEADS, LANES, t), F32),
                        pltpu.VMEM((HEADS, MOBA_MAX_BLOCKS, t), F32), pltpu.VMEM((HEADS, t, t), BF16)],
        compiler_params=_params("arbitrary", "arbitrary"),
    )(roped, roped, plain)


def _merge_kernel(oa_ref, ob_ref, oc_ref, gate_ref, wb_ref, wo_ref, x_ref, o_ref):
    d = x_ref.shape[1]
    ys = [_dot(branch[...], wb_ref[n].astype(BF16)) for n, branch in enumerate((oa_ref, ob_ref, oc_ref))]
    merged = None
    for n, y in enumerate(ys):
        term = y * gate_ref[:, n * d:(n + 1) * d].astype(F32)
        merged = term if merged is None else merged + term
    o_ref[...] = x_ref[...] + _dot(merged.astype(BF16), wo_ref[...].astype(BF16))


def _merge(oa, ob, oc, gates, wb_layers, wo_layers, layer, x, *, tm):
    t, d = x.shape
    bw = oa.shape[1]
    row = lambda i: (i, 0)
    return pl.pallas_call(
        _merge_kernel,
        grid=(t // tm,),
        in_specs=[
            pl.BlockSpec((tm, bw), row),
            pl.BlockSpec((tm, bw), row),
            pl.BlockSpec((tm, bw), row),
            pl.BlockSpec((tm, N_BRANCHES * d), row),
            pl.BlockSpec((None, N_BRANCHES, bw, d), lambda i: (layer, 0, 0, 0), pipeline_mode=pl.Buffered(1)),
            pl.BlockSpec((None, d, d), lambda i: (layer, 0, 0), pipeline_mode=pl.Buffered(1)),
            pl.BlockSpec((tm, d), row),
        ],
        out_specs=pl.BlockSpec((tm, d), row),
        out_shape=jax.ShapeDtypeStruct((t, d), F32),
        compiler_params=_params("arbitrary"),
    )(oa, ob, oc, gates, wb_layers, wo_layers, x)


def _shift_rows(u, tail, shift):
    rolled = pltpu.roll(u, shift, axis=0)
    head = jnp.where(lax.broadcasted_iota(jnp.int32, tail.shape, 0) < shift,
                     pltpu.roll(tail, shift, axis=0), rolled[:8])
    return jnp.concatenate([head, rolled[8:]], axis=0)


def _ffn_kernel(x_ref, g_ref, wup_ref, cw_ref, cb_ref, wdn_ref, gf_ref, o_ref, tail_ref, acc_ref,
                *, tiles_per_seq, chunk, final_norm):
    i = pl.program_id(0)
    tm = x_ref.shape[0]
    dff = wdn_ref.shape[0]

    @pl.when(i % tiles_per_seq == 0)
    def _():
        tail_ref[...] = jnp.zeros(tail_ref.shape, F32)

    x = x_ref[...]
    ms = jnp.mean(x * x, axis=-1, keepdims=True)
    h = (x * lax.rsqrt(ms + RMS_EPS) * g_ref[...]).astype(BF16)

    def up(c0, c1):
        return _dot(h, wup_ref[:, c0:c1]), _dot(h, wup_ref[:, dff + c0:dff + c1])

    def conv(u, c0, c1):
        tail = tail_ref[:, c0:c1]
        tail_ref[:, c0:c1] = u[tm - 8:, :]
        out = cb_ref[:, c0:c1] + cw_ref[0:1, c0:c1] * _shift_rows(u, tail, 2)
        out = out + cw_ref[1:2, c0:c1] * _shift_rows(u, tail, 1)
        return out + cw_ref[2:3, c0:c1] * u

    bounds = [(c0, min(c0 + chunk, dff)) for c0 in range(0, dff, chunk)]
    pre = up(*bounds[0])
    for n, (c0, c1) in enumerate(bounds):
        cur = pre
        if n + 1 < len(bounds):
            pre = up(*bounds[n + 1])
        ua = conv(cur[0], c0, c1)
        uv = conv(cur[1], dff + c0, dff + c1)
        act = (ua / (1.0 + jnp.exp(-ua)) * uv).astype(BF16)
        part = _dot(act, wdn_ref[c0:c1, :])
        if n == 0:
            acc_ref[...] = x + part
        else:
            acc_ref[...] += part

    y = acc_ref[...]
    if final_norm:
        ms = jnp.mean(y * y, axis=-1, keepdims=True)
        y = y * lax.rsqrt(ms + RMS_EPS) * gf_ref[...]
    o_ref[...] = y


def _ffn(x, gain, wup_layers, cw, cb, wdn_layers, layer, gain_final, *, seq, tm, chunk, final_norm):
    t, d = x.shape
    dff = wdn_layers.shape[1]
    assert seq % tm == 0
    fixed = lambda i: (0, 0)
    picked = lambda i: (layer, 0, 0)
    once = pl.Buffered(1)
    return pl.pallas_call(
        functools.partial(_ffn_kernel, tiles_per_seq=seq // tm, chunk=chunk, final_norm=final_norm),
        grid=(t // tm,),
        in_specs=[
            pl.BlockSpec((tm, d), lambda i: (i, 0)),
            pl.BlockSpec((1, d), fixed),
            pl.BlockSpec((None, d, 2 * dff), picked, pipeline_mode=once),
            pl.BlockSpec((3, 2 * dff), fixed),
            pl.BlockSpec((1, 2 * dff), fixed),
            pl.BlockSpec((None, dff, d), picked, pipeline_mode=once),
            pl.BlockSpec((1, d), fixed),
        ],
        out_specs=pl.BlockSpec((tm, d), lambda i: (i, 0)),
        out_shape=jax.ShapeDtypeStruct((t, d), F32),
        scratch_shapes=[pltpu.VMEM((8, 2 * dff), F32), pltpu.VMEM((tm, d), F32)],
        compiler_params=_params("arbitrary"),
    )(x, gain, wup_layers, cw, cb, wdn_layers, gain_final)


def _rope_tables(seq):
    inv = ROPE_THETA ** (-jnp.arange(0, HEAD_DIM, 2, dtype=F32) / HEAD_DIM)
    ang = jnp.arange(seq, dtype=F32)[:, None] * inv[None, :]
    cos, sin = jnp.cos(ang), jnp.sin(ang)
    reps = LANES // HEAD_DIM
    return jnp.tile(cos, (1, 2 * reps)), jnp.tile(jnp.concatenate([-sin, sin], axis=1), (1, reps))


def kernel(x, norm_mix, w_in, b_gate, sinks, w_branch, w_out, norm_ffn, w_up, conv_w, conv_b, w_down, norm_final):
    batch, seq, d = x.shape
    depth = w_in.shape[0]
    tokens = batch * seq
    cos, sin_signed = _rope_tables(seq)
    xt = x.reshape(tokens, d)
    tm_tok = min(512, seq)
    wup, wdn = w_up.astype(BF16), w_down.astype(BF16)
    for layer in range(depth):
        roped, plain, gates = _in_project(xt, norm_mix[layer].reshape(1, d), w_in, layer,
                                          cos, sin_signed, b_gate[layer].reshape(1, N_BRANCHES * d),
                                          seq=seq, tm=tm_tok, chunk=512)
        sink_rows = jnp.broadcast_to(sinks[layer].astype(F32)[:, None], (HEADS, LANES))
        o_a = _swa(roped, plain, sink_rows, batch=batch, seq=seq)
        o_b = _stick_breaking(plain, batch=batch, seq=seq)
        o_c = _moba(roped, plain, batch=batch, seq=seq)
        xt = _merge(o_a, o_b, o_c, gates, w_branch, w_out, layer, xt, tm=tm_tok)
        xt = _ffn(xt, norm_ffn[layer].reshape(1, d), wup, conv_w[layer],
                  conv_b[layer].reshape(1, -1), wdn, layer, norm_final.reshape(1, d),
                  seq=seq, tm=tm_tok, chunk=768, final_norm=(layer == depth - 1))
    return xt.reshape(batch, seq, d)
```

```python
import functools
import math

import jax
import jax.numpy as jnp
from jax import lax
from jax.experimental import pallas as pl
from jax.experimental.pallas import tpu as pltpu

F32 = jnp.float32
BF16 = jnp.bfloat16

HEAD_DIM = 64
LANES = 128
HEADS = 8
PAIRS = HEADS // 2
WIDTH = HEADS * HEAD_DIM
SWA_KV_HEADS = 2
SWA_WINDOW = 128
SWA_MAX_WINDOWS_PER_STEP = 8
SB_BLOCK = 256
SB_PAST = 128
SB_DONE = 88.0
MOBA_BLOCK = 256
MOBA_TOPK = 3
MOBA_MAX_BLOCKS = 8
N_BRANCHES = 3
ROPE_THETA = 10000.0
RMS_EPS = 1e-6
SCALE = HEAD_DIM ** -0.5
LOG2E = 1.4426950408889634
NEG = -1e30

VMEM_LIMIT = 56 * 1024 * 1024

ROPE_COLS = 3 * WIDTH + LANES
PLAIN_COLS = 4 * WIDTH + LANES
_KV = SWA_KV_HEADS * HEAD_DIM
_B0, _C0 = WIDTH + 2 * _KV, WIDTH + 2 * _KV + 3 * WIDTH
ROPE_SRC = ((0, WIDTH), (_C0, 2 * WIDTH), (WIDTH, _KV))
PLAIN_SRC = ((_B0, 3 * WIDTH), (_C0 + 2 * WIDTH, WIDTH), (WIDTH + _KV, _KV))
GATE_START = _C0 + 3 * WIDTH
QA, QC, KC = range(3)
KA = 3 * PAIRS
QB, KB, VB, VC = range(4)
VA = 4 * PAIRS


def _params(*sem):
    return pltpu.CompilerParams(dimension_semantics=sem, vmem_limit_bytes=VMEM_LIMIT)


def _dot(a, b):
    return jnp.dot(a, b, preferred_element_type=F32)


def _dot_nt(a, b):
    return lax.dot_general(a, b, (((1,), (1,)), ((), ())), preferred_element_type=F32)


def _lane():
    return lax.broadcasted_iota(jnp.int32, (1, LANES), 1)


def _rope(piece, cos, sin_signed):
    first_half = (_lane() & 32) == 0
    nxt = pltpu.roll(piece, LANES - 32, axis=1)
    prv = pltpu.roll(piece, 32, axis=1)
    return piece * cos + jnp.where(first_half, nxt, prv) * sin_signed


def _in_proj_kernel(x_ref, g_ref, w_ref, cos_ref, sin_ref, bias_ref, rope_ref, plain_ref, gate_ref, *, chunk):
    x = x_ref[...]
    ms = jnp.mean(x * x, axis=-1, keepdims=True)
    h = (x * lax.rsqrt(ms + RMS_EPS) * g_ref[...]).astype(BF16)
    cos, sin_signed = cos_ref[...], sin_ref[...]
    for out_ref, pieces in ((rope_ref, ROPE_SRC), (plain_ref, PLAIN_SRC), (gate_ref, ((GATE_START, gate_ref.shape[1]),))):
        dst = 0
        for src, width in pieces:
            for c0 in range(0, width, chunk):
                n = min(chunk, width - c0)
                acc = _dot(h, w_ref[:, src + c0:src + c0 + n].astype(BF16))
                lo = dst + c0
                if out_ref is rope_ref:
                    for c in range(0, n, LANES):
                        out_ref[:, lo + c:lo + c + LANES] = _rope(acc[:, c:c + LANES], cos, sin_signed).astype(out_ref.dtype)
                elif out_ref is gate_ref:
                    out_ref[:, lo:lo + n] = (1.0 / (1.0 + jnp.exp(-(acc + bias_ref[:, lo:lo + n])))).astype(out_ref.dtype)
                else:
                    out_ref[:, lo:lo + n] = acc.astype(out_ref.dtype)
            dst += width


def _in_project(x, gain, w_layers, layer, cos, sin_signed, gate_bias, *, seq, tm, chunk):
    t, d = x.shape
    n_in = w_layers.shape[2]
    n_gate = n_in - ROPE_COLS - PLAIN_COLS
    assert t % tm == 0 and seq % tm == 0
    pos_tiles = seq // tm
    row = lambda i: (i, 0)
    return pl.pallas_call(
        functools.partial(_in_proj_kernel, chunk=chunk),
        grid=(t // tm,),
        in_specs=[
            pl.BlockSpec((tm, d), row),
            pl.BlockSpec((1, d), lambda i: (0, 0)),
            pl.BlockSpec((None, d, n_in), lambda i: (layer, 0, 0), pipeline_mode=pl.Buffered(1)),
            pl.BlockSpec((tm, LANES), lambda i: (i % pos_tiles, 0)),
            pl.BlockSpec((tm, LANES), lambda i: (i % pos_tiles, 0)),
            pl.BlockSpec((1, n_gate), lambda i: (0, 0)),
        ],
        out_specs=[pl.BlockSpec((tm, ROPE_COLS), row), pl.BlockSpec((tm, PLAIN_COLS), row),
                   pl.BlockSpec((tm, n_gate), row)],
        out_shape=[jax.ShapeDtypeStruct((t, ROPE_COLS), BF16), jax.ShapeDtypeStruct((t, PLAIN_COLS), BF16),
                   jax.ShapeDtypeStruct((t, n_gate), BF16)],
        compiler_params=_params("arbitrary"),
    )(x, gain, w_layers, cos, sin_signed, gate_bias)


def _swa_kernel(q_ref, k_ref, v_ref, sink_ref, o_ref, kat_ref, vt_ref):
    n = pl.program_id(1)
    w = SWA_WINDOW
    windows = q_ref.shape[0] // w
    low = _lane() < HEAD_DIM
    chains = [(g, side) for g in range(SWA_KV_HEADS) for side in range(2)]

    @pl.when(n == 0)
    def _():
        k = k_ref[...].astype(F32)
        v = v_ref[...].astype(F32)
        k_sw = pltpu.roll(k, HEAD_DIM, axis=1)
        v_sw = pltpu.roll(v, HEAD_DIM, axis=1)
        k_at = [jnp.where(low, k, 0.0), jnp.where(low, 0.0, k_sw), jnp.where(low, k_sw, 0.0), jnp.where(low, 0.0, k)]
        v_at = [jnp.where(low, v, 1.0), jnp.where(low, 1.0, v_sw), jnp.where(low, v_sw, 1.0), jnp.where(low, 1.0, v)]
        for c in range(len(chains)):
            kat_ref[c] = k_at[c].astype(BF16)
            vt_ref[c] = v_at[c].T.astype(BF16)

    key_i = lax.broadcasted_iota(jnp.int32, (2 * w, 2 * w), 0)
    query_i = lax.broadcasted_iota(jnp.int32, (2 * w, 2 * w), 1) & (w - 1)
    sinks = []
    for g, side in chains:
        h0 = 4 * g + side
        sinks.append(jnp.concatenate([sink_ref[h0:h0 + 1, :], sink_ref[h0 + 2:h0 + 3, :]], axis=1))
    starts, scores = [], []
    for win in range(windows):
        blk = n * windows + win
        start = pl.multiple_of(jnp.maximum(blk - 1, 0) * w, w)
        diff = (blk * w + query_i) - (start + key_i)
        valid = (diff >= 0) & (diff < w)
        rows = slice(win * w, (win + 1) * w)
        qs = [jnp.concatenate([q_ref[rows, (2 * g) * LANES:(2 * g + 1) * LANES],
                               q_ref[rows, (2 * g + 1) * LANES:(2 * g + 2) * LANES]], axis=0) * SCALE
              for g in range(SWA_KV_HEADS)]
        starts.append(start)
        scores.append([jnp.where(valid, _dot_nt(kat_ref[c, pl.ds(start, 2 * w), :], qs[g]), NEG)
                       for c, (g, side) in enumerate(chains)])
    maxes = [[jnp.maximum(jnp.max(s, axis=0, keepdims=True), sink) for s, sink in zip(win_scores, sinks)]
             for win_scores in scores]
    probs = [[jnp.exp(s - m).astype(BF16) for s, m in zip(win_scores, win_maxes)]
             for win_scores, win_maxes in zip(scores, maxes)]
    outs = [[_dot(vt_ref[c, :, pl.ds(start, 2 * w)], p) for c, p in enumerate(win_probs)]
            for start, win_probs in zip(starts, probs)]
    for win in range(windows):
        for g in range(SWA_KV_HEADS):
            o_lo, o_hi = outs[win][2 * g], outs[win][2 * g + 1]
            d_lo = o_lo[HEAD_DIM:] + jnp.exp(sinks[2 * g] - maxes[win][2 * g])
            d_hi = o_hi[:HEAD_DIM] + jnp.exp(sinks[2 * g + 1] - maxes[win][2 * g + 1])
            o = jnp.concatenate([o_lo[:HEAD_DIM] / d_lo, o_hi[HEAD_DIM:] / d_hi], axis=0).T.astype(o_ref.dtype)
            o_ref[win * w:(win + 1) * w, (2 * g) * LANES:(2 * g + 1) * LANES] = o[:w]
            o_ref[win * w:(win + 1) * w, (2 * g + 1) * LANES:(2 * g + 2) * LANES] = o[w:]


def _swa(roped, plain, sink_rows, *, batch, seq):
    tq = SWA_WINDOW * math.gcd(seq // SWA_WINDOW, SWA_MAX_WINDOWS_PER_STEP)
    steps = seq // tq
    return pl.pallas_call(
        _swa_kernel,
        grid=(batch, steps),
        in_specs=[
            pl.BlockSpec((tq, WIDTH), lambda b, n: (b * steps + n, QA)),
            pl.BlockSpec((seq, LANES), lambda b, n: (b, KA)),
            pl.BlockSpec((seq, LANES), lambda b, n: (b, VA)),
            pl.BlockSpec((HEADS, LANES), lambda b, n: (0, 0)),
        ],
        out_specs=pl.BlockSpec((tq, WIDTH), lambda b, n: (b * steps + n, 0)),
        out_shape=jax.ShapeDtypeStruct((batch * seq, WIDTH), BF16),
        scratch_shapes=[pltpu.VMEM((2 * SWA_KV_HEADS, seq, LANES), BF16), pltpu.VMEM((2 * SWA_KV_HEADS, LANES, seq), BF16)],
        compiler_params=_params("arbitrary", "arbitrary"),
    )(roped, roped, plain, sink_rows)


def _softplus(z):
    return jnp.maximum(z, 0.0) + jnp.log(1.0 + jnp.exp2(jnp.abs(z) * -LOG2E))


def _sb_kernel(q_ref, k_ref, v_ref, o_ref, kmask_ref, vt_ref, acc_ref):
    i = pl.program_id(1)
    t = SB_BLOCK
    nblocks = k_ref.shape[0] // t
    low = _lane() < HEAD_DIM
    mine = [low, jnp.logical_not(low)]
    chains = [(p, side) for p in range(PAIRS) for side in range(2)]

    @pl.when(i == 0)
    def _():
        for j in range(nblocks):
            rows = slice(j * t, (j + 1) * t)
            for side in range(2):
                kmask_ref[side, rows, :] = jnp.where(jnp.tile(mine[side], (1, PAIRS)), k_ref[rows, :], jnp.zeros((1, WIDTH), BF16))
                for p in range(PAIRS):
                    v = jnp.where(mine[side], v_ref[rows, p * LANES:(p + 1) * LANES].astype(F32), 0.0)
                    vt_ref[side, p * LANES:(p + 1) * LANES, rows] = v.T.astype(BF16)

    key = lax.broadcasted_iota(jnp.int32, (t, t), 0)
    query = lax.broadcasted_iota(jnp.int32, (t, t), 1)
    suffix = jnp.where(query >= key, 1.0, 0.0).astype(BF16)
    past = key < query
    qs = [q_ref[:, p * LANES:(p + 1) * LANES] * SCALE for p in range(PAIRS)]

    def block(off, size, remains, diagonal):
        zs = [_dot_nt(kmask_ref[side, pl.ds(off, size), p * LANES:(p + 1) * LANES], qs[p]) for p, side in chains]
        sps = []
        for z in zs:
            sp = _softplus(z)
            sps.append(jnp.where(past, sp, 0.0) if diagonal else sp)
        cs = [_dot(suffix[:size, :size], sp.astype(BF16)) for sp in sps]
        wgts = []
        for n, (z, c) in enumerate(zip(zs, cs)):
            logw = z - c - remains[n]
            if diagonal:
                logw = jnp.where(past, logw, NEG)
            wgts.append(jnp.exp(logw).astype(BF16))
        pvs = [_dot(vt_ref[side, p * LANES:(p + 1) * LANES, pl.ds(off, size)], wgts[n]) for n, (p, side) in enumerate(chains)]
        for p in range(PAIRS):
            both = pvs[2 * p] + pvs[2 * p + 1]
            acc_ref[p] = both if diagonal else acc_ref[p] + both
        return tuple(remains[n] + cs[n][0:1, :] for n in range(len(chains)))

    def smallest(remains):
        return jnp.min(functools.reduce(jnp.minimum, remains))

    remains = block(pl.multiple_of(i * t, t), t, tuple(jnp.zeros((1, t), F32) for _ in chains), True)

    runs = i * (t // SB_PAST)

    def cond(state):
        s, floor, _ = state
        return (s < runs) & (floor < SB_DONE)

    def body(state):
        s, _, remains = state
        remains = block(pl.multiple_of((runs - 1 - s) * SB_PAST, SB_PAST), SB_PAST, remains, False)
        return s + 1, smallest(remains), remains

    lax.while_loop(cond, body, (0, smallest(remains), remains))
    for p in range(PAIRS):
        o_ref[:, p * LANES:(p + 1) * LANES] = acc_ref[p].T.astype(o_ref.dtype)


def _stick_breaking(plain, *, batch, seq):
    t = SB_BLOCK
    nq = seq // t
    return pl.pallas_call(
        _sb_kernel,
        grid=(batch, nq),
        in_specs=[
            pl.BlockSpec((t, WIDTH), lambda b, i: (b * nq + i, QB)),
            pl.BlockSpec((seq, WIDTH), lambda b, i: (b, KB)),
            pl.BlockSpec((seq, WIDTH), lambda b, i: (b, VB)),
        ],
        out_specs=pl.BlockSpec((t, WIDTH), lambda b, i: (b * nq + i, 0)),
        out_shape=jax.ShapeDtypeStruct((batch * seq, WIDTH), BF16),
        scratch_shapes=[pltpu.VMEM((2, seq, WIDTH), BF16), pltpu.VMEM((2, WIDTH, seq), BF16),
                        pltpu.VMEM((PAIRS, LANES, t), F32)],
        compiler_params=_params("arbitrary", "arbitrary"),
    )(plain, plain, plain)


def _moba_kernel(q_ref, k_ref, v_ref, o_ref, kmean_ref, kmask_ref, vaug_ref, acc_ref, bias_ref, prob_ref, *, nblocks):
    i = pl.program_id(1)
    t = MOBA_BLOCK
    nbp = MOBA_MAX_BLOCKS
    low = _lane() < HEAD_DIM
    mine = [low, jnp.logical_not(low)]
    chains = [(p, side) for p in range(PAIRS) for side in range(2)]

    @pl.when(i == 0)
    def _():
        for p in range(PAIRS):
            rows = [jnp.mean(k_ref[j * t:(j + 1) * t, p * LANES:(p + 1) * LANES].astype(F32), axis=0, keepdims=True)
                    for j in range(nblocks)]
            rows += [jnp.zeros((1, LANES), F32)] * (nbp - nblocks)
            km = jnp.concatenate(rows, axis=0)
            kmean_ref[2 * p * nbp:(2 * p + 1) * nbp, :] = jnp.where(low, km, 0.0)
            kmean_ref[(2 * p + 1) * nbp:(2 * p + 2) * nbp, :] = jnp.where(low, 0.0, km)
        for j in range(nblocks):
            rows = slice(j * t, (j + 1) * t)
            for side in range(2):
                kmask_ref[side, rows, :] = jnp.where(jnp.tile(mine[side], (1, PAIRS)), k_ref[rows, :], jnp.zeros((1, WIDTH), BF16))
                for p in range(PAIRS):
                    v = jnp.where(mine[side], v_ref[rows, p * LANES:(p + 1) * LANES].astype(F32), 1.0)
                    vaug_ref[side, p * LANES:(p + 1) * LANES, rows] = v.T.astype(BF16)

    gates = []
    for p in range(PAIRS):
        q = q_ref[:, p * LANES:(p + 1) * LANES]
        km = kmean_ref[2 * p * nbp:(2 * p + 2) * nbp, :]
        km_hi = km.astype(BF16)
        km_lo = (km - km_hi.astype(F32)).astype(BF16)
        gates.append((_dot_nt(km_hi, q) + _dot_nt(km_lo, q)).reshape(2, nbp, t))
    blk = lax.broadcasted_iota(jnp.int32, (HEADS, nbp, t), 1)
    gate = jnp.where(blk < i, jnp.concatenate(gates, axis=0), NEG)
    count = jnp.zeros((HEADS, nbp, t), F32)
    for other in range(nblocks):
        g_other = gate[:, other:other + 1, :]
        beats = (g_other > gate) | ((g_other == gate) & (other < blk))
        count = count + jnp.where(beats, 1.0, 0.0)
    bias_ref[...] = jnp.where((count < min(MOBA_TOPK, nblocks)) & (blk < i), 0.0, NEG)
    qs = [q_ref[:, p * LANES:(p + 1) * LANES] * SCALE for p in range(PAIRS)]

    key = lax.broadcasted_iota(jnp.int32, (t, t), 0)
    query = lax.broadcasted_iota(jnp.int32, (t, t), 1)
    causal = key <= query

    def score(j):
        off = pl.multiple_of(j * t, t)
        return [_dot_nt(kmask_ref[side, pl.ds(off, t), p * LANES:(p + 1) * LANES], qs[p]) for p, side in chains]

    def weigh(scores, shift, maxes):
        new_max = [jnp.maximum(m, jnp.max(s, axis=0, keepdims=True) + b) for m, s, b in zip(maxes, scores, shift)]
        for n, (s, m, b) in enumerate(zip(scores, new_max, shift)):
            prob_ref[n] = jnp.exp(s - (m - b)).astype(BF16)
        return tuple(new_max), tuple(jnp.exp(mo - mn) for mo, mn in zip(maxes, new_max))

    def absorb(j, decay):
        off = pl.multiple_of(j * t, t)
        pvs = [_dot(vaug_ref[side, p * LANES:(p + 1) * LANES, pl.ds(off, t)], prob_ref[n]) for n, (p, side) in enumerate(chains)]
        for n, pv in enumerate(pvs):
            acc_ref[n] = acc_ref[n] * decay[n] + pv

    def step(j, carry):
        prev, maxes, decay = carry
        scores = score(j)
        absorb(prev, decay)
        maxes, decay = weigh(scores, [bias_ref[n, pl.ds(j, 1), :] for n in range(len(chains))], maxes)
        return j, maxes, decay

    acc_ref[...] = jnp.zeros(acc_ref.shape, F32)
    own = [jnp.where(causal, s, NEG) for s in score(i)]
    start = tuple(jnp.full((1, t), NEG, F32) for _ in chains)
    maxes, decay = weigh(own, [jnp.zeros((1, t), F32)] * len(chains), start)
    last, _, decay = lax.fori_loop(0, i, step, (i, maxes, decay))
    absorb(last, decay)
    accs = [acc_ref[n] for n in range(len(chains))]
    for p in range(PAIRS):
        a_lo, a_hi = accs[2 * p], accs[2 * p + 1]
        o_t = jnp.concatenate([a_lo[:HEAD_DIM] / a_lo[HEAD_DIM:], a_hi[HEAD_DIM:] / a_hi[:HEAD_DIM]], axis=0)
        o_ref[:, p * LANES:(p + 1) * LANES] = o_t.T.astype(o_ref.dtype)


def _moba(roped, plain, *, batch, seq):
    t = MOBA_BLOCK
    nq = seq // t
    assert seq % t == 0 and nq <= MOBA_MAX_BLOCKS
    return pl.pallas_call(
        functools.partial(_moba_kernel, nblocks=nq),
        grid=(batch, nq),
        in_specs=[
            pl.BlockSpec((t, WIDTH), lambda b, i: (b * nq + i, QC)),
            pl.BlockSpec((seq, WIDTH), lambda b, i: (b, KC)),
            pl.BlockSpec((seq, WIDTH), lambda b, i: (b, VC)),
        ],
        out_specs=pl.BlockSpec((t, WIDTH), lambda b, i: (b * nq + i, 0)),
        out_shape=jax.ShapeDtypeStruct((batch * seq, WIDTH), BF16),
        scratch_shapes=[pltpu.VMEM((HEADS * MOBA_MAX_BLOCKS, LANES), F32), pltpu.VMEM((2, seq, WIDTH), BF16),
                        pltpu.VMEM((2, WIDTH, seq), BF16), pltpu.VMEM((HEADS, LANES, t), F32),
                        pltpu.VMEM((HEADS, MOBA_MAX_BLOCKS, t), F32), pltpu.VMEM((HEADS, t, t), BF16)],
        compiler_params=_params("arbitrary", "arbitrary"),
    )(roped, roped, plain)


def _merge_kernel(oa_ref, ob_ref, oc_ref, gate_ref, wb_ref, wo_ref, x_ref, o_ref):
    d = x_ref.shape[1]
    ys = [_dot(branch[...], wb_ref[n].astype(BF16)) for n, branch in enumerate((oa_ref, ob_ref, oc_ref))]
    merged = None
    for n, y in enumerate(ys):
        term = y * gate_ref[:, n * d:(n + 1) * d].astype(F32)
        merged = term if merged is None else merged + term
    o_ref[...] = x_ref[...] + _dot(merged.astype(BF16), wo_ref[...].astype(BF16))


def _merge(oa, ob, oc, gates, wb_layers, wo_layers, layer, x, *, tm):
    t, d = x.shape
    bw = oa.shape[1]
    row = lambda i: (i, 0)
    return pl.pallas_call(
        _merge_kernel,
        grid=(t // tm,),
        in_specs=[
            pl.BlockSpec((tm, bw), row),
            pl.BlockSpec((tm, bw), row),
            pl.BlockSpec((tm, bw), row),
            pl.BlockSpec((tm, N_BRANCHES * d), row),
            pl.BlockSpec((None, N_BRANCHES, bw, d), lambda i: (layer, 0, 0, 0), pipeline_mode=pl.Buffered(1)),
            pl.BlockSpec((None, d, d), lambda i: (layer, 0, 0), pipeline_mode=pl.Buffered(1)),
            pl.BlockSpec((tm, d), row),
        ],
        out_specs=pl.BlockSpec((tm, d), row),
        out_shape=jax.ShapeDtypeStruct((t, d), F32),
        compiler_params=_params("arbitrary"),
    )(oa, ob, oc, gates, wb_layers, wo_layers, x)


def _shift_rows(u, tail, shift):
    rolled = pltpu.roll(u, shift, axis=0)
    head = jnp.where(lax.broadcasted_iota(jnp.int32, tail.shape, 0) < shift,
                     pltpu.roll(tail, shift, axis=0), rolled[:8])
    return jnp.concatenate([head, rolled[8:]], axis=0)


def _ffn_kernel(x_ref, g_ref, wup_ref, cw_ref, cb_ref, wdn_ref, gf_ref, o_ref, tail_ref, acc_ref,
                *, tiles_per_seq, chunk, final_norm):
    i = pl.program_id(0)
    tm = x_ref.shape[0]
    dff = wdn_ref.shape[0]

    @pl.when(i % tiles_per_seq == 0)
    def _():
        tail_ref[...] = jnp.zeros(tail_ref.shape, F32)

    x = x_ref[...]
    ms = jnp.mean(x * x, axis=-1, keepdims=True)
    h = (x * lax.rsqrt(ms + RMS_EPS) * g_ref[...]).astype(BF16)

    def up(c0, c1):
        return _dot(h, wup_ref[:, c0:c1]), _dot(h, wup_ref[:, dff + c0:dff + c1])

    def conv(u, c0, c1):
        tail = tail_ref[:, c0:c1]
        tail_ref[:, c0:c1] = u[tm - 8:, :]
        out = cb_ref[:, c0:c1] + cw_ref[0:1, c0:c1] * _shift_rows(u, tail, 2)
        out = out + cw_ref[1:2, c0:c1] * _shift_rows(u, tail, 1)
        return out + cw_ref[2:3, c0:c1] * u

    bounds = [(c0, min(c0 + chunk, dff)) for c0 in range(0, dff, chunk)]
    pre = up(*bounds[0])
    for n, (c0, c1) in enumerate(bounds):
        cur = pre
        if n + 1 < len(bounds):
            pre = up(*bounds[n + 1])
        ua = conv(cur[0], c0, c1)
        uv = conv(cur[1], dff + c0, dff + c1)
        act = (ua / (1.0 + jnp.exp(-ua)) * uv).astype(BF16)
        part = _dot(act, wdn_ref[c0:c1, :])
        if n == 0:
            acc_ref[...] = x + part
        else:
            acc_ref[...] += part

    y = acc_ref[...]
    if final_norm:
        ms = jnp.mean(y * y, axis=-1, keepdims=True)
        y = y * lax.rsqrt(ms + RMS_EPS) * gf_ref[...]
    o_ref[...] = y


def _ffn(x, gain, wup_layers, cw, cb, wdn_layers, layer, gain_final, *, seq, tm, chunk, final_norm):
    t, d = x.shape
    dff = wdn_layers.shape[1]
    assert seq % tm == 0
    fixed = lambda i: (0, 0)
    picked = lambda i: (layer, 0, 0)
    once = pl.Buffered(1)
    return pl.pallas_call(
        functools.partial(_ffn_kernel, tiles_per_seq=seq // tm, chunk=chunk, final_norm=final_norm),
        grid=(t // tm,),
        in_specs=[
            pl.BlockSpec((tm, d), lambda i: (i, 0)),
            pl.BlockSpec((1, d), fixed),
            pl.BlockSpec((None, d, 2 * dff), picked, pipeline_mode=once),
            pl.BlockSpec((3, 2 * dff), fixed),
            pl.BlockSpec((1, 2 * dff), fixed),
            pl.BlockSpec((None, dff, d), picked, pipeline_mode=once),
            pl.BlockSpec((1, d), fixed),
        ],
        out_specs=pl.BlockSpec((tm, d), lambda i: (i, 0)),
        out_shape=jax.ShapeDtypeStruct((t, d), F32),
        scratch_shapes=[pltpu.VMEM((8, 2 * dff), F32), pltpu.VMEM((tm, d), F32)],
        compiler_params=_params("arbitrary"),
    )(x, gain, wup_layers, cw, cb, wdn_layers, gain_final)


def _rope_tables(seq):
    inv = ROPE_THETA ** (-jnp.arange(0, HEAD_DIM, 2, dtype=F32) / HEAD_DIM)
    ang = jnp.arange(seq, dtype=F32)[:, None] * inv[None, :]
    cos, sin = jnp.cos(ang), jnp.sin(ang)
    reps = LANES // HEAD_DIM
    return jnp.tile(cos, (1, 2 * reps)), jnp.tile(jnp.concatenate([-sin, sin], axis=1), (1, reps))


def kernel(x, norm_mix, w_in, b_gate, sinks, w_branch, w_out, norm_ffn, w_up, conv_w, conv_b, w_down, norm_final):
    batch, seq, d = x.shape
    depth = w_in.shape[0]
    tokens = batch * seq
    cos, sin_signed = _rope_tables(seq)
    xt = x.reshape(tokens, d)
    tm_tok = min(512, seq)
    wup, wdn = w_up.astype(BF16), w_down.astype(BF16)
    for layer in range(depth):
        roped, plain, gates = _in_project(xt, norm_mix[layer].reshape(1, d), w_in, layer,
                                          cos, sin_signed, b_gate[layer].reshape(1, N_BRANCHES * d),
                                          seq=seq, tm=tm_tok, chunk=512)
        sink_rows = jnp.broadcast_to(sinks[layer].astype(F32)[:, None], (HEADS, LANES))
        o_a = _swa(roped, plain, sink_rows, batch=batch, seq=seq)
        o_b = _stick_breaking(plain, batch=batch, seq=seq)
        o_c = _moba(roped, plain, batch=batch, seq=seq)
        xt = _merge(o_a, o_b, o_c, gates, w_branch, w_out, layer, xt, tm=tm_tok)
        xt = _ffn(xt, norm_ffn[layer].reshape(1, d), wup, conv_w[layer],
                  conv_b[layer].reshape(1, -1), wdn, layer, norm_final.reshape(1, d),
                  seq=seq, tm=tm_tok, chunk=768, final_norm=(layer == depth - 1))
    return xt.reshape(batch, seq, d)
```

```python
import functools
import math

import jax
import jax.numpy as jnp
from jax import lax
from jax.experimental import pallas as pl
from jax.experimental.pallas import tpu as pltpu

F32 = jnp.float32
BF16 = jnp.bfloat16

HEAD_DIM = 64
LANES = 128
HEADS = 8
PAIRS = HEADS // 2
WIDTH = HEADS * HEAD_DIM
SWA_KV_HEADS = 2
SWA_WINDOW = 128
SWA_MAX_WINDOWS_PER_STEP = 8
SB_BLOCK = 256
SB_PAST = 128
SB_DONE = 88.0
MOBA_BLOCK = 256
MOBA_TOPK = 3
MOBA_MAX_BLOCKS = 8
N_BRANCHES = 3
ROPE_THETA = 10000.0
RMS_EPS = 1e-6
SCALE = HEAD_DIM ** -0.5
LOG2E = 1.4426950408889634
NEG = -1e30

VMEM_LIMIT = 56 * 1024 * 1024

ROPE_COLS = 3 * WIDTH + LANES
PLAIN_COLS = 4 * WIDTH + LANES
_KV = SWA_KV_HEADS * HEAD_DIM
_B0, _C0 = WIDTH + 2 * _KV, WIDTH + 2 * _KV + 3 * WIDTH
ROPE_SRC = ((0, WIDTH), (_C0, 2 * WIDTH), (WIDTH, _KV))
PLAIN_SRC = ((_B0, 3 * WIDTH), (_C0 + 2 * WIDTH, WIDTH), (WIDTH + _KV, _KV))
GATE_START = _C0 + 3 * WIDTH
QA, QC, KC = range(3)
KA = 3 * PAIRS
QB, KB, VB, VC = range(4)
VA = 4 * PAIRS


def _params(*sem):
    return pltpu.CompilerParams(dimension_semantics=sem, vmem_limit_bytes=VMEM_LIMIT)


def _dot(a, b):
    return jnp.dot(a, b, preferred_element_type=F32)


def _dot_nt(a, b):
    return lax.dot_general(a, b, (((1,), (1,)), ((), ())), preferred_element_type=F32)


def _lane():
    return lax.broadcasted_iota(jnp.int32, (1, LANES), 1)


def _rope(piece, cos, sin_signed):
    first_half = (_lane() & 32) == 0
    nxt = pltpu.roll(piece, LANES - 32, axis=1)
    prv = pltpu.roll(piece, 32, axis=1)
    return piece * cos + jnp.where(first_half, nxt, prv) * sin_signed


def _in_proj_kernel(x_ref, g_ref, w_ref, cos_ref, sin_ref, bias_ref, rope_ref, plain_ref, gate_ref, *, chunk):
    x = x_ref[...]
    ms = jnp.mean(x * x, axis=-1, keepdims=True)
    h = (x * lax.rsqrt(ms + RMS_EPS) * g_ref[...]).astype(BF16)
    cos, sin_signed = cos_ref[...], sin_ref[...]
    for out_ref, pieces in ((rope_ref, ROPE_SRC), (plain_ref, PLAIN_SRC), (gate_ref, ((GATE_START, gate_ref.shape[1]),))):
        dst = 0
        for src, width in pieces:
            for c0 in range(0, width, chunk):
                n = min(chunk, width - c0)
                acc = _dot(h, w_ref[:, src + c0:src + c0 + n].astype(BF16))
                lo = dst + c0
                if out_ref is rope_ref:
                    for c in range(0, n, LANES):
                        out_ref[:, lo + c:lo + c + LANES] = _rope(acc[:, c:c + LANES], cos, sin_signed).astype(out_ref.dtype)
                elif out_ref is gate_ref:
                    out_ref[:, lo:lo + n] = (1.0 / (1.0 + jnp.exp(-(acc + bias_ref[:, lo:lo + n])))).astype(out_ref.dtype)
                else:
                    out_ref[:, lo:lo + n] = acc.astype(out_ref.dtype)
            dst += width


def _in_project(x, gain, w_layers, layer, cos, sin_signed, gate_bias, *, seq, tm, chunk):
    t, d = x.shape
    n_in = w_layers.shape[2]
    n_gate = n_in - ROPE_COLS - PLAIN_COLS
    assert t % tm == 0 and seq % tm == 0
    pos_tiles = seq // tm
    row = lambda i: (i, 0)
    return pl.pallas_call(
        functools.partial(_in_proj_kernel, chunk=chunk),
        grid=(t // tm,),
        in_specs=[
            pl.BlockSpec((tm, d), row),
            pl.BlockSpec((1, d), lambda i: (0, 0)),
            pl.BlockSpec((None, d, n_in), lambda i: (layer, 0, 0), pipeline_mode=pl.Buffered(1)),
            pl.BlockSpec((tm, LANES), lambda i: (i % pos_tiles, 0)),
            pl.BlockSpec((tm, LANES), lambda i: (i % pos_tiles, 0)),
            pl.BlockSpec((1, n_gate), lambda i: (0, 0)),
        ],
        out_specs=[pl.BlockSpec((tm, ROPE_COLS), row), pl.BlockSpec((tm, PLAIN_COLS), row),
                   pl.BlockSpec((tm, n_gate), row)],
        out_shape=[jax.ShapeDtypeStruct((t, ROPE_COLS), BF16), jax.ShapeDtypeStruct((t, PLAIN_COLS), BF16),
                   jax.ShapeDtypeStruct((t, n_gate), BF16)],
        compiler_params=_params("arbitrary"),
    )(x, gain, w_layers, cos, sin_signed, gate_bias)


def _swa_kernel(q_ref, k_ref, v_ref, sink_ref, o_ref, kat_ref, vt_ref):
    n = pl.program_id(1)
    w = SWA_WINDOW
    windows = q_ref.shape[0] // w
    low = _lane() < HEAD_DIM
    chains = [(g, side) for g in range(SWA_KV_HEADS) for side in range(2)]

    @pl.when(n == 0)
    def _():
        k = k_ref[...].astype(F32)
        v = v_ref[...].astype(F32)
        k_sw = pltpu.roll(k, HEAD_DIM, axis=1)
        v_sw = pltpu.roll(v, HEAD_DIM, axis=1)
        k_at = [jnp.where(low, k, 0.0), jnp.where(low, 0.0, k_sw), jnp.where(low, k_sw, 0.0), jnp.where(low, 0.0, k)]
        v_at = [jnp.where(low, v, 1.0), jnp.where(low, 1.0, v_sw), jnp.where(low, v_sw, 1.0), jnp.where(low, 1.0, v)]
        for c in range(len(chains)):
            kat_ref[c] = k_at[c].astype(BF16)
            vt_ref[c] = v_at[c].T.astype(BF16)

    key_i = lax.broadcasted_iota(jnp.int32, (2 * w, 2 * w), 0)
    query_i = lax.broadcasted_iota(jnp.int32, (2 * w, 2 * w), 1) & (w - 1)
    sinks = []
    for g, side in chains:
        h0 = 4 * g + side
        sinks.append(jnp.concatenate([sink_ref[h0:h0 + 1, :], sink_ref[h0 + 2:h0 + 3, :]], axis=1))
    starts, scores = [], []
    for win in range(windows):
        blk = n * windows + win
        start = pl.multiple_of(jnp.maximum(blk - 1, 0) * w, w)
        diff = (blk * w + query_i) - (start + key_i)
        valid = (diff >= 0) & (diff < w)
        rows = slice(win * w, (win + 1) * w)
        qs = [jnp.concatenate([q_ref[rows, (2 * g) * LANES:(2 * g + 1) * LANES],
                               q_ref[rows, (2 * g + 1) * LANES:(2 * g + 2) * LANES]], axis=0) * SCALE
              for g in range(SWA_KV_HEADS)]
        starts.append(start)
        scores.append([jnp.where(valid, _dot_nt(kat_ref[c, pl.ds(start, 2 * w), :], qs[g]), NEG)
                       for c, (g, side) in enumerate(chains)])
    maxes = [[jnp.maximum(jnp.max(s, axis=0, keepdims=True), sink) for s, sink in zip(win_scores, sinks)]
             for win_scores in scores]
    probs = [[jnp.exp(s - m).astype(BF16) for s, m in zip(win_scores, win_maxes)]
             for win_scores, win_maxes in zip(scores, maxes)]
    outs = [[_dot(vt_ref[c, :, pl.ds(start, 2 * w)], p) for c, p in enumerate(win_probs)]
            for start, win_probs in zip(starts, probs)]
    for win in range(windows):
        for g in range(SWA_KV_HEADS):
            o_lo, o_hi = outs[win][2 * g], outs[win][2 * g + 1]
            d_lo = o_lo[HEAD_DIM:] + jnp.exp(sinks[2 * g] - maxes[win][2 * g])
            d_hi = o_hi[:HEAD_DIM] + jnp.exp(sinks[2 * g + 1] - maxes[win][2 * g + 1])
            o = jnp.concatenate([o_lo[:HEAD_DIM] / d_lo, o_hi[HEAD_DIM:] / d_hi], axis=0).T.astype(o_ref.dtype)
            o_ref[win * w:(win + 1) * w, (2 * g) * LANES:(2 * g + 1) * LANES] = o[:w]
            o_ref[win * w:(win + 1) * w, (2 * g + 1) * LANES:(2 * g + 2) * LANES] = o[w:]


def _swa(roped, plain, sink_rows, *, batch, seq):
    tq = SWA_WINDOW * math.gcd(seq // SWA_WINDOW, SWA_MAX_WINDOWS_PER_STEP)
    steps = seq // tq
    return pl.pallas_call(
        _swa_kernel,
        grid=(batch, steps),
        in_specs=[
            pl.BlockSpec((tq, WIDTH), lambda b, n: (b * steps + n, QA)),
            pl.BlockSpec((seq, LANES), lambda b, n: (b, KA)),
            pl.BlockSpec((seq, LANES), lambda b, n: (b, VA)),
            pl.BlockSpec((HEADS, LANES), lambda b, n: (0, 0)),
        ],
        out_specs=pl.BlockSpec((tq, WIDTH), lambda b, n: (b * steps + n, 0)),
        out_shape=jax.ShapeDtypeStruct((batch * seq, WIDTH), BF16),
        scratch_shapes=[pltpu.VMEM((2 * SWA_KV_HEADS, seq, LANES), BF16), pltpu.VMEM((2 * SWA_KV_HEADS, LANES, seq), BF16)],
        compiler_params=_params("arbitrary", "arbitrary"),
    )(roped, roped, plain, sink_rows)


def _softplus(z):
    return jnp.maximum(z, 0.0) + jnp.log(1.0 + jnp.exp2(jnp.abs(z) * -LOG2E))


def _sb_kernel(q_ref, k_ref, v_ref, o_ref, kmask_ref, vt_ref, acc_ref):
    i = pl.program_id(1)
    t = SB_BLOCK
    nblocks = k_ref.shape[0] // t
    low = _lane() < HEAD_DIM
    mine = [low, jnp.logical_not(low)]
    chains = [(p, side) for p in range(PAIRS) for side in range(2)]

    @pl.when(i == 0)
    def _():
        for j in range(nblocks):
            rows = slice(j * t, (j + 1) * t)
            for side in range(2):
                kmask_ref[side, rows, :] = jnp.where(jnp.tile(mine[side], (1, PAIRS)), k_ref[rows, :], jnp.zeros((1, WIDTH), BF16))
                for p in range(PAIRS):
                    v = jnp.where(mine[side], v_ref[rows, p * LANES:(p + 1) * LANES].astype(F32), 0.0)
                    vt_ref[side, p * LANES:(p + 1) * LANES, rows] = v.T.astype(BF16)

    key = lax.broadcasted_iota(jnp.int32, (t, t), 0)
    query = lax.broadcasted_iota(jnp.int32, (t, t), 1)
    suffix = jnp.where(query >= key, 1.0, 0.0).astype(BF16)
    past = key < query
    qs = [q_ref[:, p * LANES:(p + 1) * LANES] * SCALE for p in range(PAIRS)]

    def block(off, size, remains, diagonal):
        zs = [_dot_nt(kmask_ref[side, pl.ds(off, size), p * LANES:(p + 1) * LANES], qs[p]) for p, side in chains]
        sps = []
        for z in zs:
            sp = _softplus(z)
            sps.append(jnp.where(past, sp, 0.0) if diagonal else sp)
        cs = [_dot(suffix[:size, :size], sp.astype(BF16)) for sp in sps]
        wgts = []
        for n, (z, c) in enumerate(zip(zs, cs)):
            logw = z - c - remains[n]
            if diagonal:
                logw = jnp.where(past, logw, NEG)
            wgts.append(jnp.exp(logw).astype(BF16))
        pvs = [_dot(vt_ref[side, p * LANES:(p + 1) * LANES, pl.ds(off, size)], wgts[n]) for n, (p, side) in enumerate(chains)]
        for p in range(PAIRS):
            both = pvs[2 * p] + pvs[2 * p + 1]
            acc_ref[p] = both if diagonal else acc_ref[p] + both
        return tuple(remains[n] + cs[n][0:1, :] for n in range(len(chains)))

    def smallest(remains):
        return jnp.min(functools.reduce(jnp.minimum, remains))

    remains = block(pl.multiple_of(i * t, t), t, tuple(jnp.zeros((1, t), F32) for _ in chains), True)

    runs = i * (t // SB_PAST)

    def cond(state):
        s, floor, _ = state
        return (s < runs) & (floor < SB_DONE)

    def body(state):
        s, _, remains = state
        remains = block(pl.multiple_of((runs - 1 - s) * SB_PAST, SB_PAST), SB_PAST, remains, False)
        return s + 1, smallest(remains), remains

    lax.while_loop(cond, body, (0, smallest(remains), remains))
    for p in range(PAIRS):
        o_ref[:, p * LANES:(p + 1) * LANES] = acc_ref[p].T.astype(o_ref.dtype)


def _stick_breaking(plain, *, batch, seq):
    t = SB_BLOCK
    nq = seq // t
    return pl.pallas_call(
        _sb_kernel,
        grid=(batch, nq),
        in_specs=[
            pl.BlockSpec((t, WIDTH), lambda b, i: (b * nq + i, QB)),
            pl.BlockSpec((seq, WIDTH), lambda b, i: (b, KB)),
            pl.BlockSpec((seq, WIDTH), lambda b, i: (b, VB)),
        ],
        out_specs=pl.BlockSpec((t, WIDTH), lambda b, i: (b * nq + i, 0)),
        out_shape=jax.ShapeDtypeStruct((batch * seq, WIDTH), BF16),
        scratch_shapes=[pltpu.VMEM((2, seq, WIDTH), BF16), pltpu.VMEM((2, WIDTH, seq), BF16),
                        pltpu.VMEM((PAIRS, LANES, t), F32)],
        compiler_params=_params("arbitrary", "arbitrary"),
    )(plain, plain, plain)


def _moba_kernel(q_ref, k_ref, v_ref, o_ref, kmean_ref, kmask_ref, vaug_ref, acc_ref, bias_ref, prob_ref, *, nblocks):
    i = pl.program_id(1)
    t = MOBA_BLOCK
    nbp = MOBA_MAX_BLOCKS
    low = _lane() < HEAD_DIM
    mine = [low, jnp.logical_not(low)]
    chains = [(p, side) for p in range(PAIRS) for side in range(2)]

    @pl.when(i == 0)
    def _():
        for p in range(PAIRS):
            rows = [jnp.mean(k_ref[j * t:(j + 1) * t, p * LANES:(p + 1) * LANES].astype(F32), axis=0, keepdims=True)
                    for j in range(nblocks)]
            rows += [jnp.zeros((1, LANES), F32)] * (nbp - nblocks)
            km = jnp.concatenate(rows, axis=0)
            kmean_ref[2 * p * nbp:(2 * p + 1) * nbp, :] = jnp.where(low, km, 0.0)
            kmean_ref[(2 * p + 1) * nbp:(2 * p + 2) * nbp, :] = jnp.where(low, 0.0, km)
        for j in range(nblocks):
            rows = slice(j * t, (j + 1) * t)
            for side in range(2):
                kmask_ref[side, rows, :] = jnp.where(jnp.tile(mine[side], (1, PAIRS)), k_ref[rows, :], jnp.zeros((1, WIDTH), BF16))
                for p in range(PAIRS):
                    v = jnp.where(mine[side], v_ref[rows, p * LANES:(p + 1) * LANES].astype(F32), 1.0)
                    vaug_ref[side, p * LANES:(p + 1) * LANES, rows] = v.T.astype(BF16)

    gates = []
    for p in range(PAIRS):
        q = q_ref[:, p * LANES:(p + 1) * LANES]
        km = kmean_ref[2 * p * nbp:(2 * p + 2) * nbp, :]
        km_hi = km.astype(BF16)
        km_lo = (km - km_hi.astype(F32)).astype(BF16)
        gates.append((_dot_nt(km_hi, q) + _dot_nt(km_lo, q)).reshape(2, nbp, t))
    blk = lax.broadcasted_iota(jnp.int32, (HEADS, nbp, t), 1)
    gate = jnp.where(blk < i, jnp.concatenate(gates, axis=0), NEG)
    count = jnp.zeros((HEADS, nbp, t), F32)
    for other in range(nblocks):
        g_other = gate[:, other:other + 1, :]
        beats = (g_other > gate) | ((g_other == gate) & (other < blk))
        count = count + jnp.where(beats, 1.0, 0.0)
    bias_ref[...] = jnp.where((count < min(MOBA_TOPK, nblocks)) & (blk < i), 0.0, NEG)
    qs = [q_ref[:, p * LANES:(p + 1) * LANES] * SCALE for p in range(PAIRS)]

    key = lax.broadcasted_iota(jnp.int32, (t, t), 0)
    query = lax.broadcasted_iota(jnp.int32, (t, t), 1)
    causal = key <= query

    def score(j):
        off = pl.multiple_of(j * t, t)
        return [_dot_nt(kmask_ref[side, pl.ds(off, t), p * LANES:(p + 1) * LANES], qs[p]) for p, side in chains]

    def weigh(scores, shift, maxes):
        new_max = [jnp.maximum(m, jnp.max(s, axis=0, keepdims=True) + b) for m, s, b in zip(maxes, scores, shift)]
        for n, (s, m, b) in enumerate(zip(scores, new_max, shift)):
            prob_ref[n] = jnp.exp(s - (m - b)).astype(BF16)
        return tuple(new_max), tuple(jnp.exp(mo - mn) for mo, mn in zip(maxes, new_max))

    def absorb(j, decay):
        off = pl.multiple_of(j * t, t)
        pvs = [_dot(vaug_ref[side, p * LANES:(p + 1) * LANES, pl.ds(off, t)], prob_ref[n]) for n, (p, side) in enumerate(chains)]
        for n, pv in enumerate(pvs):
            acc_ref[n] = acc_ref[n] * decay[n] + pv

    def step(j, carry):
        prev, maxes, decay = carry
        scores = score(j)
        absorb(prev, decay)
        maxes, decay = weigh(scores, [bias_ref[n, pl.ds(j, 1), :] for n in range(len(chains))], maxes)
        return j, maxes, decay

    acc_ref[...] = jnp.zeros(acc_ref.shape, F32)
    own = [jnp.where(causal, s, NEG) for s in score(i)]
    start = tuple(jnp.full((1, t), NEG, F32) for _ in chains)
    maxes, decay = weigh(own, [jnp.zeros((1, t), F32)] * len(chains), start)
    last, _, decay = lax.fori_loop(0, i, step, (i, maxes, decay))
    absorb(last, decay)
    accs = [acc_ref[n] for n in range(len(chains))]
    for p in range(PAIRS):
        a_lo, a_hi = accs[2 * p], accs[2 * p + 1]
        o_t = jnp.concatenate([a_lo[:HEAD_DIM] / a_lo[HEAD_DIM:], a_hi[HEAD_DIM:] / a_hi[:HEAD_DIM]], axis=0)
        o_ref[:, p * LANES:(p + 1) * LANES] = o_t.T.astype(o_ref.dtype)


def _moba(roped, plain, *, batch, seq):
    t = MOBA_BLOCK
    nq = seq // t
    assert seq % t == 0 and nq <= MOBA_MAX_BLOCKS
    return pl.pallas_call(
        functools.partial(_moba_kernel, nblocks=nq),
        grid=(batch, nq),
        in_specs=[
            pl.BlockSpec((t, WIDTH), lambda b, i: (b * nq + i, QC)),
            pl.BlockSpec((seq, WIDTH), lambda b, i: (b, KC)),
            pl.BlockSpec((seq, WIDTH), lambda b, i: (b, VC)),
        ],
        out_specs=pl.BlockSpec((t, WIDTH), lambda b, i: (b * nq + i, 0)),
        out_shape=jax.ShapeDtypeStruct((batch * seq, WIDTH), BF16),
        scratch_shapes=[pltpu.VMEM((HEADS * MOBA_MAX_BLOCKS, LANES), F32), pltpu.VMEM((2, seq, WIDTH), BF16),
                        pltpu.VMEM((2, WIDTH, seq), BF16), pltpu.VMEM((HEADS, LANES, t), F32),
                        pltpu.VMEM((HEADS, MOBA_MAX_BLOCKS, t), F32), pltpu.VMEM((HEADS, t, t), BF16)],
        compiler_params=_params("arbitrary", "arbitrary"),
    )(roped, roped, plain)


def _merge_kernel(oa_ref, ob_ref, oc_ref, gate_ref, wb_ref, wo_ref, x_ref, o_ref):
    d = x_ref.shape[1]
    ys = [_dot(branch[...], wb_ref[n].astype(BF16)) for n, branch in enumerate((oa_ref, ob_ref, oc_ref))]
    merged = None
    for n, y in enumerate(ys):
        term = y * gate_ref[:, n * d:(n + 1) * d].astype(F32)
        merged = term if merged is None else merged + term
    o_ref[...] = x_ref[...] + _dot(merged.astype(BF16), wo_ref[...].astype(BF16))


def _merge(oa, ob, oc, gates, wb_layers, wo_layers, layer, x, *, tm):
    t, d = x.shape
    bw = oa.shape[1]
    row = lambda i: (i, 0)
    return pl.pallas_call(
        _merge_kernel,
        grid=(t // tm,),
        in_specs=[
            pl.BlockSpec((tm, bw), row),
            pl.BlockSpec((tm, bw), row),
            pl.BlockSpec((tm, bw), row),
            pl.BlockSpec((tm, N_BRANCHES * d), row),
            pl.BlockSpec((None, N_BRANCHES, bw, d), lambda i: (layer, 0, 0, 0), pipeline_mode=pl.Buffered(1)),
            pl.BlockSpec((None, d, d), lambda i: (layer, 0, 0), pipeline_mode=pl.Buffered(1)),
            pl.BlockSpec((tm, d), row),
        ],
        out_specs=pl.BlockSpec((tm, d), row),
        out_shape=jax.ShapeDtypeStruct((t, d), F32),
        compiler_params=_params("arbitrary"),
    )(oa, ob, oc, gates, wb_layers, wo_layers, x)


def _shift_rows(u, tail, shift):
    rolled = pltpu.roll(u, shift, axis=0)
    head = jnp.where(lax.broadcasted_iota(jnp.int32, tail.shape, 0) < shift,
                     pltpu.roll(tail, shift, axis=0), rolled[:8])
    return jnp.concatenate([head, rolled[8:]], axis=0)


def _ffn_kernel(x_ref, g_ref, wup_ref, cw_ref, cb_ref, wdn_ref, gf_ref, o_ref, tail_ref, acc_ref,
                *, tiles_per_seq, chunk, final_norm):
    i = pl.program_id(0)
    tm = x_ref.shape[0]
    dff = wdn_ref.shape[0]

    @pl.when(i % tiles_per_seq == 0)
    def _():
        tail_ref[...] = jnp.zeros(tail_ref.shape, F32)

    x = x_ref[...]
    ms = jnp.mean(x * x, axis=-1, keepdims=True)
    h = (x * lax.rsqrt(ms + RMS_EPS) * g_ref[...]).astype(BF16)

    def up(c0, c1):
        return _dot(h, wup_ref[:, c0:c1]), _dot(h, wup_ref[:, dff + c0:dff + c1])

    def conv(u, c0, c1):
        tail = tail_ref[:, c0:c1]
        tail_ref[:, c0:c1] = u[tm - 8:, :]
        out = cb_ref[:, c0:c1] + cw_ref[0:1, c0:c1] * _shift_rows(u, tail, 2)
        out = out + cw_ref[1:2, c0:c1] * _shift_rows(u, tail, 1)
        return out + cw_ref[2:3, c0:c1] * u

    bounds = [(c0, min(c0 + chunk, dff)) for c0 in range(0, dff, chunk)]
    pre = up(*bounds[0])
    for n, (c0, c1) in enumerate(bounds):
        cur = pre
        if n + 1 < len(bounds):
            pre = up(*bounds[n + 1])
        ua = conv(cur[0], c0, c1)
        uv = conv(cur[1], dff + c0, dff + c1)
        act = (ua / (1.0 + jnp.exp(-ua)) * uv).astype(BF16)
        part = _dot(act, wdn_ref[c0:c1, :])
        if n == 0:
            acc_ref[...] = x + part
        else:
            acc_ref[...] += part

    y = acc_ref[...]
    if final_norm:
        ms = jnp.mean(y * y, axis=-1, keepdims=True)
        y = y * lax.rsqrt(ms + RMS_EPS) * gf_ref[...]
    o_ref[...] = y


def _ffn(x, gain, wup_layers, cw, cb, wdn_layers, layer, gain_final, *, seq, tm, chunk, final_norm):
    t, d = x.shape
    dff = wdn_layers.shape[1]
    assert seq % tm == 0
    fixed = lambda i: (0, 0)
    picked = lambda i: (layer, 0, 0)
    once = pl.Buffered(1)
    return pl.pallas_call(
        functools.partial(_ffn_kernel, tiles_per_seq=seq // tm, chunk=chunk, final_norm=final_norm),
        grid=(t // tm,),
        in_specs=[
            pl.BlockSpec((tm, d), lambda i: (i, 0)),
            pl.BlockSpec((1, d), fixed),
            pl.BlockSpec((None, d, 2 * dff), picked, pipeline_mode=once),
            pl.BlockSpec((3, 2 * dff), fixed),
            pl.BlockSpec((1, 2 * dff), fixed),
            pl.BlockSpec((None, dff, d), picked, pipeline_mode=once),
            pl.BlockSpec((1, d), fixed),
        ],
        out_specs=pl.BlockSpec((tm, d), lambda i: (i, 0)),
        out_shape=jax.ShapeDtypeStruct((t, d), F32),
        scratch_shapes=[pltpu.VMEM((8, 2 * dff), F32), pltpu.VMEM((tm, d), F32)],
        compiler_params=_params("arbitrary"),
    )(x, gain, wup_layers, cw, cb, wdn_layers, gain_final)


def _rope_tables(seq):
    inv = ROPE_THETA ** (-jnp.arange(0, HEAD_DIM, 2, dtype=F32) / HEAD_DIM)
    ang = jnp.arange(seq, dtype=F32)[:, None] * inv[None, :]
    cos, sin = jnp.cos(ang), jnp.sin(ang)
    reps = LANES // HEAD_DIM
    return jnp.tile(cos, (1, 2 * reps)), jnp.tile(jnp.concatenate([-sin, sin], axis=1), (1, reps))


def kernel(x, norm_mix, w_in, b_gate, sinks, w_branch, w_out, norm_ffn, w_up, conv_w, conv_b, w_down, norm_final):
    batch, seq, d = x.shape
    depth = w_in.shape[0]
    tokens = batch * seq
    cos, sin_signed = _rope_tables(seq)
    xt = x.reshape(tokens, d)
    tm_tok, tm_ffn = min(512, seq), min(1024, seq)
    wup, wdn = w_up.astype(BF16), w_down.astype(BF16)
    for layer in range(depth):
        roped, plain, gates = _in_project(xt, norm_mix[layer].reshape(1, d), w_in, layer,
                                          cos, sin_signed, b_gate[layer].reshape(1, N_BRANCHES * d),
                                          seq=seq, tm=tm_tok, chunk=512)
        sink_rows = jnp.broadcast_to(sinks[layer].astype(F32)[:, None], (HEADS, LANES))
        o_a = _swa(roped, plain, sink_rows, batch=batch, seq=seq)
        o_b = _stick_breaking(plain, batch=batch, seq=seq)
        o_c = _moba(roped, plain, batch=batch, seq=seq)
        xt = _merge(o_a, o_b, o_c, gates, w_branch, w_out, layer, xt, tm=tm_tok)
        xt = _ffn(xt, norm_ffn[layer].reshape(1, d), wup, conv_w[layer],
                  conv_b[layer].reshape(1, -1), wdn, layer, norm_final.reshape(1, d),
                  seq=seq, tm=tm_ffn, chunk=1536, final_norm=(layer == depth - 1))
    return xt.reshape(batch, seq, d)
```

```python
import functools
import math

import jax
import jax.numpy as jnp
from jax import lax
from jax.experimental import pallas as pl
from jax.experimental.pallas import tpu as pltpu

F32 = jnp.float32
BF16 = jnp.bfloat16

HEAD_DIM = 64
LANES = 128
HEADS = 8
PAIRS = HEADS // 2
WIDTH = HEADS * HEAD_DIM
SWA_KV_HEADS = 2
SWA_WINDOW = 128
SWA_MAX_WINDOWS_PER_STEP = 8
SB_BLOCK = 256
SB_PAST = 128
SB_DONE = 88.0
MOBA_BLOCK = 256
MOBA_TOPK = 3
MOBA_MAX_BLOCKS = 8
N_BRANCHES = 3
ROPE_THETA = 10000.0
RMS_EPS = 1e-6
SCALE = HEAD_DIM ** -0.5
LOG2E = 1.4426950408889634
NEG = -1e30

VMEM_LIMIT = 56 * 1024 * 1024

ROPE_COLS = 3 * WIDTH + LANES
PLAIN_COLS = 4 * WIDTH + LANES
_KV = SWA_KV_HEADS * HEAD_DIM
_B0, _C0 = WIDTH + 2 * _KV, WIDTH + 2 * _KV + 3 * WIDTH
ROPE_SRC = ((0, WIDTH), (_C0, 2 * WIDTH), (WIDTH, _KV))
PLAIN_SRC = ((_B0, 3 * WIDTH), (_C0 + 2 * WIDTH, WIDTH), (WIDTH + _KV, _KV))
GATE_START = _C0 + 3 * WIDTH
QA, QC, KC = range(3)
KA = 3 * PAIRS
QB, KB, VB, VC = range(4)
VA = 4 * PAIRS


def _params(*sem):
    return pltpu.CompilerParams(dimension_semantics=sem, vmem_limit_bytes=VMEM_LIMIT)


def _dot(a, b):
    return jnp.dot(a, b, preferred_element_type=F32)


def _dot_nt(a, b):
    return lax.dot_general(a, b, (((1,), (1,)), ((), ())), preferred_element_type=F32)


def _lane():
    return lax.broadcasted_iota(jnp.int32, (1, LANES), 1)


def _rope(piece, cos, sin_signed):
    first_half = (_lane() & 32) == 0
    nxt = pltpu.roll(piece, LANES - 32, axis=1)
    prv = pltpu.roll(piece, 32, axis=1)
    return piece * cos + jnp.where(first_half, nxt, prv) * sin_signed


def _in_proj_kernel(x_ref, g_ref, w_ref, cos_ref, sin_ref, bias_ref, rope_ref, plain_ref, gate_ref, *, chunk):
    x = x_ref[...]
    ms = jnp.mean(x * x, axis=-1, keepdims=True)
    h = (x * lax.rsqrt(ms + RMS_EPS) * g_ref[...]).astype(BF16)
    cos, sin_signed = cos_ref[...], sin_ref[...]
    for out_ref, pieces in ((rope_ref, ROPE_SRC), (plain_ref, PLAIN_SRC), (gate_ref, ((GATE_START, gate_ref.shape[1]),))):
        dst = 0
        for src, width in pieces:
            for c0 in range(0, width, chunk):
                n = min(chunk, width - c0)
                acc = _dot(h, w_ref[:, src + c0:src + c0 + n].astype(BF16))
                lo = dst + c0
                if out_ref is rope_ref:
                    for c in range(0, n, LANES):
                        out_ref[:, lo + c:lo + c + LANES] = _rope(acc[:, c:c + LANES], cos, sin_signed).astype(out_ref.dtype)
                elif out_ref is gate_ref:
                    out_ref[:, lo:lo + n] = (acc + bias_ref[:, lo:lo + n]).astype(out_ref.dtype)
                else:
                    out_ref[:, lo:lo + n] = acc.astype(out_ref.dtype)
            dst += width


def _in_project(x, gain, w_layers, layer, cos, sin_signed, gate_bias, *, seq, tm, chunk):
    t, d = x.shape
    n_in = w_layers.shape[2]
    n_gate = n_in - ROPE_COLS - PLAIN_COLS
    assert t % tm == 0 and seq % tm == 0
    pos_tiles = seq // tm
    row = lambda i: (i, 0)
    return pl.pallas_call(
        functools.partial(_in_proj_kernel, chunk=chunk),
        grid=(t // tm,),
        in_specs=[
            pl.BlockSpec((tm, d), row),
            pl.BlockSpec((1, d), lambda i: (0, 0)),
            pl.BlockSpec((None, d, n_in), lambda i: (layer, 0, 0), pipeline_mode=pl.Buffered(1)),
            pl.BlockSpec((tm, LANES), lambda i: (i % pos_tiles, 0)),
            pl.BlockSpec((tm, LANES), lambda i: (i % pos_tiles, 0)),
            pl.BlockSpec((1, n_gate), lambda i: (0, 0)),
        ],
        out_specs=[pl.BlockSpec((tm, ROPE_COLS), row), pl.BlockSpec((tm, PLAIN_COLS), row),
                   pl.BlockSpec((tm, n_gate), row)],
        out_shape=[jax.ShapeDtypeStruct((t, ROPE_COLS), BF16), jax.ShapeDtypeStruct((t, PLAIN_COLS), BF16),
                   jax.ShapeDtypeStruct((t, n_gate), BF16)],
        compiler_params=_params("arbitrary"),
    )(x, gain, w_layers, cos, sin_signed, gate_bias)


def _swa_kernel(q_ref, k_ref, v_ref, sink_ref, o_ref, kat_ref, vt_ref):
    n = pl.program_id(1)
    w = SWA_WINDOW
    windows = q_ref.shape[0] // w
    low = _lane() < HEAD_DIM
    chains = [(g, side) for g in range(SWA_KV_HEADS) for side in range(2)]

    @pl.when(n == 0)
    def _():
        k = k_ref[...].astype(F32)
        v = v_ref[...].astype(F32)
        k_sw = pltpu.roll(k, HEAD_DIM, axis=1)
        v_sw = pltpu.roll(v, HEAD_DIM, axis=1)
        k_at = [jnp.where(low, k, 0.0), jnp.where(low, 0.0, k_sw), jnp.where(low, k_sw, 0.0), jnp.where(low, 0.0, k)]
        v_at = [jnp.where(low, v, 1.0), jnp.where(low, 1.0, v_sw), jnp.where(low, v_sw, 1.0), jnp.where(low, 1.0, v)]
        for c in range(len(chains)):
            kat_ref[c] = k_at[c].astype(BF16)
            vt_ref[c] = v_at[c].T.astype(BF16)

    key_i = lax.broadcasted_iota(jnp.int32, (2 * w, 2 * w), 0)
    query_i = lax.broadcasted_iota(jnp.int32, (2 * w, 2 * w), 1) & (w - 1)
    sinks = []
    for g, side in chains:
        h0 = 4 * g + side
        sinks.append(jnp.concatenate([sink_ref[h0:h0 + 1, :], sink_ref[h0 + 2:h0 + 3, :]], axis=1))
    starts, scores = [], []
    for win in range(windows):
        blk = n * windows + win
        start = pl.multiple_of(jnp.maximum(blk - 1, 0) * w, w)
        diff = (blk * w + query_i) - (start + key_i)
        valid = (diff >= 0) & (diff < w)
        rows = slice(win * w, (win + 1) * w)
        qs = [jnp.concatenate([q_ref[rows, (2 * g) * LANES:(2 * g + 1) * LANES],
                               q_ref[rows, (2 * g + 1) * LANES:(2 * g + 2) * LANES]], axis=0) * SCALE
              for g in range(SWA_KV_HEADS)]
        starts.append(start)
        scores.append([jnp.where(valid, _dot_nt(kat_ref[c, pl.ds(start, 2 * w), :], qs[g]), NEG)
                       for c, (g, side) in enumerate(chains)])
    maxes = [[jnp.maximum(jnp.max(s, axis=0, keepdims=True), sink) for s, sink in zip(win_scores, sinks)]
             for win_scores in scores]
    probs = [[jnp.exp(s - m).astype(BF16) for s, m in zip(win_scores, win_maxes)]
             for win_scores, win_maxes in zip(scores, maxes)]
    outs = [[_dot(vt_ref[c, :, pl.ds(start, 2 * w)], p) for c, p in enumerate(win_probs)]
            for start, win_probs in zip(starts, probs)]
    for win in range(windows):
        for g in range(SWA_KV_HEADS):
            o_lo, o_hi = outs[win][2 * g], outs[win][2 * g + 1]
            d_lo = o_lo[HEAD_DIM:] + jnp.exp(sinks[2 * g] - maxes[win][2 * g])
            d_hi = o_hi[:HEAD_DIM] + jnp.exp(sinks[2 * g + 1] - maxes[win][2 * g + 1])
            o = jnp.concatenate([o_lo[:HEAD_DIM] / d_lo, o_hi[HEAD_DIM:] / d_hi], axis=0).T.astype(o_ref.dtype)
            o_ref[win * w:(win + 1) * w, (2 * g) * LANES:(2 * g + 1) * LANES] = o[:w]
            o_ref[win * w:(win + 1) * w, (2 * g + 1) * LANES:(2 * g + 2) * LANES] = o[w:]


def _swa(roped, plain, sink_rows, *, batch, seq):
    tq = SWA_WINDOW * math.gcd(seq // SWA_WINDOW, SWA_MAX_WINDOWS_PER_STEP)
    steps = seq // tq
    return pl.pallas_call(
        _swa_kernel,
        grid=(batch, steps),
        in_specs=[
            pl.BlockSpec((tq, WIDTH), lambda b, n: (b * steps + n, QA)),
            pl.BlockSpec((seq, LANES), lambda b, n: (b, KA)),
            pl.BlockSpec((seq, LANES), lambda b, n: (b, VA)),
            pl.BlockSpec((HEADS, LANES), lambda b, n: (0, 0)),
        ],
        out_specs=pl.BlockSpec((tq, WIDTH), lambda b, n: (b * steps + n, 0)),
        out_shape=jax.ShapeDtypeStruct((batch * seq, WIDTH), BF16),
        scratch_shapes=[pltpu.VMEM((2 * SWA_KV_HEADS, seq, LANES), BF16), pltpu.VMEM((2 * SWA_KV_HEADS, LANES, seq), BF16)],
        compiler_params=_params("arbitrary", "arbitrary"),
    )(roped, roped, plain, sink_rows)


def _softplus(z):
    return jnp.maximum(z, 0.0) + jnp.log(1.0 + jnp.exp2(jnp.abs(z) * -LOG2E))


def _sb_kernel(q_ref, k_ref, v_ref, o_ref, kmask_ref, vt_ref, acc_ref):
    i = pl.program_id(1)
    t = SB_BLOCK
    nblocks = k_ref.shape[0] // t
    low = _lane() < HEAD_DIM
    mine = [low, jnp.logical_not(low)]
    chains = [(p, side) for p in range(PAIRS) for side in range(2)]

    @pl.when(i == 0)
    def _():
        for j in range(nblocks):
            rows = slice(j * t, (j + 1) * t)
            for side in range(2):
                kmask_ref[side, rows, :] = jnp.where(jnp.tile(mine[side], (1, PAIRS)), k_ref[rows, :], jnp.zeros((1, WIDTH), BF16))
                for p in range(PAIRS):
                    v = jnp.where(mine[side], v_ref[rows, p * LANES:(p + 1) * LANES].astype(F32), 0.0)
                    vt_ref[side, p * LANES:(p + 1) * LANES, rows] = v.T.astype(BF16)

    key = lax.broadcasted_iota(jnp.int32, (t, t), 0)
    query = lax.broadcasted_iota(jnp.int32, (t, t), 1)
    suffix = jnp.where(query >= key, 1.0, 0.0).astype(BF16)
    past = key < query
    qs = [q_ref[:, p * LANES:(p + 1) * LANES] * SCALE for p in range(PAIRS)]

    def block(off, size, remains, diagonal):
        zs = [_dot_nt(kmask_ref[side, pl.ds(off, size), p * LANES:(p + 1) * LANES], qs[p]) for p, side in chains]
        sps = []
        for z in zs:
            sp = _softplus(z)
            sps.append(jnp.where(past, sp, 0.0) if diagonal else sp)
        cs = [_dot(suffix[:size, :size], sp.astype(BF16)) for sp in sps]
        wgts = []
        for n, (z, c) in enumerate(zip(zs, cs)):
            logw = z - c - remains[n]
            if diagonal:
                logw = jnp.where(past, logw, NEG)
            wgts.append(jnp.exp(logw).astype(BF16))
        pvs = [_dot(vt_ref[side, p * LANES:(p + 1) * LANES, pl.ds(off, size)], wgts[n]) for n, (p, side) in enumerate(chains)]
        for p in range(PAIRS):
            both = pvs[2 * p] + pvs[2 * p + 1]
            acc_ref[p] = both if diagonal else acc_ref[p] + both
        return tuple(remains[n] + cs[n][0:1, :] for n in range(len(chains)))

    def smallest(remains):
        return jnp.min(functools.reduce(jnp.minimum, remains))

    remains = block(pl.multiple_of(i * t, t), t, tuple(jnp.zeros((1, t), F32) for _ in chains), True)

    runs = i * (t // SB_PAST)

    def cond(state):
        s, floor, _ = state
        return (s < runs) & (floor < SB_DONE)

    def body(state):
        s, _, remains = state
        remains = block(pl.multiple_of((runs - 1 - s) * SB_PAST, SB_PAST), SB_PAST, remains, False)
        return s + 1, smallest(remains), remains

    lax.while_loop(cond, body, (0, smallest(remains), remains))
    for p in range(PAIRS):
        o_ref[:, p * LANES:(p + 1) * LANES] = acc_ref[p].T.astype(o_ref.dtype)


def _stick_breaking(plain, *, batch, seq):
    t = SB_BLOCK
    nq = seq // t
    return pl.pallas_call(
        _sb_kernel,
        grid=(batch, nq),
        in_specs=[
            pl.BlockSpec((t, WIDTH), lambda b, i: (b * nq + i, QB)),
            pl.BlockSpec((seq, WIDTH), lambda b, i: (b, KB)),
            pl.BlockSpec((seq, WIDTH), lambda b, i: (b, VB)),
        ],
        out_specs=pl.BlockSpec((t, WIDTH), lambda b, i: (b * nq + i, 0)),
        out_shape=jax.ShapeDtypeStruct((batch * seq, WIDTH), BF16),
        scratch_shapes=[pltpu.VMEM((2, seq, WIDTH), BF16), pltpu.VMEM((2, WIDTH, seq), BF16),
                        pltpu.VMEM((PAIRS, LANES, t), F32)],
        compiler_params=_params("arbitrary", "arbitrary"),
    )(plain, plain, plain)


def _moba_kernel(q_ref, k_ref, v_ref, o_ref, kmean_ref, kmask_ref, vaug_ref, acc_ref, bias_ref, prob_ref, *, nblocks):
    i = pl.program_id(1)
    t = MOBA_BLOCK
    nbp = MOBA_MAX_BLOCKS
    low = _lane() < HEAD_DIM
    mine = [low, jnp.logical_not(low)]
    chains = [(p, side) for p in range(PAIRS) for side in range(2)]

    @pl.when(i == 0)
    def _():
        for p in range(PAIRS):
            rows = [jnp.mean(k_ref[j * t:(j + 1) * t, p * LANES:(p + 1) * LANES].astype(F32), axis=0, keepdims=True)
                    for j in range(nblocks)]
            rows += [jnp.zeros((1, LANES), F32)] * (nbp - nblocks)
            km = jnp.concatenate(rows, axis=0)
            kmean_ref[2 * p * nbp:(2 * p + 1) * nbp, :] = jnp.where(low, km, 0.0)
            kmean_ref[(2 * p + 1) * nbp:(2 * p + 2) * nbp, :] = jnp.where(low, 0.0, km)
        for j in range(nblocks):
            rows = slice(j * t, (j + 1) * t)
            for side in range(2):
                kmask_ref[side, rows, :] = jnp.where(jnp.tile(mine[side], (1, PAIRS)), k_ref[rows, :], jnp.zeros((1, WIDTH), BF16))
                for p in range(PAIRS):
                    v = jnp.where(mine[side], v_ref[rows, p * LANES:(p + 1) * LANES].astype(F32), 1.0)
                    vaug_ref[side, p * LANES:(p + 1) * LANES, rows] = v.T.astype(BF16)

    gates = []
    for p in range(PAIRS):
        q = q_ref[:, p * LANES:(p + 1) * LANES]
        km = kmean_ref[2 * p * nbp:(2 * p + 2) * nbp, :]
        km_hi = km.astype(BF16)
        km_lo = (km - km_hi.astype(F32)).astype(BF16)
        gates.append((_dot_nt(km_hi, q) + _dot_nt(km_lo, q)).reshape(2, nbp, t))
    blk = lax.broadcasted_iota(jnp.int32, (HEADS, nbp, t), 1)
    gate = jnp.where(blk < i, jnp.concatenate(gates, axis=0), NEG)
    count = jnp.zeros((HEADS, nbp, t), F32)
    for other in range(nblocks):
        g_other = gate[:, other:other + 1, :]
        beats = (g_other > gate) | ((g_other == gate) & (other < blk))
        count = count + jnp.where(beats, 1.0, 0.0)
    bias_ref[...] = jnp.where((count < min(MOBA_TOPK, nblocks)) & (blk < i), 0.0, NEG)
    qs = [q_ref[:, p * LANES:(p + 1) * LANES] * SCALE for p in range(PAIRS)]

    key = lax.broadcasted_iota(jnp.int32, (t, t), 0)
    query = lax.broadcasted_iota(jnp.int32, (t, t), 1)
    causal = key <= query

    def score(j):
        off = pl.multiple_of(j * t, t)
        return [_dot_nt(kmask_ref[side, pl.ds(off, t), p * LANES:(p + 1) * LANES], qs[p]) for p, side in chains]

    def weigh(scores, shift, maxes):
        new_max = [jnp.maximum(m, jnp.max(s, axis=0, keepdims=True) + b) for m, s, b in zip(maxes, scores, shift)]
        for n, (s, m, b) in enumerate(zip(scores, new_max, shift)):
            prob_ref[n] = jnp.exp(s - (m - b)).astype(BF16)
        return tuple(new_max), tuple(jnp.exp(mo - mn) for mo, mn in zip(maxes, new_max))

    def absorb(j, decay):
        off = pl.multiple_of(j * t, t)
        pvs = [_dot(vaug_ref[side, p * LANES:(p + 1) * LANES, pl.ds(off, t)], prob_ref[n]) for n, (p, side) in enumerate(chains)]
        for n, pv in enumerate(pvs):
            acc_ref[n] = acc_ref[n] * decay[n] + pv

    def step(j, carry):
        prev, maxes, decay = carry
        scores = score(j)
        absorb(prev, decay)
        maxes, decay = weigh(scores, [bias_ref[n, pl.ds(j, 1), :] for n in range(len(chains))], maxes)
        return j, maxes, decay

    acc_ref[...] = jnp.zeros(acc_ref.shape, F32)
    own = [jnp.where(causal, s, NEG) for s in score(i)]
    start = tuple(jnp.full((1, t), NEG, F32) for _ in chains)
    maxes, decay = weigh(own, [jnp.zeros((1, t), F32)] * len(chains), start)
    last, _, decay = lax.fori_loop(0, i, step, (i, maxes, decay))
    absorb(last, decay)
    accs = [acc_ref[n] for n in range(len(chains))]
    for p in range(PAIRS):
        a_lo, a_hi = accs[2 * p], accs[2 * p + 1]
        o_t = jnp.concatenate([a_lo[:HEAD_DIM] / a_lo[HEAD_DIM:], a_hi[HEAD_DIM:] / a_hi[:HEAD_DIM]], axis=0)
        o_ref[:, p * LANES:(p + 1) * LANES] = o_t.T.astype(o_ref.dtype)


def _moba(roped, plain, *, batch, seq):
    t = MOBA_BLOCK
    nq = seq // t
    assert seq % t == 0 and nq <= MOBA_MAX_BLOCKS
    return pl.pallas_call(
        functools.partial(_moba_kernel, nblocks=nq),
        grid=(batch, nq),
        in_specs=[
            pl.BlockSpec((t, WIDTH), lambda b, i: (b * nq + i, QC)),
            pl.BlockSpec((seq, WIDTH), lambda b, i: (b, KC)),
            pl.BlockSpec((seq, WIDTH), lambda b, i: (b, VC)),
        ],
        out_specs=pl.BlockSpec((t, WIDTH), lambda b, i: (b * nq + i, 0)),
        out_shape=jax.ShapeDtypeStruct((batch * seq, WIDTH), BF16),
        scratch_shapes=[pltpu.VMEM((HEADS * MOBA_MAX_BLOCKS, LANES), F32), pltpu.VMEM((2, seq, WIDTH), BF16),
                        pltpu.VMEM((2, WIDTH, seq), BF16), pltpu.VMEM((HEADS, LANES, t), F32),
                        pltpu.VMEM((HEADS, MOBA_MAX_BLOCKS, t), F32), pltpu.VMEM((HEADS, t, t), BF16)],
        compiler_params=_params("arbitrary", "arbitrary"),
    )(roped, roped, plain)


def _merge_kernel(oa_ref, ob_ref, oc_ref, gate_ref, wb_ref, wo_ref, x_ref, o_ref):
    d = x_ref.shape[1]
    ys = [_dot(branch[...], wb_ref[n].astype(BF16)) for n, branch in enumerate((oa_ref, ob_ref, oc_ref))]
    merged = None
    for n, y in enumerate(ys):
        term = y / (1.0 + jnp.exp(-gate_ref[:, n * d:(n + 1) * d].astype(F32)))
        merged = term if merged is None else merged + term
    o_ref[...] = x_ref[...] + _dot(merged.astype(BF16), wo_ref[...].astype(BF16))


def _merge(oa, ob, oc, gates, wb_layers, wo_layers, layer, x, *, tm):
    t, d = x.shape
    bw = oa.shape[1]
    row = lambda i: (i, 0)
    return pl.pallas_call(
        _merge_kernel,
        grid=(t // tm,),
        in_specs=[
            pl.BlockSpec((tm, bw), row),
            pl.BlockSpec((tm, bw), row),
            pl.BlockSpec((tm, bw), row),
            pl.BlockSpec((tm, N_BRANCHES * d), row),
            pl.BlockSpec((None, N_BRANCHES, bw, d), lambda i: (layer, 0, 0, 0), pipeline_mode=pl.Buffered(1)),
            pl.BlockSpec((None, d, d), lambda i: (layer, 0, 0), pipeline_mode=pl.Buffered(1)),
            pl.BlockSpec((tm, d), row),
        ],
        out_specs=pl.BlockSpec((tm, d), row),
        out_shape=jax.ShapeDtypeStruct((t, d), F32),
        compiler_params=_params("arbitrary"),
    )(oa, ob, oc, gates, wb_layers, wo_layers, x)


def _shift_rows(u, tail, shift):
    rolled = pltpu.roll(u, shift, axis=0)
    head = jnp.where(lax.broadcasted_iota(jnp.int32, tail.shape, 0) < shift,
                     pltpu.roll(tail, shift, axis=0), rolled[:8])
    return jnp.concatenate([head, rolled[8:]], axis=0)


def _ffn_kernel(x_ref, g_ref, wup_ref, cw_ref, cb_ref, wdn_ref, gf_ref, o_ref, tail_ref, acc_ref,
                *, tiles_per_seq, chunk, final_norm):
    i = pl.program_id(0)
    tm = x_ref.shape[0]
    dff = wdn_ref.shape[0]

    @pl.when(i % tiles_per_seq == 0)
    def _():
        tail_ref[...] = jnp.zeros(tail_ref.shape, F32)

    x = x_ref[...]
    ms = jnp.mean(x * x, axis=-1, keepdims=True)
    h = (x * lax.rsqrt(ms + RMS_EPS) * g_ref[...]).astype(BF16)

    def up(c0, c1):
        return _dot(h, wup_ref[:, c0:c1]), _dot(h, wup_ref[:, dff + c0:dff + c1])

    def conv(u, c0, c1):
        tail = tail_ref[:, c0:c1]
        tail_ref[:, c0:c1] = u[tm - 8:, :]
        out = cb_ref[:, c0:c1] + cw_ref[0:1, c0:c1] * _shift_rows(u, tail, 2)
        out = out + cw_ref[1:2, c0:c1] * _shift_rows(u, tail, 1)
        return out + cw_ref[2:3, c0:c1] * u

    bounds = [(c0, min(c0 + chunk, dff)) for c0 in range(0, dff, chunk)]
    pre = up(*bounds[0])
    for n, (c0, c1) in enumerate(bounds):
        cur = pre
        if n + 1 < len(bounds):
            pre = up(*bounds[n + 1])
        ua = conv(cur[0], c0, c1)
        uv = conv(cur[1], dff + c0, dff + c1)
        act = (ua / (1.0 + jnp.exp(-ua)) * uv).astype(BF16)
        part = _dot(act, wdn_ref[c0:c1, :])
        if n == 0:
            acc_ref[...] = x + part
        else:
            acc_ref[...] += part

    y = acc_ref[...]
    if final_norm:
        ms = jnp.mean(y * y, axis=-1, keepdims=True)
        y = y * lax.rsqrt(ms + RMS_EPS) * gf_ref[...]
    o_ref[...] = y


def _ffn(x, gain, wup_layers, cw, cb, wdn_layers, layer, gain_final, *, seq, tm, chunk, final_norm):
    t, d = x.shape
    dff = wdn_layers.shape[1]
    assert seq % tm == 0
    fixed = lambda i: (0, 0)
    picked = lambda i: (layer, 0, 0)
    once = pl.Buffered(1)
    return pl.pallas_call(
        functools.partial(_ffn_kernel, tiles_per_seq=seq // tm, chunk=chunk, final_norm=final_norm),
        grid=(t // tm,),
        in_specs=[
            pl.BlockSpec((tm, d), lambda i: (i, 0)),
            pl.BlockSpec((1, d), fixed),
            pl.BlockSpec((None, d, 2 * dff), picked, pipeline_mode=once),
            pl.BlockSpec((3, 2 * dff), fixed),
            pl.BlockSpec((1, 2 * dff), fixed),
            pl.BlockSpec((None, dff, d), picked, pipeline_mode=once),
            pl.BlockSpec((1, d), fixed),
        ],
        out_specs=pl.BlockSpec((tm, d), lambda i: (i, 0)),
        out_shape=jax.ShapeDtypeStruct((t, d), F32),
        scratch_shapes=[pltpu.VMEM((8, 2 * dff), F32), pltpu.VMEM((tm, d), F32)],
        compiler_params=_params("arbitrary"),
    )(x, gain, wup_layers, cw, cb, wdn_layers, gain_final)


def _rope_tables(seq):
    inv = ROPE_THETA ** (-jnp.arange(0, HEAD_DIM, 2, dtype=F32) / HEAD_DIM)
    ang = jnp.arange(seq, dtype=F32)[:, None] * inv[None, :]
    cos, sin = jnp.cos(ang), jnp.sin(ang)
    reps = LANES // HEAD_DIM
    return jnp.tile(cos, (1, 2 * reps)), jnp.tile(jnp.concatenate([-sin, sin], axis=1), (1, reps))


def kernel(x, norm_mix, w_in, b_gate, sinks, w_branch, w_out, norm_ffn, w_up, conv_w, conv_b, w_down, norm_final):
    batch, seq, d = x.shape
    depth = w_in.shape[0]
    tokens = batch * seq
    cos, sin_signed = _rope_tables(seq)
    xt = x.reshape(tokens, d)
    tm_tok, tm_ffn = min(512, seq), min(1024, seq)
    wup, wdn = w_up.astype(BF16), w_down.astype(BF16)
    for layer in range(depth):
        roped, plain, gates = _in_project(xt, norm_mix[layer].reshape(1, d), w_in, layer,
                                          cos, sin_signed, b_gate[layer].reshape(1, N_BRANCHES * d),
                                          seq=seq, tm=tm_tok, chunk=512)
        sink_rows = jnp.broadcast_to(sinks[layer].astype(F32)[:, None], (HEADS, LANES))
        o_a = _swa(roped, plain, sink_rows, batch=batch, seq=seq)
        o_b = _stick_breaking(plain, batch=batch, seq=seq)
        o_c = _moba(roped, plain, batch=batch, seq=seq)
        xt = _merge(o_a, o_b, o_c, gates, w_branch, w_out, layer, xt, tm=tm_tok)
        xt = _ffn(xt, norm_ffn[layer].reshape(1, d), wup, conv_w[layer],
                  conv_b[layer].reshape(1, -1), wdn, layer, norm_final.reshape(1, d),
                  seq=seq, tm=tm_ffn, chunk=1536, final_norm=(layer == depth - 1))
    return xt.reshape(batch, seq, d)
```

```python
import functools
import math

import jax
import jax.numpy as jnp
from jax import lax
from jax.experimental import pallas as pl
from jax.experimental.pallas import tpu as pltpu

F32 = jnp.float32
BF16 = jnp.bfloat16

HEAD_DIM = 64
LANES = 128
HEADS = 8
PAIRS = HEADS // 2
WIDTH = HEADS * HEAD_DIM
SWA_KV_HEADS = 2
SWA_WINDOW = 128
SWA_MAX_WINDOWS_PER_STEP = 16
SB_BLOCK = 256
SB_PAST = 128
SB_DONE = 88.0
MOBA_BLOCK = 256
MOBA_TOPK = 3
MOBA_MAX_BLOCKS = 8
N_BRANCHES = 3
ROPE_THETA = 10000.0
RMS_EPS = 1e-6
SCALE = HEAD_DIM ** -0.5
LOG2E = 1.4426950408889634
NEG = -1e30

VMEM_LIMIT = 56 * 1024 * 1024

ROPE_COLS = 3 * WIDTH + LANES
PLAIN_COLS = 4 * WIDTH + LANES
_KV = SWA_KV_HEADS * HEAD_DIM
_B0, _C0 = WIDTH + 2 * _KV, WIDTH + 2 * _KV + 3 * WIDTH
ROPE_SRC = ((0, WIDTH), (_C0, 2 * WIDTH), (WIDTH, _KV))
PLAIN_SRC = ((_B0, 3 * WIDTH), (_C0 + 2 * WIDTH, WIDTH), (WIDTH + _KV, _KV))
GATE_START = _C0 + 3 * WIDTH
QA, QC, KC = range(3)
KA = 3 * PAIRS
QB, KB, VB, VC = range(4)
VA = 4 * PAIRS


def _params(*sem):
    return pltpu.CompilerParams(dimension_semantics=sem, vmem_limit_bytes=VMEM_LIMIT)


def _dot(a, b):
    return jnp.dot(a, b, preferred_element_type=F32)


def _dot_nt(a, b):
    return lax.dot_general(a, b, (((1,), (1,)), ((), ())), preferred_element_type=F32)


def _lane():
    return lax.broadcasted_iota(jnp.int32, (1, LANES), 1)


def _rope(piece, cos, sin_signed):
    first_half = (_lane() & 32) == 0
    nxt = pltpu.roll(piece, LANES - 32, axis=1)
    prv = pltpu.roll(piece, 32, axis=1)
    return piece * cos + jnp.where(first_half, nxt, prv) * sin_signed


def _in_proj_kernel(x_ref, g_ref, w_ref, cos_ref, sin_ref, bias_ref, rope_ref, plain_ref, gate_ref, *, chunk):
    x = x_ref[...]
    ms = jnp.mean(x * x, axis=-1, keepdims=True)
    h = (x * lax.rsqrt(ms + RMS_EPS) * g_ref[...]).astype(BF16)
    cos, sin_signed = cos_ref[...], sin_ref[...]
    for out_ref, pieces in ((rope_ref, ROPE_SRC), (plain_ref, PLAIN_SRC), (gate_ref, ((GATE_START, gate_ref.shape[1]),))):
        dst = 0
        for src, width in pieces:
            for c0 in range(0, width, chunk):
                n = min(chunk, width - c0)
                acc = _dot(h, w_ref[:, src + c0:src + c0 + n].astype(BF16))
                lo = dst + c0
                if out_ref is rope_ref:
                    for c in range(0, n, LANES):
                        out_ref[:, lo + c:lo + c + LANES] = _rope(acc[:, c:c + LANES], cos, sin_signed).astype(out_ref.dtype)
                elif out_ref is gate_ref:
                    out_ref[:, lo:lo + n] = (acc + bias_ref[:, lo:lo + n]).astype(out_ref.dtype)
                else:
                    out_ref[:, lo:lo + n] = acc.astype(out_ref.dtype)
            dst += width


def _in_project(x, gain, w_layers, layer, cos, sin_signed, gate_bias, *, seq, tm, chunk):
    t, d = x.shape
    n_in = w_layers.shape[2]
    n_gate = n_in - ROPE_COLS - PLAIN_COLS
    assert t % tm == 0 and seq % tm == 0
    pos_tiles = seq // tm
    row = lambda i: (i, 0)
    return pl.pallas_call(
        functools.partial(_in_proj_kernel, chunk=chunk),
        grid=(t // tm,),
        in_specs=[
            pl.BlockSpec((tm, d), row),
            pl.BlockSpec((1, d), lambda i: (0, 0)),
            pl.BlockSpec((None, d, n_in), lambda i: (layer, 0, 0), pipeline_mode=pl.Buffered(1)),
            pl.BlockSpec((tm, LANES), lambda i: (i % pos_tiles, 0)),
            pl.BlockSpec((tm, LANES), lambda i: (i % pos_tiles, 0)),
            pl.BlockSpec((1, n_gate), lambda i: (0, 0)),
        ],
        out_specs=[pl.BlockSpec((tm, ROPE_COLS), row), pl.BlockSpec((tm, PLAIN_COLS), row),
                   pl.BlockSpec((tm, n_gate), row)],
        out_shape=[jax.ShapeDtypeStruct((t, ROPE_COLS), BF16), jax.ShapeDtypeStruct((t, PLAIN_COLS), BF16),
                   jax.ShapeDtypeStruct((t, n_gate), BF16)],
        compiler_params=_params("arbitrary"),
    )(x, gain, w_layers, cos, sin_signed, gate_bias)


def _swa_kernel(q_ref, k_ref, v_ref, sink_ref, o_ref, kat_ref, vt_ref):
    n = pl.program_id(1)
    w = SWA_WINDOW
    windows = q_ref.shape[0] // w
    low = _lane() < HEAD_DIM
    chains = [(g, side) for g in range(SWA_KV_HEADS) for side in range(2)]

    @pl.when(n == 0)
    def _():
        k = k_ref[...].astype(F32)
        v = v_ref[...].astype(F32)
        k_sw = pltpu.roll(k, HEAD_DIM, axis=1)
        v_sw = pltpu.roll(v, HEAD_DIM, axis=1)
        k_at = [jnp.where(low, k, 0.0), jnp.where(low, 0.0, k_sw), jnp.where(low, k_sw, 0.0), jnp.where(low, 0.0, k)]
        v_at = [jnp.where(low, v, 1.0), jnp.where(low, 1.0, v_sw), jnp.where(low, v_sw, 1.0), jnp.where(low, 1.0, v)]
        for c in range(len(chains)):
            kat_ref[c] = k_at[c].astype(BF16)
            vt_ref[c] = v_at[c].T.astype(BF16)

    key_i = lax.broadcasted_iota(jnp.int32, (2 * w, 2 * w), 0)
    query_i = lax.broadcasted_iota(jnp.int32, (2 * w, 2 * w), 1) & (w - 1)
    sinks = []
    for g, side in chains:
        h0 = 4 * g + side
        sinks.append(jnp.concatenate([sink_ref[h0:h0 + 1, :], sink_ref[h0 + 2:h0 + 3, :]], axis=1))
    starts, scores = [], []
    for win in range(windows):
        blk = n * windows + win
        start = pl.multiple_of(jnp.maximum(blk - 1, 0) * w, w)
        diff = (blk * w + query_i) - (start + key_i)
        valid = (diff >= 0) & (diff < w)
        rows = slice(win * w, (win + 1) * w)
        qs = [jnp.concatenate([q_ref[rows, (2 * g) * LANES:(2 * g + 1) * LANES],
                               q_ref[rows, (2 * g + 1) * LANES:(2 * g + 2) * LANES]], axis=0) * SCALE
              for g in range(SWA_KV_HEADS)]
        starts.append(start)
        scores.append([jnp.where(valid, _dot_nt(kat_ref[c, pl.ds(start, 2 * w), :], qs[g]), NEG)
                       for c, (g, side) in enumerate(chains)])
    maxes = [[jnp.maximum(jnp.max(s, axis=0, keepdims=True), sink) for s, sink in zip(win_scores, sinks)]
             for win_scores in scores]
    probs = [[jnp.exp(s - m).astype(BF16) for s, m in zip(win_scores, win_maxes)]
             for win_scores, win_maxes in zip(scores, maxes)]
    outs = [[_dot(vt_ref[c, :, pl.ds(start, 2 * w)], p) for c, p in enumerate(win_probs)]
            for start, win_probs in zip(starts, probs)]
    for win in range(windows):
        for g in range(SWA_KV_HEADS):
            o_lo, o_hi = outs[win][2 * g], outs[win][2 * g + 1]
            d_lo = o_lo[HEAD_DIM:] + jnp.exp(sinks[2 * g] - maxes[win][2 * g])
            d_hi = o_hi[:HEAD_DIM] + jnp.exp(sinks[2 * g + 1] - maxes[win][2 * g + 1])
            o = jnp.concatenate([o_lo[:HEAD_DIM] / d_lo, o_hi[HEAD_DIM:] / d_hi], axis=0).T.astype(o_ref.dtype)
            o_ref[win * w:(win + 1) * w, (2 * g) * LANES:(2 * g + 1) * LANES] = o[:w]
            o_ref[win * w:(win + 1) * w, (2 * g + 1) * LANES:(2 * g + 2) * LANES] = o[w:]


def _swa(roped, plain, sink_rows, *, batch, seq):
    tq = SWA_WINDOW * math.gcd(seq // SWA_WINDOW, SWA_MAX_WINDOWS_PER_STEP)
    steps = seq // tq
    return pl.pallas_call(
        _swa_kernel,
        grid=(batch, steps),
        in_specs=[
            pl.BlockSpec((tq, WIDTH), lambda b, n: (b * steps + n, QA)),
            pl.BlockSpec((seq, LANES), lambda b, n: (b, KA)),
            pl.BlockSpec((seq, LANES), lambda b, n: (b, VA)),
            pl.BlockSpec((HEADS, LANES), lambda b, n: (0, 0)),
        ],
        out_specs=pl.BlockSpec((tq, WIDTH), lambda b, n: (b * steps + n, 0)),
        out_shape=jax.ShapeDtypeStruct((batch * seq, WIDTH), BF16),
        scratch_shapes=[pltpu.VMEM((2 * SWA_KV_HEADS, seq, LANES), BF16), pltpu.VMEM((2 * SWA_KV_HEADS, LANES, seq), BF16)],
        compiler_params=_params("arbitrary", "arbitrary"),
    )(roped, roped, plain, sink_rows)


def _softplus(z):
    return jnp.maximum(z, 0.0) + jnp.log(1.0 + jnp.exp2(jnp.abs(z) * -LOG2E))


def _sb_kernel(q_ref, k_ref, v_ref, o_ref, kmask_ref, vt_ref, acc_ref):
    i = pl.program_id(1)
    t = SB_BLOCK
    nblocks = k_ref.shape[0] // t
    low = _lane() < HEAD_DIM
    mine = [low, jnp.logical_not(low)]
    chains = [(p, side) for p in range(PAIRS) for side in range(2)]

    @pl.when(i == 0)
    def _():
        for j in range(nblocks):
            rows = slice(j * t, (j + 1) * t)
            for side in range(2):
                kmask_ref[side, rows, :] = jnp.where(jnp.tile(mine[side], (1, PAIRS)), k_ref[rows, :], jnp.zeros((1, WIDTH), BF16))
                for p in range(PAIRS):
                    v = jnp.where(mine[side], v_ref[rows, p * LANES:(p + 1) * LANES].astype(F32), 0.0)
                    vt_ref[side, p * LANES:(p + 1) * LANES, rows] = v.T.astype(BF16)

    key = lax.broadcasted_iota(jnp.int32, (t, t), 0)
    query = lax.broadcasted_iota(jnp.int32, (t, t), 1)
    suffix = jnp.where(query >= key, 1.0, 0.0).astype(BF16)
    past = key < query
    qs = [q_ref[:, p * LANES:(p + 1) * LANES] * SCALE for p in range(PAIRS)]

    def block(off, size, remains, diagonal):
        zs = [_dot_nt(kmask_ref[side, pl.ds(off, size), p * LANES:(p + 1) * LANES], qs[p]) for p, side in chains]
        sps = []
        for z in zs:
            sp = _softplus(z)
            sps.append(jnp.where(past, sp, 0.0) if diagonal else sp)
        cs = [_dot(suffix[:size, :size], sp.astype(BF16)) for sp in sps]
        wgts = []
        for n, (z, c) in enumerate(zip(zs, cs)):
            logw = z - c - remains[n]
            if diagonal:
                logw = jnp.where(past, logw, NEG)
            wgts.append(jnp.exp(logw).astype(BF16))
        pvs = [_dot(vt_ref[side, p * LANES:(p + 1) * LANES, pl.ds(off, size)], wgts[n]) for n, (p, side) in enumerate(chains)]
        for p in range(PAIRS):
            both = pvs[2 * p] + pvs[2 * p + 1]
            acc_ref[p] = both if diagonal else acc_ref[p] + both
        return tuple(remains[n] + cs[n][0:1, :] for n in range(len(chains)))

    def smallest(remains):
        return jnp.min(functools.reduce(jnp.minimum, remains))

    remains = block(pl.multiple_of(i * t, t), t, tuple(jnp.zeros((1, t), F32) for _ in chains), True)

    runs = i * (t // SB_PAST)

    def cond(state):
        s, floor, _ = state
        return (s < runs) & (floor < SB_DONE)

    def body(state):
        s, _, remains = state
        remains = block(pl.multiple_of((runs - 1 - s) * SB_PAST, SB_PAST), SB_PAST, remains, False)
        return s + 1, smallest(remains), remains

    lax.while_loop(cond, body, (0, smallest(remains), remains))
    for p in range(PAIRS):
        o_ref[:, p * LANES:(p + 1) * LANES] = acc_ref[p].T.astype(o_ref.dtype)


def _stick_breaking(plain, *, batch, seq):
    t = SB_BLOCK
    nq = seq // t
    return pl.pallas_call(
        _sb_kernel,
        grid=(batch, nq),
        in_specs=[
            pl.BlockSpec((t, WIDTH), lambda b, i: (b * nq + i, QB)),
            pl.BlockSpec((seq, WIDTH), lambda b, i: (b, KB)),
            pl.BlockSpec((seq, WIDTH), lambda b, i: (b, VB)),
        ],
        out_specs=pl.BlockSpec((t, WIDTH), lambda b, i: (b * nq + i, 0)),
        out_shape=jax.ShapeDtypeStruct((batch * seq, WIDTH), BF16),
        scratch_shapes=[pltpu.VMEM((2, seq, WIDTH), BF16), pltpu.VMEM((2, WIDTH, seq), BF16),
                        pltpu.VMEM((PAIRS, LANES, t), F32)],
        compiler_params=_params("arbitrary", "arbitrary"),
    )(plain, plain, plain)


def _moba_kernel(q_ref, k_ref, v_ref, o_ref, kmean_ref, kmask_ref, vaug_ref, acc_ref, bias_ref, prob_ref, *, nblocks):
    i = pl.program_id(1)
    t = MOBA_BLOCK
    nbp = MOBA_MAX_BLOCKS
    low = _lane() < HEAD_DIM
    mine = [low, jnp.logical_not(low)]
    chains = [(p, side) for p in range(PAIRS) for side in range(2)]

    @pl.when(i == 0)
    def _():
        for p in range(PAIRS):
            rows = [jnp.mean(k_ref[j * t:(j + 1) * t, p * LANES:(p + 1) * LANES].astype(F32), axis=0, keepdims=True)
                    for j in range(nblocks)]
            rows += [jnp.zeros((1, LANES), F32)] * (nbp - nblocks)
            km = jnp.concatenate(rows, axis=0)
            kmean_ref[2 * p * nbp:(2 * p + 1) * nbp, :] = jnp.where(low, km, 0.0)
            kmean_ref[(2 * p + 1) * nbp:(2 * p + 2) * nbp, :] = jnp.where(low, 0.0, km)
        for j in range(nblocks):
            rows = slice(j * t, (j + 1) * t)
            for side in range(2):
                kmask_ref[side, rows, :] = jnp.where(jnp.tile(mine[side], (1, PAIRS)), k_ref[rows, :], jnp.zeros((1, WIDTH), BF16))
                for p in range(PAIRS):
                    v = jnp.where(mine[side], v_ref[rows, p * LANES:(p + 1) * LANES].astype(F32), 1.0)
                    vaug_ref[side, p * LANES:(p + 1) * LANES, rows] = v.T.astype(BF16)

    gates = []
    for p in range(PAIRS):
        q = q_ref[:, p * LANES:(p + 1) * LANES]
        km = kmean_ref[2 * p * nbp:(2 * p + 2) * nbp, :]
        km_hi = km.astype(BF16)
        km_lo = (km - km_hi.astype(F32)).astype(BF16)
        gates.append((_dot_nt(km_hi, q) + _dot_nt(km_lo, q)).reshape(2, nbp, t))
    blk = lax.broadcasted_iota(jnp.int32, (HEADS, nbp, t), 1)
    gate = jnp.where(blk < i, jnp.concatenate(gates, axis=0), NEG)
    count = jnp.zeros((HEADS, nbp, t), F32)
    for other in range(nblocks):
        g_other = gate[:, other:other + 1, :]
        beats = (g_other > gate) | ((g_other == gate) & (other < blk))
        count = count + jnp.where(beats, 1.0, 0.0)
    bias_ref[...] = jnp.where((count < min(MOBA_TOPK, nblocks)) & (blk < i), 0.0, NEG)
    qs = [q_ref[:, p * LANES:(p + 1) * LANES] * SCALE for p in range(PAIRS)]

    key = lax.broadcasted_iota(jnp.int32, (t, t), 0)
    query = lax.broadcasted_iota(jnp.int32, (t, t), 1)
    causal = key <= query

    def score(j):
        off = pl.multiple_of(j * t, t)
        return [_dot_nt(kmask_ref[side, pl.ds(off, t), p * LANES:(p + 1) * LANES], qs[p]) for p, side in chains]

    def weigh(scores, shift, maxes):
        new_max = [jnp.maximum(m, jnp.max(s, axis=0, keepdims=True) + b) for m, s, b in zip(maxes, scores, shift)]
        for n, (s, m, b) in enumerate(zip(scores, new_max, shift)):
            prob_ref[n] = jnp.exp(s - (m - b)).astype(BF16)
        return tuple(new_max), tuple(jnp.exp(mo - mn) for mo, mn in zip(maxes, new_max))

    def absorb(j, decay):
        off = pl.multiple_of(j * t, t)
        pvs = [_dot(vaug_ref[side, p * LANES:(p + 1) * LANES, pl.ds(off, t)], prob_ref[n]) for n, (p, side) in enumerate(chains)]
        for n, pv in enumerate(pvs):
            acc_ref[n] = acc_ref[n] * decay[n] + pv

    def step(j, carry):
        prev, maxes, decay = carry
        scores = score(j)
        absorb(prev, decay)
        maxes, decay = weigh(scores, [bias_ref[n, pl.ds(j, 1), :] for n in range(len(chains))], maxes)
        return j, maxes, decay

    acc_ref[...] = jnp.zeros(acc_ref.shape, F32)
    own = [jnp.where(causal, s, NEG) for s in score(i)]
    start = tuple(jnp.full((1, t), NEG, F32) for _ in chains)
    maxes, decay = weigh(own, [jnp.zeros((1, t), F32)] * len(chains), start)
    last, _, decay = lax.fori_loop(0, i, step, (i, maxes, decay))
    absorb(last, decay)
    accs = [acc_ref[n] for n in range(len(chains))]
    for p in range(PAIRS):
        a_lo, a_hi = accs[2 * p], accs[2 * p + 1]
        o_t = jnp.concatenate([a_lo[:HEAD_DIM] / a_lo[HEAD_DIM:], a_hi[HEAD_DIM:] / a_hi[:HEAD_DIM]], axis=0)
        o_ref[:, p * LANES:(p + 1) * LANES] = o_t.T.astype(o_ref.dtype)


def _moba(roped, plain, *, batch, seq):
    t = MOBA_BLOCK
    nq = seq // t
    assert seq % t == 0 and nq <= MOBA_MAX_BLOCKS
    return pl.pallas_call(
        functools.partial(_moba_kernel, nblocks=nq),
        grid=(batch, nq),
        in_specs=[
            pl.BlockSpec((t, WIDTH), lambda b, i: (b * nq + i, QC)),
            pl.BlockSpec((seq, WIDTH), lambda b, i: (b, KC)),
            pl.BlockSpec((seq, WIDTH), lambda b, i: (b, VC)),
        ],
        out_specs=pl.BlockSpec((t, WIDTH), lambda b, i: (b * nq + i, 0)),
        out_shape=jax.ShapeDtypeStruct((batch * seq, WIDTH), BF16),
        scratch_shapes=[pltpu.VMEM((HEADS * MOBA_MAX_BLOCKS, LANES), F32), pltpu.VMEM((2, seq, WIDTH), BF16),
                        pltpu.VMEM((2, WIDTH, seq), BF16), pltpu.VMEM((HEADS, LANES, t), F32),
                        pltpu.VMEM((HEADS, MOBA_MAX_BLOCKS, t), F32), pltpu.VMEM((HEADS, t, t), BF16)],
        compiler_params=_params("arbitrary", "arbitrary"),
    )(roped, roped, plain)


def _merge_kernel(oa_ref, ob_ref, oc_ref, gate_ref, wb_ref, wo_ref, x_ref, o_ref):
    d = x_ref.shape[1]
    ys = [_dot(branch[...], wb_ref[n].astype(BF16)) for n, branch in enumerate((oa_ref, ob_ref, oc_ref))]
    merged = None
    for n, y in enumerate(ys):
        term = y / (1.0 + jnp.exp(-gate_ref[:, n * d:(n + 1) * d].astype(F32)))
        merged = term if merged is None else merged + term
    o_ref[...] = x_ref[...] + _dot(merged.astype(BF16), wo_ref[...].astype(BF16))


def _merge(oa, ob, oc, gates, wb_layers, wo_layers, layer, x, *, tm):
    t, d = x.shape
    bw = oa.shape[1]
    row = lambda i: (i, 0)
    return pl.pallas_call(
        _merge_kernel,
        grid=(t // tm,),
        in_specs=[
            pl.BlockSpec((tm, bw), row),
            pl.BlockSpec((tm, bw), row),
            pl.BlockSpec((tm, bw), row),
            pl.BlockSpec((tm, N_BRANCHES * d), row),
            pl.BlockSpec((None, N_BRANCHES, bw, d), lambda i: (layer, 0, 0, 0), pipeline_mode=pl.Buffered(1)),
            pl.BlockSpec((None, d, d), lambda i: (layer, 0, 0), pipeline_mode=pl.Buffered(1)),
            pl.BlockSpec((tm, d), row),
        ],
        out_specs=pl.BlockSpec((tm, d), row),
        out_shape=jax.ShapeDtypeStruct((t, d), F32),
        compiler_params=_params("arbitrary"),
    )(oa, ob, oc, gates, wb_layers, wo_layers, x)


def _shift_rows(u, tail, shift):
    rolled = pltpu.roll(u, shift, axis=0)
    head = jnp.where(lax.broadcasted_iota(jnp.int32, tail.shape, 0) < shift,
                     pltpu.roll(tail, shift, axis=0), rolled[:8])
    return jnp.concatenate([head, rolled[8:]], axis=0)


def _ffn_kernel(x_ref, g_ref, wup_ref, cw_ref, cb_ref, wdn_ref, gf_ref, o_ref, tail_ref, acc_ref,
                *, tiles_per_seq, chunk, final_norm):
    i = pl.program_id(0)
    tm = x_ref.shape[0]
    dff = wdn_ref.shape[0]

    @pl.when(i % tiles_per_seq == 0)
    def _():
        tail_ref[...] = jnp.zeros(tail_ref.shape, F32)

    x = x_ref[...]
    ms = jnp.mean(x * x, axis=-1, keepdims=True)
    h = (x * lax.rsqrt(ms + RMS_EPS) * g_ref[...]).astype(BF16)

    def up(c0, c1):
        return _dot(h, wup_ref[:, c0:c1]), _dot(h, wup_ref[:, dff + c0:dff + c1])

    def conv(u, c0, c1):
        tail = tail_ref[:, c0:c1]
        tail_ref[:, c0:c1] = u[tm - 8:, :]
        out = cb_ref[:, c0:c1] + cw_ref[0:1, c0:c1] * _shift_rows(u, tail, 2)
        out = out + cw_ref[1:2, c0:c1] * _shift_rows(u, tail, 1)
        return out + cw_ref[2:3, c0:c1] * u

    bounds = [(c0, min(c0 + chunk, dff)) for c0 in range(0, dff, chunk)]
    pre = up(*bounds[0])
    for n, (c0, c1) in enumerate(bounds):
        cur = pre
        if n + 1 < len(bounds):
            pre = up(*bounds[n + 1])
        ua = conv(cur[0], c0, c1)
        uv = conv(cur[1], dff + c0, dff + c1)
        act = (ua / (1.0 + jnp.exp(-ua)) * uv).astype(BF16)
        part = _dot(act, wdn_ref[c0:c1, :])
        if n == 0:
            acc_ref[...] = x + part
        else:
            acc_ref[...] += part

    y = acc_ref[...]
    if final_norm:
        ms = jnp.mean(y * y, axis=-1, keepdims=True)
        y = y * lax.rsqrt(ms + RMS_EPS) * gf_ref[...]
    o_ref[...] = y


def _ffn(x, gain, wup_layers, cw, cb, wdn_layers, layer, gain_final, *, seq, tm, chunk, final_norm):
    t, d = x.shape
    dff = wdn_layers.shape[1]
    assert seq % tm == 0
    fixed = lambda i: (0, 0)
    picked = lambda i: (layer, 0, 0)
    once = pl.Buffered(1)
    return pl.pallas_call(
        functools.partial(_ffn_kernel, tiles_per_seq=seq // tm, chunk=chunk, final_norm=final_norm),
        grid=(t // tm,),
        in_specs=[
            pl.BlockSpec((tm, d), lambda i: (i, 0)),
            pl.BlockSpec((1, d), fixed),
            pl.BlockSpec((None, d, 2 * dff), picked, pipeline_mode=once),
            pl.BlockSpec((3, 2 * dff), fixed),
            pl.BlockSpec((1, 2 * dff), fixed),
            pl.BlockSpec((None, dff, d), picked, pipeline_mode=once),
            pl.BlockSpec((1, d), fixed),
        ],
        out_specs=pl.BlockSpec((tm, d), lambda i: (i, 0)),
        out_shape=jax.ShapeDtypeStruct((t, d), F32),
        scratch_shapes=[pltpu.VMEM((8, 2 * dff), F32), pltpu.VMEM((tm, d), F32)],
        compiler_params=_params("arbitrary"),
    )(x, gain, wup_layers, cw, cb, wdn_layers, gain_final)


def _rope_tables(seq):
    inv = ROPE_THETA ** (-jnp.arange(0, HEAD_DIM, 2, dtype=F32) / HEAD_DIM)
    ang = jnp.arange(seq, dtype=F32)[:, None] * inv[None, :]
    cos, sin = jnp.cos(ang), jnp.sin(ang)
    reps = LANES // HEAD_DIM
    return jnp.tile(cos, (1, 2 * reps)), jnp.tile(jnp.concatenate([-sin, sin], axis=1), (1, reps))


def kernel(x, norm_mix, w_in, b_gate, sinks, w_branch, w_out, norm_ffn, w_up, conv_w, conv_b, w_down, norm_final):
    batch, seq, d = x.shape
    depth = w_in.shape[0]
    tokens = batch * seq
    cos, sin_signed = _rope_tables(seq)
    xt = x.reshape(tokens, d)
    tm_tok, tm_ffn = min(512, seq), min(1024, seq)
    wup, wdn = w_up.astype(BF16), w_down.astype(BF16)
    for layer in range(depth):
        roped, plain, gates = _in_project(xt, norm_mix[layer].reshape(1, d), w_in, layer,
                                          cos, sin_signed, b_gate[layer].reshape(1, N_BRANCHES * d),
                                          seq=seq, tm=tm_tok, chunk=512)
        sink_rows = jnp.broadcast_to(sinks[layer].astype(F32)[:, None], (HEADS, LANES))
        o_a = _swa(roped, plain, sink_rows, batch=batch, seq=seq)
        o_b = _stick_breaking(plain, batch=batch, seq=seq)
        o_c = _moba(roped, plain, batch=batch, seq=seq)
        xt = _merge(o_a, o_b, o_c, gates, w_branch, w_out, layer, xt, tm=tm_tok)
        xt = _ffn(xt, norm_ffn[layer].reshape(1, d), wup, conv_w[layer],
                  conv_b[layer].reshape(1, -1), wdn, layer, norm_final.reshape(1, d),
                  seq=seq, tm=tm_ffn, chunk=1536, final_norm=(layer == depth - 1))
    return xt.reshape(batch, seq, d)
```

```python
import functools
import math

import jax
import jax.numpy as jnp
from jax import lax
from jax.experimental import pallas as pl
from jax.experimental.pallas import tpu as pltpu

F32 = jnp.float32
BF16 = jnp.bfloat16

HEAD_DIM = 64
LANES = 128
HEADS = 8
PAIRS = HEADS // 2
WIDTH = HEADS * HEAD_DIM
SWA_KV_HEADS = 2
SWA_WINDOW = 128
SWA_MAX_WINDOWS_PER_STEP = 16
SB_BLOCK = 256
SB_PAST = 128
SB_DONE = 88.0
MOBA_BLOCK = 256
MOBA_TOPK = 3
MOBA_MAX_BLOCKS = 8
N_BRANCHES = 3
ROPE_THETA = 10000.0
RMS_EPS = 1e-6
SCALE = HEAD_DIM ** -0.5
LOG2E = 1.4426950408889634
NEG = -1e30

VMEM_LIMIT = 56 * 1024 * 1024

ROPE_COLS = 3 * WIDTH + LANES
PLAIN_COLS = 4 * WIDTH + LANES
_KV = SWA_KV_HEADS * HEAD_DIM
_B0, _C0 = WIDTH + 2 * _KV, WIDTH + 2 * _KV + 3 * WIDTH
ROPE_SRC = ((0, WIDTH), (_C0, 2 * WIDTH), (WIDTH, _KV))
PLAIN_SRC = ((_B0, 3 * WIDTH), (_C0 + 2 * WIDTH, WIDTH), (WIDTH + _KV, _KV))
GATE_START = _C0 + 3 * WIDTH
QA, QC, KC = range(3)
KA = 3 * PAIRS
QB, KB, VB, VC = range(4)
VA = 4 * PAIRS


def _params(*sem):
    return pltpu.CompilerParams(dimension_semantics=sem, vmem_limit_bytes=VMEM_LIMIT)


def _dot(a, b):
    return jnp.dot(a, b, preferred_element_type=F32)


def _dot_nt(a, b):
    return lax.dot_general(a, b, (((1,), (1,)), ((), ())), preferred_element_type=F32)


def _lane():
    return lax.broadcasted_iota(jnp.int32, (1, LANES), 1)


def _rope(piece, cos, sin_signed):
    first_half = (_lane() & 32) == 0
    nxt = pltpu.roll(piece, LANES - 32, axis=1)
    prv = pltpu.roll(piece, 32, axis=1)
    return piece * cos + jnp.where(first_half, nxt, prv) * sin_signed


def _in_proj_kernel(x_ref, g_ref, w_ref, cos_ref, sin_ref, bias_ref, rope_ref, plain_ref, gate_ref, *, chunk):
    x = x_ref[...]
    ms = jnp.mean(x * x, axis=-1, keepdims=True)
    h = (x * lax.rsqrt(ms + RMS_EPS) * g_ref[...]).astype(BF16)
    cos, sin_signed = cos_ref[...], sin_ref[...]
    for out_ref, pieces in ((rope_ref, ROPE_SRC), (plain_ref, PLAIN_SRC), (gate_ref, ((GATE_START, gate_ref.shape[1]),))):
        dst = 0
        for src, width in pieces:
            for c0 in range(0, width, chunk):
                n = min(chunk, width - c0)
                acc = _dot(h, w_ref[:, src + c0:src + c0 + n].astype(BF16))
                lo = dst + c0
                if out_ref is rope_ref:
                    for c in range(0, n, LANES):
                        out_ref[:, lo + c:lo + c + LANES] = _rope(acc[:, c:c + LANES], cos, sin_signed).astype(out_ref.dtype)
                elif out_ref is gate_ref:
                    out_ref[:, lo:lo + n] = (acc + bias_ref[:, lo:lo + n]).astype(out_ref.dtype)
                else:
                    out_ref[:, lo:lo + n] = acc.astype(out_ref.dtype)
            dst += width


def _in_project(x, gain, w_layers, layer, cos, sin_signed, gate_bias, *, seq, tm, chunk):
    t, d = x.shape
    n_in = w_layers.shape[2]
    n_gate = n_in - ROPE_COLS - PLAIN_COLS
    assert t % tm == 0 and seq % tm == 0
    pos_tiles = seq // tm
    row = lambda i: (i, 0)
    return pl.pallas_call(
        functools.partial(_in_proj_kernel, chunk=chunk),
        grid=(t // tm,),
        in_specs=[
            pl.BlockSpec((tm, d), row),
            pl.BlockSpec((1, d), lambda i: (0, 0)),
            pl.BlockSpec((None, d, n_in), lambda i: (layer, 0, 0), pipeline_mode=pl.Buffered(1)),
            pl.BlockSpec((tm, LANES), lambda i: (i % pos_tiles, 0)),
            pl.BlockSpec((tm, LANES), lambda i: (i % pos_tiles, 0)),
            pl.BlockSpec((1, n_gate), lambda i: (0, 0)),
        ],
        out_specs=[pl.BlockSpec((tm, ROPE_COLS), row), pl.BlockSpec((tm, PLAIN_COLS), row),
                   pl.BlockSpec((tm, n_gate), row)],
        out_shape=[jax.ShapeDtypeStruct((t, ROPE_COLS), BF16), jax.ShapeDtypeStruct((t, PLAIN_COLS), BF16),
                   jax.ShapeDtypeStruct((t, n_gate), BF16)],
        compiler_params=_params("arbitrary"),
    )(x, gain, w_layers, cos, sin_signed, gate_bias)


def _swa_kernel(q_ref, k_ref, v_ref, sink_ref, o_ref, kat_ref, vt_ref):
    n = pl.program_id(1)
    w = SWA_WINDOW
    windows = q_ref.shape[0] // w
    low = _lane() < HEAD_DIM
    chains = [(g, side) for g in range(SWA_KV_HEADS) for side in range(2)]

    @pl.when(n == 0)
    def _():
        k = k_ref[...].astype(F32)
        v = v_ref[...].astype(F32)
        k_sw = pltpu.roll(k, HEAD_DIM, axis=1)
        v_sw = pltpu.roll(v, HEAD_DIM, axis=1)
        k_at = [jnp.where(low, k, 0.0), jnp.where(low, 0.0, k_sw), jnp.where(low, k_sw, 0.0), jnp.where(low, 0.0, k)]
        v_at = [jnp.where(low, v, 1.0), jnp.where(low, 1.0, v_sw), jnp.where(low, v_sw, 1.0), jnp.where(low, 1.0, v)]
        for c in range(len(chains)):
            kat_ref[c] = k_at[c].astype(BF16)
            vt_ref[c] = v_at[c].T.astype(BF16)

    key_i = lax.broadcasted_iota(jnp.int32, (2 * w, 2 * w), 0)
    query_i = lax.broadcasted_iota(jnp.int32, (2 * w, 2 * w), 1) & (w - 1)
    sinks = []
    for g, side in chains:
        h0 = 4 * g + side
        sinks.append(jnp.concatenate([sink_ref[h0:h0 + 1, :], sink_ref[h0 + 2:h0 + 3, :]], axis=1))
    starts, scores = [], []
    for win in range(windows):
        blk = n * windows + win
        start = pl.multiple_of(jnp.maximum(blk - 1, 0) * w, w)
        diff = (blk * w + query_i) - (start + key_i)
        valid = (diff >= 0) & (diff < w)
        rows = slice(win * w, (win + 1) * w)
        qs = [jnp.concatenate([q_ref[rows, (2 * g) * LANES:(2 * g + 1) * LANES],
                               q_ref[rows, (2 * g + 1) * LANES:(2 * g + 2) * LANES]], axis=0) * SCALE
              for g in range(SWA_KV_HEADS)]
        starts.append(start)
        scores.append([jnp.where(valid, _dot_nt(kat_ref[c, pl.ds(start, 2 * w), :], qs[g]), NEG)
                       for c, (g, side) in enumerate(chains)])
    maxes = [[jnp.maximum(jnp.max(s, axis=0, keepdims=True), sink) for s, sink in zip(win_scores, sinks)]
             for win_scores in scores]
    probs = [[jnp.exp(s - m).astype(BF16) for s, m in zip(win_scores, win_maxes)]
             for win_scores, win_maxes in zip(scores, maxes)]
    outs = [[_dot(vt_ref[c, :, pl.ds(start, 2 * w)], p) for c, p in enumerate(win_probs)]
            for start, win_probs in zip(starts, probs)]
    for win in range(windows):
        for g in range(SWA_KV_HEADS):
            o_lo, o_hi = outs[win][2 * g], outs[win][2 * g + 1]
            d_lo = o_lo[HEAD_DIM:] + jnp.exp(sinks[2 * g] - maxes[win][2 * g])
            d_hi = o_hi[:HEAD_DIM] + jnp.exp(sinks[2 * g + 1] - maxes[win][2 * g + 1])
            o = jnp.concatenate([o_lo[:HEAD_DIM] / d_lo, o_hi[HEAD_DIM:] / d_hi], axis=0).T.astype(o_ref.dtype)
            o_ref[win * w:(win + 1) * w, (2 * g) * LANES:(2 * g + 1) * LANES] = o[:w]
            o_ref[win * w:(win + 1) * w, (2 * g + 1) * LANES:(2 * g + 2) * LANES] = o[w:]


def _swa(roped, plain, sink_rows, *, batch, seq):
    tq = SWA_WINDOW * math.gcd(seq // SWA_WINDOW, SWA_MAX_WINDOWS_PER_STEP)
    steps = seq // tq
    return pl.pallas_call(
        _swa_kernel,
        grid=(batch, steps),
        in_specs=[
            pl.BlockSpec((tq, WIDTH), lambda b, n: (b * steps + n, QA)),
            pl.BlockSpec((seq, LANES), lambda b, n: (b, KA)),
            pl.BlockSpec((seq, LANES), lambda b, n: (b, VA)),
            pl.BlockSpec((HEADS, LANES), lambda b, n: (0, 0)),
        ],
        out_specs=pl.BlockSpec((tq, WIDTH), lambda b, n: (b * steps + n, 0)),
        out_shape=jax.ShapeDtypeStruct((batch * seq, WIDTH), BF16),
        scratch_shapes=[pltpu.VMEM((2 * SWA_KV_HEADS, seq, LANES), BF16), pltpu.VMEM((2 * SWA_KV_HEADS, LANES, seq), BF16)],
        compiler_params=_params("arbitrary", "arbitrary"),
    )(roped, roped, plain, sink_rows)


def _softplus(z):
    return jnp.maximum(z, 0.0) + jnp.log(1.0 + jnp.exp2(jnp.abs(z) * -LOG2E))


def _sb_kernel(q_ref, k_ref, v_ref, o_ref, kmask_ref, vt_ref, acc_ref):
    i = pl.program_id(1)
    t = SB_BLOCK
    nblocks = k_ref.shape[0] // t
    low = _lane() < HEAD_DIM
    mine = [low, jnp.logical_not(low)]
    chains = [(p, side) for p in range(PAIRS) for side in range(2)]

    @pl.when(i == 0)
    def _():
        dim_low = lax.broadcasted_iota(jnp.int32, (LANES, t), 0) < HEAD_DIM
        for j in range(nblocks):
            rows = slice(j * t, (j + 1) * t)
            for side in range(2):
                kmask_ref[side, rows, :] = jnp.where(jnp.tile(mine[side], (1, PAIRS)), k_ref[rows, :], jnp.zeros((1, WIDTH), BF16))
            for p in range(PAIRS):
                v_t = v_ref[rows, p * LANES:(p + 1) * LANES].astype(F32).T
                vt_ref[0, p * LANES:(p + 1) * LANES, rows] = jnp.where(dim_low, v_t, 0.0).astype(BF16)
                vt_ref[1, p * LANES:(p + 1) * LANES, rows] = jnp.where(dim_low, 0.0, v_t).astype(BF16)

    key = lax.broadcasted_iota(jnp.int32, (t, t), 0)
    query = lax.broadcasted_iota(jnp.int32, (t, t), 1)
    suffix = jnp.where(query >= key, 1.0, 0.0).astype(BF16)
    past = key < query
    qs = [q_ref[:, p * LANES:(p + 1) * LANES] * SCALE for p in range(PAIRS)]

    def block(off, size, remains, diagonal):
        zs = [_dot_nt(kmask_ref[side, pl.ds(off, size), p * LANES:(p + 1) * LANES], qs[p]) for p, side in chains]
        sps = []
        for z in zs:
            sp = _softplus(z)
            sps.append(jnp.where(past, sp, 0.0) if diagonal else sp)
        cs = [_dot(suffix[:size, :size], sp.astype(BF16)) for sp in sps]
        wgts = []
        for n, (z, c) in enumerate(zip(zs, cs)):
            logw = z - c - remains[n]
            if diagonal:
                logw = jnp.where(past, logw, NEG)
            wgts.append(jnp.exp(logw).astype(BF16))
        pvs = [_dot(vt_ref[side, p * LANES:(p + 1) * LANES, pl.ds(off, size)], wgts[n]) for n, (p, side) in enumerate(chains)]
        for p in range(PAIRS):
            both = pvs[2 * p] + pvs[2 * p + 1]
            acc_ref[p] = both if diagonal else acc_ref[p] + both
        return tuple(remains[n] + cs[n][0:1, :] for n in range(len(chains)))

    def smallest(remains):
        return jnp.min(functools.reduce(jnp.minimum, remains))

    remains = block(pl.multiple_of(i * t, t), t, tuple(jnp.zeros((1, t), F32) for _ in chains), True)

    runs = i * (t // SB_PAST)

    def cond(state):
        s, floor, _ = state
        return (s < runs) & (floor < SB_DONE)

    def body(state):
        s, _, remains = state
        remains = block(pl.multiple_of((runs - 1 - s) * SB_PAST, SB_PAST), SB_PAST, remains, False)
        return s + 1, smallest(remains), remains

    lax.while_loop(cond, body, (0, smallest(remains), remains))
    for p in range(PAIRS):
        o_ref[:, p * LANES:(p + 1) * LANES] = acc_ref[p].T.astype(o_ref.dtype)


def _stick_breaking(plain, *, batch, seq):
    t = SB_BLOCK
    nq = seq // t
    return pl.pallas_call(
        _sb_kernel,
        grid=(batch, nq),
        in_specs=[
            pl.BlockSpec((t, WIDTH), lambda b, i: (b * nq + i, QB)),
            pl.BlockSpec((seq, WIDTH), lambda b, i: (b, KB)),
            pl.BlockSpec((seq, WIDTH), lambda b, i: (b, VB)),
        ],
        out_specs=pl.BlockSpec((t, WIDTH), lambda b, i: (b * nq + i, 0)),
        out_shape=jax.ShapeDtypeStruct((batch * seq, WIDTH), BF16),
        scratch_shapes=[pltpu.VMEM((2, seq, WIDTH), BF16), pltpu.VMEM((2, WIDTH, seq), BF16),
                        pltpu.VMEM((PAIRS, LANES, t), F32)],
        compiler_params=_params("arbitrary", "arbitrary"),
    )(plain, plain, plain)


def _moba_kernel(q_ref, k_ref, v_ref, o_ref, kmean_ref, kmask_ref, vaug_ref, acc_ref, bias_ref, prob_ref, *, nblocks):
    i = pl.program_id(1)
    t = MOBA_BLOCK
    nbp = MOBA_MAX_BLOCKS
    low = _lane() < HEAD_DIM
    mine = [low, jnp.logical_not(low)]
    chains = [(p, side) for p in range(PAIRS) for side in range(2)]

    @pl.when(i == 0)
    def _():
        for p in range(PAIRS):
            rows = [jnp.mean(k_ref[j * t:(j + 1) * t, p * LANES:(p + 1) * LANES].astype(F32), axis=0, keepdims=True)
                    for j in range(nblocks)]
            rows += [jnp.zeros((1, LANES), F32)] * (nbp - nblocks)
            km = jnp.concatenate(rows, axis=0)
            kmean_ref[2 * p * nbp:(2 * p + 1) * nbp, :] = jnp.where(low, km, 0.0)
            kmean_ref[(2 * p + 1) * nbp:(2 * p + 2) * nbp, :] = jnp.where(low, 0.0, km)
        dim_low = lax.broadcasted_iota(jnp.int32, (LANES, t), 0) < HEAD_DIM
        for j in range(nblocks):
            rows = slice(j * t, (j + 1) * t)
            for side in range(2):
                kmask_ref[side, rows, :] = jnp.where(jnp.tile(mine[side], (1, PAIRS)), k_ref[rows, :], jnp.zeros((1, WIDTH), BF16))
            for p in range(PAIRS):
                v_t = v_ref[rows, p * LANES:(p + 1) * LANES].astype(F32).T
                vaug_ref[0, p * LANES:(p + 1) * LANES, rows] = jnp.where(dim_low, v_t, 1.0).astype(BF16)
                vaug_ref[1, p * LANES:(p + 1) * LANES, rows] = jnp.where(dim_low, 1.0, v_t).astype(BF16)

    gates = []
    for p in range(PAIRS):
        q = q_ref[:, p * LANES:(p + 1) * LANES]
        km = kmean_ref[2 * p * nbp:(2 * p + 2) * nbp, :]
        km_hi = km.astype(BF16)
        km_lo = (km - km_hi.astype(F32)).astype(BF16)
        gates.append((_dot_nt(km_hi, q) + _dot_nt(km_lo, q)).reshape(2, nbp, t))
    blk = lax.broadcasted_iota(jnp.int32, (HEADS, nbp, t), 1)
    gate = jnp.where(blk < i, jnp.concatenate(gates, axis=0), NEG)
    count = jnp.zeros((HEADS, nbp, t), F32)
    for other in range(nblocks):
        g_other = gate[:, other:other + 1, :]
        beats = (g_other > gate) | ((g_other == gate) & (other < blk))
        count = count + jnp.where(beats, 1.0, 0.0)
    bias_ref[...] = jnp.where((count < min(MOBA_TOPK, nblocks)) & (blk < i), 0.0, NEG)
    qs = [q_ref[:, p * LANES:(p + 1) * LANES] * SCALE for p in range(PAIRS)]

    key = lax.broadcasted_iota(jnp.int32, (t, t), 0)
    query = lax.broadcasted_iota(jnp.int32, (t, t), 1)
    causal = key <= query

    def score(j):
        off = pl.multiple_of(j * t, t)
        return [_dot_nt(kmask_ref[side, pl.ds(off, t), p * LANES:(p + 1) * LANES], qs[p]) for p, side in chains]

    def weigh(scores, shift, maxes):
        new_max = [jnp.maximum(m, jnp.max(s, axis=0, keepdims=True) + b) for m, s, b in zip(maxes, scores, shift)]
        for n, (s, m, b) in enumerate(zip(scores, new_max, shift)):
            prob_ref[n] = jnp.exp(s - (m - b)).astype(BF16)
        return tuple(new_max), tuple(jnp.exp(mo - mn) for mo, mn in zip(maxes, new_max))

    def absorb(j, decay):
        off = pl.multiple_of(j * t, t)
        pvs = [_dot(vaug_ref[side, p * LANES:(p + 1) * LANES, pl.ds(off, t)], prob_ref[n]) for n, (p, side) in enumerate(chains)]
        for n, pv in enumerate(pvs):
            acc_ref[n] = acc_ref[n] * decay[n] + pv

    def step(j, carry):
        prev, maxes, decay = carry
        scores = score(j)
        absorb(prev, decay)
        maxes, decay = weigh(scores, [bias_ref[n, pl.ds(j, 1), :] for n in range(len(chains))], maxes)
        return j, maxes, decay

    acc_ref[...] = jnp.zeros(acc_ref.shape, F32)
    own = [jnp.where(causal, s, NEG) for s in score(i)]
    start = tuple(jnp.full((1, t), NEG, F32) for _ in chains)
    maxes, decay = weigh(own, [jnp.zeros((1, t), F32)] * len(chains), start)
    last, _, decay = lax.fori_loop(0, i, step, (i, maxes, decay))
    absorb(last, decay)
    accs = [acc_ref[n] for n in range(len(chains))]
    for p in range(PAIRS):
        a_lo, a_hi = accs[2 * p], accs[2 * p + 1]
        o_t = jnp.concatenate([a_lo[:HEAD_DIM] / a_lo[HEAD_DIM:], a_hi[HEAD_DIM:] / a_hi[:HEAD_DIM]], axis=0)
        o_ref[:, p * LANES:(p + 1) * LANES] = o_t.T.astype(o_ref.dtype)


def _moba(roped, plain, *, batch, seq):
    t = MOBA_BLOCK
    nq = seq // t
    assert seq % t == 0 and nq <= MOBA_MAX_BLOCKS
    return pl.pallas_call(
        functools.partial(_moba_kernel, nblocks=nq),
        grid=(batch, nq),
        in_specs=[
            pl.BlockSpec((t, WIDTH), lambda b, i: (b * nq + i, QC)),
            pl.BlockSpec((seq, WIDTH), lambda b, i: (b, KC)),
            pl.BlockSpec((seq, WIDTH), lambda b, i: (b, VC)),
        ],
        out_specs=pl.BlockSpec((t, WIDTH), lambda b, i: (b * nq + i, 0)),
        out_shape=jax.ShapeDtypeStruct((batch * seq, WIDTH), BF16),
        scratch_shapes=[pltpu.VMEM((HEADS * MOBA_MAX_BLOCKS, LANES), F32), pltpu.VMEM((2, seq, WIDTH), BF16),
                        pltpu.VMEM((2, WIDTH, seq), BF16), pltpu.VMEM((HEADS, LANES, t), F32),
                        pltpu.VMEM((HEADS, MOBA_MAX_BLOCKS, t), F32), pltpu.VMEM((HEADS, t, t), BF16)],
        compiler_params=_params("arbitrary", "arbitrary"),
    )(roped, roped, plain)


def _merge_kernel(oa_ref, ob_ref, oc_ref, gate_ref, wb_ref, wo_ref, x_ref, o_ref):
    d = x_ref.shape[1]
    ys = [_dot(branch[...], wb_ref[n].astype(BF16)) for n, branch in enumerate((oa_ref, ob_ref, oc_ref))]
    merged = None
    for n, y in enumerate(ys):
        term = y / (1.0 + jnp.exp(-gate_ref[:, n * d:(n + 1) * d].astype(F32)))
        merged = term if merged is None else merged + term
    o_ref[...] = x_ref[...] + _dot(merged.astype(BF16), wo_ref[...].astype(BF16))


def _merge(oa, ob, oc, gates, wb_layers, wo_layers, layer, x, *, tm):
    t, d = x.shape
    bw = oa.shape[1]
    row = lambda i: (i, 0)
    return pl.pallas_call(
        _merge_kernel,
        grid=(t // tm,),
        in_specs=[
            pl.BlockSpec((tm, bw), row),
            pl.BlockSpec((tm, bw), row),
            pl.BlockSpec((tm, bw), row),
            pl.BlockSpec((tm, N_BRANCHES * d), row),
            pl.BlockSpec((None, N_BRANCHES, bw, d), lambda i: (layer, 0, 0, 0), pipeline_mode=pl.Buffered(1)),
            pl.BlockSpec((None, d, d), lambda i: (layer, 0, 0), pipeline_mode=pl.Buffered(1)),
            pl.BlockSpec((tm, d), row),
        ],
        out_specs=pl.BlockSpec((tm, d), row),
        out_shape=jax.ShapeDtypeStruct((t, d), F32),
        compiler_params=_params("arbitrary"),
    )(oa, ob, oc, gates, wb_layers, wo_layers, x)


def _shift_rows(u, tail, shift):
    rolled = pltpu.roll(u, shift, axis=0)
    head = jnp.where(lax.broadcasted_iota(jnp.int32, tail.shape, 0) < shift,
                     pltpu.roll(tail, shift, axis=0), rolled[:8])
    return jnp.concatenate([head, rolled[8:]], axis=0)


def _ffn_kernel(x_ref, g_ref, wup_ref, cw_ref, cb_ref, wdn_ref, gf_ref, o_ref, tail_ref, acc_ref,
                *, tiles_per_seq, chunk, final_norm):
    i = pl.program_id(0)
    tm = x_ref.shape[0]
    dff = wdn_ref.shape[0]

    @pl.when(i % tiles_per_seq == 0)
    def _():
        tail_ref[...] = jnp.zeros(tail_ref.shape, F32)

    x = x_ref[...]
    ms = jnp.mean(x * x, axis=-1, keepdims=True)
    h = (x * lax.rsqrt(ms + RMS_EPS) * g_ref[...]).astype(BF16)

    def up(c0, c1):
        return _dot(h, wup_ref[:, c0:c1]), _dot(h, wup_ref[:, dff + c0:dff + c1])

    def conv(u, c0, c1):
        tail = tail_ref[:, c0:c1]
        tail_ref[:, c0:c1] = u[tm - 8:, :]
        out = cb_ref[:, c0:c1] + cw_ref[0:1, c0:c1] * _shift_rows(u, tail, 2)
        out = out + cw_ref[1:2, c0:c1] * _shift_rows(u, tail, 1)
        return out + cw_ref[2:3, c0:c1] * u

    bounds = [(c0, min(c0 + chunk, dff)) for c0 in range(0, dff, chunk)]
    pre = up(*bounds[0])
    for n, (c0, c1) in enumerate(bounds):
        cur = pre
        if n + 1 < len(bounds):
            pre = up(*bounds[n + 1])
        ua = conv(cur[0], c0, c1)
        uv = conv(cur[1], dff + c0, dff + c1)
        act = (ua / (1.0 + jnp.exp(-ua)) * uv).astype(BF16)
        part = _dot(act, wdn_ref[c0:c1, :])
        if n == 0:
            acc_ref[...] = x + part
        else:
            acc_ref[...] += part

    y = acc_ref[...]
    if final_norm:
        ms = jnp.mean(y * y, axis=-1, keepdims=True)
        y = y * lax.rsqrt(ms + RMS_EPS) * gf_ref[...]
    o_ref[...] = y


def _ffn(x, gain, wup_layers, cw, cb, wdn_layers, layer, gain_final, *, seq, tm, chunk, final_norm):
    t, d = x.shape
    dff = wdn_layers.shape[1]
    assert seq % tm == 0
    fixed = lambda i: (0, 0)
    picked = lambda i: (layer, 0, 0)
    once = pl.Buffered(1)
    return pl.pallas_call(
        functools.partial(_ffn_kernel, tiles_per_seq=seq // tm, chunk=chunk, final_norm=final_norm),
        grid=(t // tm,),
        in_specs=[
            pl.BlockSpec((tm, d), lambda i: (i, 0)),
            pl.BlockSpec((1, d), fixed),
            pl.BlockSpec((None, d, 2 * dff), picked, pipeline_mode=once),
            pl.BlockSpec((3, 2 * dff), fixed),
            pl.BlockSpec((1, 2 * dff), fixed),
            pl.BlockSpec((None, dff, d), picked, pipeline_mode=once),
            pl.BlockSpec((1, d), fixed),
        ],
        out_specs=pl.BlockSpec((tm, d), lambda i: (i, 0)),
        out_shape=jax.ShapeDtypeStruct((t, d), F32),
        scratch_shapes=[pltpu.VMEM((8, 2 * dff), F32), pltpu.VMEM((tm, d), F32)],
        compiler_params=_params("arbitrary"),
    )(x, gain, wup_layers, cw, cb, wdn_layers, gain_final)


def _rope_tables(seq):
    inv = ROPE_THETA ** (-jnp.arange(0, HEAD_DIM, 2, dtype=F32) / HEAD_DIM)
    ang = jnp.arange(seq, dtype=F32)[:, None] * inv[None, :]
    cos, sin = jnp.cos(ang), jnp.sin(ang)
    reps = LANES // HEAD_DIM
    return jnp.tile(cos, (1, 2 * reps)), jnp.tile(jnp.concatenate([-sin, sin], axis=1), (1, reps))


def kernel(x, norm_mix, w_in, b_gate, sinks, w_branch, w_out, norm_ffn, w_up, conv_w, conv_b, w_down, norm_final):
    batch, seq, d = x.shape
    depth = w_in.shape[0]
    tokens = batch * seq
    cos, sin_signed = _rope_tables(seq)
    xt = x.reshape(tokens, d)
    tm_tok, tm_ffn = min(512, seq), min(1024, seq)
    wup, wdn = w_up.astype(BF16), w_down.astype(BF16)
    for layer in range(depth):
        roped, plain, gates = _in_project(xt, norm_mix[layer].reshape(1, d), w_in, layer,
                                          cos, sin_signed, b_gate[layer].reshape(1, N_BRANCHES * d),
                                          seq=seq, tm=tm_tok, chunk=512)
        sink_rows = jnp.broadcast_to(sinks[layer].astype(F32)[:, None], (HEADS, LANES))
        o_a = _swa(roped, plain, sink_rows, batch=batch, seq=seq)
        o_b = _stick_breaking(plain, batch=batch, seq=seq)
        o_c = _moba(roped, plain, batch=batch, seq=seq)
        xt = _merge(o_a, o_b, o_c, gates, w_branch, w_out, layer, xt, tm=tm_tok)
        xt = _ffn(xt, norm_ffn[layer].reshape(1, d), wup, conv_w[layer],
                  conv_b[layer].reshape(1, -1), wdn, layer, norm_final.reshape(1, d),
                  seq=seq, tm=tm_ffn, chunk=1536, final_norm=(layer == depth - 1))
    return xt.reshape(batch, seq, d)
```

```python
import functools
import math

import jax
import jax.numpy as jnp
from jax import lax
from jax.experimental import pallas as pl
from jax.experimental.pallas import tpu as pltpu

F32 = jnp.float32
BF16 = jnp.bfloat16

HEAD_DIM = 64
LANES = 128
HEADS = 8
PAIRS = HEADS // 2
WIDTH = HEADS * HEAD_DIM
SWA_KV_HEADS = 2
SWA_WINDOW = 128
SWA_MAX_WINDOWS_PER_STEP = 16
SB_BLOCK = 256
SB_PAST = 128
SB_DONE = 88.0
MOBA_BLOCK = 256
MOBA_TOPK = 3
MOBA_MAX_BLOCKS = 8
N_BRANCHES = 3
ROPE_THETA = 10000.0
RMS_EPS = 1e-6
SCALE = HEAD_DIM ** -0.5
LOG2E = 1.4426950408889634
NEG = -1e30

VMEM_LIMIT = 56 * 1024 * 1024

ROPE_COLS = 3 * WIDTH + LANES
PLAIN_COLS = 4 * WIDTH + LANES
_KV = SWA_KV_HEADS * HEAD_DIM
_B0, _C0 = WIDTH + 2 * _KV, WIDTH + 2 * _KV + 3 * WIDTH
ROPE_SRC = ((0, WIDTH), (_C0, 2 * WIDTH), (WIDTH, _KV))
PLAIN_SRC = ((_B0, 3 * WIDTH), (_C0 + 2 * WIDTH, WIDTH), (WIDTH + _KV, _KV))
GATE_START = _C0 + 3 * WIDTH
QA, QC, KC = range(3)
KA = 3 * PAIRS
QB, KB, VB, VC = range(4)
VA = 4 * PAIRS


def _params(*sem):
    return pltpu.CompilerParams(dimension_semantics=sem, vmem_limit_bytes=VMEM_LIMIT)


def _dot(a, b):
    return jnp.dot(a, b, preferred_element_type=F32)


def _dot_nt(a, b):
    return lax.dot_general(a, b, (((1,), (1,)), ((), ())), preferred_element_type=F32)


def _lane():
    return lax.broadcasted_iota(jnp.int32, (1, LANES), 1)


def _rope(piece, cos, sin_signed):
    first_half = (_lane() & 32) == 0
    nxt = pltpu.roll(piece, LANES - 32, axis=1)
    prv = pltpu.roll(piece, 32, axis=1)
    return piece * cos + jnp.where(first_half, nxt, prv) * sin_signed


def _in_proj_kernel(x_ref, g_ref, w_ref, cos_ref, sin_ref, bias_ref, rope_ref, plain_ref, gate_ref, *, chunk):
    x = x_ref[...]
    ms = jnp.mean(x * x, axis=-1, keepdims=True)
    h = (x * lax.rsqrt(ms + RMS_EPS) * g_ref[...]).astype(BF16)
    cos, sin_signed = cos_ref[...], sin_ref[...]
    for out_ref, pieces in ((rope_ref, ROPE_SRC), (plain_ref, PLAIN_SRC), (gate_ref, ((GATE_START, gate_ref.shape[1]),))):
        dst = 0
        for src, width in pieces:
            for c0 in range(0, width, chunk):
                n = min(chunk, width - c0)
                acc = _dot(h, w_ref[:, src + c0:src + c0 + n].astype(BF16))
                lo = dst + c0
                if out_ref is rope_ref:
                    for c in range(0, n, LANES):
                        out_ref[:, lo + c:lo + c + LANES] = _rope(acc[:, c:c + LANES], cos, sin_signed).astype(out_ref.dtype)
                elif out_ref is gate_ref:
                    out_ref[:, lo:lo + n] = (acc + bias_ref[:, lo:lo + n]).astype(out_ref.dtype)
                else:
                    out_ref[:, lo:lo + n] = acc.astype(out_ref.dtype)
            dst += width


def _in_project(x, gain, w_layers, layer, cos, sin_signed, gate_bias, *, seq, tm, chunk):
    t, d = x.shape
    n_in = w_layers.shape[2]
    n_gate = n_in - ROPE_COLS - PLAIN_COLS
    assert t % tm == 0 and seq % tm == 0
    pos_tiles = seq // tm
    row = lambda i: (i, 0)
    return pl.pallas_call(
        functools.partial(_in_proj_kernel, chunk=chunk),
        grid=(t // tm,),
        in_specs=[
            pl.BlockSpec((tm, d), row),
            pl.BlockSpec((1, d), lambda i: (0, 0)),
            pl.BlockSpec((None, d, n_in), lambda i: (layer, 0, 0), pipeline_mode=pl.Buffered(1)),
            pl.BlockSpec((tm, LANES), lambda i: (i % pos_tiles, 0)),
            pl.BlockSpec((tm, LANES), lambda i: (i % pos_tiles, 0)),
            pl.BlockSpec((1, n_gate), lambda i: (0, 0)),
        ],
        out_specs=[pl.BlockSpec((tm, ROPE_COLS), row), pl.BlockSpec((tm, PLAIN_COLS), row),
                   pl.BlockSpec((tm, n_gate), row)],
        out_shape=[jax.ShapeDtypeStruct((t, ROPE_COLS), BF16), jax.ShapeDtypeStruct((t, PLAIN_COLS), BF16),
                   jax.ShapeDtypeStruct((t, n_gate), BF16)],
        compiler_params=_params("arbitrary"),
    )(x, gain, w_layers, cos, sin_signed, gate_bias)


def _swa_kernel(q_ref, k_ref, v_ref, sink_ref, o_ref, kat_ref, vt_ref):
    n = pl.program_id(1)
    w = SWA_WINDOW
    windows = q_ref.shape[0] // w
    low = _lane() < HEAD_DIM
    chains = [(g, side) for g in range(SWA_KV_HEADS) for side in range(2)]

    @pl.when(n == 0)
    def _():
        k = k_ref[...].astype(F32)
        v = v_ref[...].astype(F32)
        k_sw = pltpu.roll(k, HEAD_DIM, axis=1)
        v_sw = pltpu.roll(v, HEAD_DIM, axis=1)
        k_at = [jnp.where(low, k, 0.0), jnp.where(low, 0.0, k_sw), jnp.where(low, k_sw, 0.0), jnp.where(low, 0.0, k)]
        v_at = [jnp.where(low, v, 1.0), jnp.where(low, 1.0, v_sw), jnp.where(low, v_sw, 1.0), jnp.where(low, 1.0, v)]
        for c in range(len(chains)):
            kat_ref[c] = k_at[c].astype(BF16)
            vt_ref[c] = v_at[c].T.astype(BF16)

    key_i = lax.broadcasted_iota(jnp.int32, (2 * w, 2 * w), 0)
    query_i = lax.broadcasted_iota(jnp.int32, (2 * w, 2 * w), 1) & (w - 1)
    sinks = []
    for g, side in chains:
        h0 = 4 * g + side
        sinks.append(jnp.concatenate([sink_ref[h0:h0 + 1, :], sink_ref[h0 + 2:h0 + 3, :]], axis=1))
    starts, scores = [], []
    for win in range(windows):
        blk = n * windows + win
        start = pl.multiple_of(jnp.maximum(blk - 1, 0) * w, w)
        diff = (blk * w + query_i) - (start + key_i)
        valid = (diff >= 0) & (diff < w)
        rows = slice(win * w, (win + 1) * w)
        qs = [jnp.concatenate([q_ref[rows, (2 * g) * LANES:(2 * g + 1) * LANES],
                               q_ref[rows, (2 * g + 1) * LANES:(2 * g + 2) * LANES]], axis=0) * SCALE
              for g in range(SWA_KV_HEADS)]
        starts.append(start)
        scores.append([jnp.where(valid, _dot_nt(kat_ref[c, pl.ds(start, 2 * w), :], qs[g]), NEG)
                       for c, (g, side) in enumerate(chains)])
    maxes = [[jnp.maximum(jnp.max(s, axis=0, keepdims=True), sink) for s, sink in zip(win_scores, sinks)]
             for win_scores in scores]
    probs = [[jnp.exp(s - m).astype(BF16) for s, m in zip(win_scores, win_maxes)]
             for win_scores, win_maxes in zip(scores, maxes)]
    outs = [[_dot(vt_ref[c, :, pl.ds(start, 2 * w)], p) for c, p in enumerate(win_probs)]
            for start, win_probs in zip(starts, probs)]
    for win in range(windows):
        for g in range(SWA_KV_HEADS):
            o_lo, o_hi = outs[win][2 * g], outs[win][2 * g + 1]
            d_lo = o_lo[HEAD_DIM:] + jnp.exp(sinks[2 * g] - maxes[win][2 * g])
            d_hi = o_hi[:HEAD_DIM] + jnp.exp(sinks[2 * g + 1] - maxes[win][2 * g + 1])
            o = jnp.concatenate([o_lo[:HEAD_DIM] / d_lo, o_hi[HEAD_DIM:] / d_hi], axis=0).T.astype(o_ref.dtype)
            o_ref[win * w:(win + 1) * w, (2 * g) * LANES:(2 * g + 1) * LANES] = o[:w]
            o_ref[win * w:(win + 1) * w, (2 * g + 1) * LANES:(2 * g + 2) * LANES] = o[w:]


def _swa(roped, plain, sink_rows, *, batch, seq):
    tq = SWA_WINDOW * math.gcd(seq // SWA_WINDOW, SWA_MAX_WINDOWS_PER_STEP)
    steps = seq // tq
    return pl.pallas_call(
        _swa_kernel,
        grid=(batch, steps),
        in_specs=[
            pl.BlockSpec((tq, WIDTH), lambda b, n: (b * steps + n, QA)),
            pl.BlockSpec((seq, LANES), lambda b, n: (b, KA)),
            pl.BlockSpec((seq, LANES), lambda b, n: (b, VA)),
            pl.BlockSpec((HEADS, LANES), lambda b, n: (0, 0)),
        ],
        out_specs=pl.BlockSpec((tq, WIDTH), lambda b, n: (b * steps + n, 0)),
        out_shape=jax.ShapeDtypeStruct((batch * seq, WIDTH), BF16),
        scratch_shapes=[pltpu.VMEM((2 * SWA_KV_HEADS, seq, LANES), BF16), pltpu.VMEM((2 * SWA_KV_HEADS, LANES, seq), BF16)],
        compiler_params=_params("arbitrary", "arbitrary"),
    )(roped, roped, plain, sink_rows)


def _softplus(z):
    return jnp.maximum(z, 0.0) + jnp.log(1.0 + jnp.exp2(jnp.abs(z) * -LOG2E))


def _sb_kernel(q_ref, k_ref, v_ref, o_ref, vt_ref, acc_ref):
    i = pl.program_id(1)
    t = SB_BLOCK
    nblocks = k_ref.shape[0] // t
    low = _lane() < HEAD_DIM
    mine = [low, jnp.logical_not(low)]
    chains = [(p, side) for p in range(PAIRS) for side in range(2)]

    @pl.when(i == 0)
    def _():
        dim_low = lax.broadcasted_iota(jnp.int32, (LANES, t), 0) < HEAD_DIM
        for j in range(nblocks):
            rows = slice(j * t, (j + 1) * t)
            for p in range(PAIRS):
                v_t = v_ref[rows, p * LANES:(p + 1) * LANES].astype(F32).T
                vt_ref[0, p * LANES:(p + 1) * LANES, rows] = jnp.where(dim_low, v_t, 0.0).astype(BF16)
                vt_ref[1, p * LANES:(p + 1) * LANES, rows] = jnp.where(dim_low, 0.0, v_t).astype(BF16)

    key = lax.broadcasted_iota(jnp.int32, (t, t), 0)
    query = lax.broadcasted_iota(jnp.int32, (t, t), 1)
    suffix = jnp.where(query >= key, 1.0, 0.0).astype(BF16)
    past = key < query
    qs = [jnp.where(mine[side], q_ref[:, p * LANES:(p + 1) * LANES] * SCALE, jnp.zeros((1, LANES), BF16)) for p, side in chains]

    def block(off, size, remains, diagonal):
        zs = [_dot_nt(k_ref[pl.ds(off, size), p * LANES:(p + 1) * LANES], qs[n]) for n, (p, side) in enumerate(chains)]
        sps = []
        for z in zs:
            sp = _softplus(z)
            sps.append(jnp.where(past, sp, 0.0) if diagonal else sp)
        cs = [_dot(suffix[:size, :size], sp.astype(BF16)) for sp in sps]
        wgts = []
        for n, (z, c) in enumerate(zip(zs, cs)):
            logw = z - c - remains[n]
            if diagonal:
                logw = jnp.where(past, logw, NEG)
            wgts.append(jnp.exp(logw).astype(BF16))
        pvs = [_dot(vt_ref[side, p * LANES:(p + 1) * LANES, pl.ds(off, size)], wgts[n]) for n, (p, side) in enumerate(chains)]
        for p in range(PAIRS):
            both = pvs[2 * p] + pvs[2 * p + 1]
            acc_ref[p] = both if diagonal else acc_ref[p] + both
        return tuple(remains[n] + cs[n][0:1, :] for n in range(len(chains)))

    def smallest(remains):
        return jnp.min(functools.reduce(jnp.minimum, remains))

    remains = block(pl.multiple_of(i * t, t), t, tuple(jnp.zeros((1, t), F32) for _ in chains), True)

    runs = i * (t // SB_PAST)

    def cond(state):
        s, floor, _ = state
        return (s < runs) & (floor < SB_DONE)

    def body(state):
        s, _, remains = state
        remains = block(pl.multiple_of((runs - 1 - s) * SB_PAST, SB_PAST), SB_PAST, remains, False)
        return s + 1, smallest(remains), remains

    lax.while_loop(cond, body, (0, smallest(remains), remains))
    for p in range(PAIRS):
        o_ref[:, p * LANES:(p + 1) * LANES] = acc_ref[p].T.astype(o_ref.dtype)


def _stick_breaking(plain, *, batch, seq):
    t = SB_BLOCK
    nq = seq // t
    return pl.pallas_call(
        _sb_kernel,
        grid=(batch, nq),
        in_specs=[
            pl.BlockSpec((t, WIDTH), lambda b, i: (b * nq + i, QB)),
            pl.BlockSpec((seq, WIDTH), lambda b, i: (b, KB)),
            pl.BlockSpec((seq, WIDTH), lambda b, i: (b, VB)),
        ],
        out_specs=pl.BlockSpec((t, WIDTH), lambda b, i: (b * nq + i, 0)),
        out_shape=jax.ShapeDtypeStruct((batch * seq, WIDTH), BF16),
        scratch_shapes=[pltpu.VMEM((2, WIDTH, seq), BF16), pltpu.VMEM((PAIRS, LANES, t), F32)],
        compiler_params=_params("arbitrary", "arbitrary"),
    )(plain, plain, plain)


def _moba_kernel(q_ref, k_ref, v_ref, o_ref, kmean_ref, vaug_ref, acc_ref, bias_ref, prob_ref, *, nblocks):
    i = pl.program_id(1)
    t = MOBA_BLOCK
    nbp = MOBA_MAX_BLOCKS
    low = _lane() < HEAD_DIM
    mine = [low, jnp.logical_not(low)]
    chains = [(p, side) for p in range(PAIRS) for side in range(2)]

    @pl.when(i == 0)
    def _():
        for p in range(PAIRS):
            rows = [jnp.mean(k_ref[j * t:(j + 1) * t, p * LANES:(p + 1) * LANES].astype(F32), axis=0, keepdims=True)
                    for j in range(nblocks)]
            rows += [jnp.zeros((1, LANES), F32)] * (nbp - nblocks)
            km = jnp.concatenate(rows, axis=0)
            kmean_ref[2 * p * nbp:(2 * p + 1) * nbp, :] = jnp.where(low, km, 0.0)
            kmean_ref[(2 * p + 1) * nbp:(2 * p + 2) * nbp, :] = jnp.where(low, 0.0, km)
        dim_low = lax.broadcasted_iota(jnp.int32, (LANES, t), 0) < HEAD_DIM
        for j in range(nblocks):
            rows = slice(j * t, (j + 1) * t)
            for p in range(PAIRS):
                v_t = v_ref[rows, p * LANES:(p + 1) * LANES].astype(F32).T
                vaug_ref[0, p * LANES:(p + 1) * LANES, rows] = jnp.where(dim_low, v_t, 1.0).astype(BF16)
                vaug_ref[1, p * LANES:(p + 1) * LANES, rows] = jnp.where(dim_low, 1.0, v_t).astype(BF16)

    gates = []
    for p in range(PAIRS):
        q = q_ref[:, p * LANES:(p + 1) * LANES]
        km = kmean_ref[2 * p * nbp:(2 * p + 2) * nbp, :]
        km_hi = km.astype(BF16)
        km_lo = (km - km_hi.astype(F32)).astype(BF16)
        gates.append((_dot_nt(km_hi, q) + _dot_nt(km_lo, q)).reshape(2, nbp, t))
    blk = lax.broadcasted_iota(jnp.int32, (HEADS, nbp, t), 1)
    gate = jnp.where(blk < i, jnp.concatenate(gates, axis=0), NEG)
    count = jnp.zeros((HEADS, nbp, t), F32)
    for other in range(nblocks):
        g_other = gate[:, other:other + 1, :]
        beats = (g_other > gate) | ((g_other == gate) & (other < blk))
        count = count + jnp.where(beats, 1.0, 0.0)
    bias_ref[...] = jnp.where((count < min(MOBA_TOPK, nblocks)) & (blk < i), 0.0, NEG)
    qs = [jnp.where(mine[side], q_ref[:, p * LANES:(p + 1) * LANES] * SCALE, jnp.zeros((1, LANES), BF16)) for p, side in chains]

    key = lax.broadcasted_iota(jnp.int32, (t, t), 0)
    query = lax.broadcasted_iota(jnp.int32, (t, t), 1)
    causal = key <= query

    def score(j):
        off = pl.multiple_of(j * t, t)
        return [_dot_nt(k_ref[pl.ds(off, t), p * LANES:(p + 1) * LANES], qs[n]) for n, (p, side) in enumerate(chains)]

    def weigh(scores, shift, maxes):
        new_max = [jnp.maximum(m, jnp.max(s, axis=0, keepdims=True) + b) for m, s, b in zip(maxes, scores, shift)]
        for n, (s, m, b) in enumerate(zip(scores, new_max, shift)):
            prob_ref[n] = jnp.exp(s - (m - b)).astype(BF16)
        return tuple(new_max), tuple(jnp.exp(mo - mn) for mo, mn in zip(maxes, new_max))

    def absorb(j, decay):
        off = pl.multiple_of(j * t, t)
        pvs = [_dot(vaug_ref[side, p * LANES:(p + 1) * LANES, pl.ds(off, t)], prob_ref[n]) for n, (p, side) in enumerate(chains)]
        for n, pv in enumerate(pvs):
            acc_ref[n] = acc_ref[n] * decay[n] + pv

    def step(j, carry):
        prev, maxes, decay = carry
        scores = score(j)
        absorb(prev, decay)
        maxes, decay = weigh(scores, [bias_ref[n, pl.ds(j, 1), :] for n in range(len(chains))], maxes)
        return j, maxes, decay

    acc_ref[...] = jnp.zeros(acc_ref.shape, F32)
    own = [jnp.where(causal, s, NEG) for s in score(i)]
    start = tuple(jnp.full((1, t), NEG, F32) for _ in chains)
    maxes, decay = weigh(own, [jnp.zeros((1, t), F32)] * len(chains), start)
    last, _, decay = lax.fori_loop(0, i, step, (i, maxes, decay))
    absorb(last, decay)
    accs = [acc_ref[n] for n in range(len(chains))]
    for p in range(PAIRS):
        a_lo, a_hi = accs[2 * p], accs[2 * p + 1]
        o_t = jnp.concatenate([a_lo[:HEAD_DIM] / a_lo[HEAD_DIM:], a_hi[HEAD_DIM:] / a_hi[:HEAD_DIM]], axis=0)
        o_ref[:, p * LANES:(p + 1) * LANES] = o_t.T.astype(o_ref.dtype)


def _moba(roped, plain, *, batch, seq):
    t = MOBA_BLOCK
    nq = seq // t
    assert seq % t == 0 and nq <= MOBA_MAX_BLOCKS
    return pl.pallas_call(
        functools.partial(_moba_kernel, nblocks=nq),
        grid=(batch, nq),
        in_specs=[
            pl.BlockSpec((t, WIDTH), lambda b, i: (b * nq + i, QC)),
            pl.BlockSpec((seq, WIDTH), lambda b, i: (b, KC)),
            pl.BlockSpec((seq, WIDTH), lambda b, i: (b, VC)),
        ],
        out_specs=pl.BlockSpec((t, WIDTH), lambda b, i: (b * nq + i, 0)),
        out_shape=jax.ShapeDtypeStruct((batch * seq, WIDTH), BF16),
        scratch_shapes=[pltpu.VMEM((HEADS * MOBA_MAX_BLOCKS, LANES), F32),
                        pltpu.VMEM((2, WIDTH, seq), BF16), pltpu.VMEM((HEADS, LANES, t), F32),
                        pltpu.VMEM((HEADS, MOBA_MAX_BLOCKS, t), F32), pltpu.VMEM((HEADS, t, t), BF16)],
        compiler_params=_params("arbitrary", "arbitrary"),
    )(roped, roped, plain)


def _merge_kernel(oa_ref, ob_ref, oc_ref, gate_ref, wb_ref, wo_ref, x_ref, o_ref):
    d = x_ref.shape[1]
    ys = [_dot(branch[...], wb_ref[n].astype(BF16)) for n, branch in enumerate((oa_ref, ob_ref, oc_ref))]
    merged = None
    for n, y in enumerate(ys):
        term = y / (1.0 + jnp.exp(-gate_ref[:, n * d:(n + 1) * d].astype(F32)))
        merged = term if merged is None else merged + term
    o_ref[...] = x_ref[...] + _dot(merged.astype(BF16), wo_ref[...].astype(BF16))


def _merge(oa, ob, oc, gates, wb_layers, wo_layers, layer, x, *, tm):
    t, d = x.shape
    bw = oa.shape[1]
    row = lambda i: (i, 0)
    return pl.pallas_call(
        _merge_kernel,
        grid=(t // tm,),
        in_specs=[
            pl.BlockSpec((tm, bw), row),
            pl.BlockSpec((tm, bw), row),
            pl.BlockSpec((tm, bw), row),
            pl.BlockSpec((tm, N_BRANCHES * d), row),
            pl.BlockSpec((None, N_BRANCHES, bw, d), lambda i: (layer, 0, 0, 0), pipeline_mode=pl.Buffered(1)),
            pl.BlockSpec((None, d, d), lambda i: (layer, 0, 0), pipeline_mode=pl.Buffered(1)),
            pl.BlockSpec((tm, d), row),
        ],
        out_specs=pl.BlockSpec((tm, d), row),
        out_shape=jax.ShapeDtypeStruct((t, d), F32),
        compiler_params=_params("arbitrary"),
    )(oa, ob, oc, gates, wb_layers, wo_layers, x)


def _shift_rows(u, tail, shift):
    rolled = pltpu.roll(u, shift, axis=0)
    head = jnp.where(lax.broadcasted_iota(jnp.int32, tail.shape, 0) < shift,
                     pltpu.roll(tail, shift, axis=0), rolled[:8])
    return jnp.concatenate([head, rolled[8:]], axis=0)


def _ffn_kernel(x_ref, g_ref, wup_ref, cw_ref, cb_ref, wdn_ref, gf_ref, o_ref, tail_ref, acc_ref,
                *, tiles_per_seq, chunk, final_norm):
    i = pl.program_id(0)
    tm = x_ref.shape[0]
    dff = wdn_ref.shape[0]

    @pl.when(i % tiles_per_seq == 0)
    def _():
        tail_ref[...] = jnp.zeros(tail_ref.shape, F32)

    x = x_ref[...]
    ms = jnp.mean(x * x, axis=-1, keepdims=True)
    h = (x * lax.rsqrt(ms + RMS_EPS) * g_ref[...]).astype(BF16)

    def up(c0, c1):
        return _dot(h, wup_ref[:, c0:c1]), _dot(h, wup_ref[:, dff + c0:dff + c1])

    def conv(u, c0, c1):
        tail = tail_ref[:, c0:c1]
        tail_ref[:, c0:c1] = u[tm - 8:, :]
        out = cb_ref[:, c0:c1] + cw_ref[0:1, c0:c1] * _shift_rows(u, tail, 2)
        out = out + cw_ref[1:2, c0:c1] * _shift_rows(u, tail, 1)
        return out + cw_ref[2:3, c0:c1] * u

    bounds = [(c0, min(c0 + chunk, dff)) for c0 in range(0, dff, chunk)]
    pre = up(*bounds[0])
    for n, (c0, c1) in enumerate(bounds):
        cur = pre
        if n + 1 < len(bounds):
            pre = up(*bounds[n + 1])
        ua = conv(cur[0], c0, c1)
        uv = conv(cur[1], dff + c0, dff + c1)
        act = (ua / (1.0 + jnp.exp(-ua)) * uv).astype(BF16)
        part = _dot(act, wdn_ref[c0:c1, :])
        if n == 0:
            acc_ref[...] = x + part
        else:
            acc_ref[...] += part

    y = acc_ref[...]
    if final_norm:
        ms = jnp.mean(y * y, axis=-1, keepdims=True)
        y = y * lax.rsqrt(ms + RMS_EPS) * gf_ref[...]
    o_ref[...] = y


def _ffn(x, gain, wup_layers, cw, cb, wdn_layers, layer, gain_final, *, seq, tm, chunk, final_norm):
    t, d = x.shape
    dff = wdn_layers.shape[1]
    assert seq % tm == 0
    fixed = lambda i: (0, 0)
    picked = lambda i: (layer, 0, 0)
    once = pl.Buffered(1)
    return pl.pallas_call(
        functools.partial(_ffn_kernel, tiles_per_seq=seq // tm, chunk=chunk, final_norm=final_norm),
        grid=(t // tm,),
        in_specs=[
            pl.BlockSpec((tm, d), lambda i: (i, 0)),
            pl.BlockSpec((1, d), fixed),
            pl.BlockSpec((None, d, 2 * dff), picked, pipeline_mode=once),
            pl.BlockSpec((3, 2 * dff), fixed),
            pl.BlockSpec((1, 2 * dff), fixed),
            pl.BlockSpec((None, dff, d), picked, pipeline_mode=once),
            pl.BlockSpec((1, d), fixed),
        ],
        out_specs=pl.BlockSpec((tm, d), lambda i: (i, 0)),
        out_shape=jax.ShapeDtypeStruct((t, d), F32),
        scratch_shapes=[pltpu.VMEM((8, 2 * dff), F32), pltpu.VMEM((tm, d), F32)],
        compiler_params=_params("arbitrary"),
    )(x, gain, wup_layers, cw, cb, wdn_layers, gain_final)


def _rope_tables(seq):
    inv = ROPE_THETA ** (-jnp.arange(0, HEAD_DIM, 2, dtype=F32) / HEAD_DIM)
    ang = jnp.arange(seq, dtype=F32)[:, None] * inv[None, :]
    cos, sin = jnp.cos(ang), jnp.sin(ang)
    reps = LANES // HEAD_DIM
    return jnp.tile(cos, (1, 2 * reps)), jnp.tile(jnp.concatenate([-sin, sin], axis=1), (1, reps))


def kernel(x, norm_mix, w_in, b_gate, sinks, w_branch, w_out, norm_ffn, w_up, conv_w, conv_b, w_down, norm_final):
    batch, seq, d = x.shape
    depth = w_in.shape[0]
    tokens = batch * seq
    cos, sin_signed = _rope_tables(seq)
    xt = x.reshape(tokens, d)
    tm_tok, tm_ffn = min(512, seq), min(1024, seq)
    wup, wdn = w_up.astype(BF16), w_down.astype(BF16)
    for layer in range(depth):
        roped, plain, gates = _in_project(xt, norm_mix[layer].reshape(1, d), w_in, layer,
                                          cos, sin_signed, b_gate[layer].reshape(1, N_BRANCHES * d),
                                          seq=seq, tm=tm_tok, chunk=512)
        sink_rows = jnp.broadcast_to(sinks[layer].astype(F32)[:, None], (HEADS, LANES))
        o_a = _swa(roped, plain, sink_rows, batch=batch, seq=seq)
        o_b = _stick_breaking(plain, batch=batch, seq=seq)
        o_c = _moba(roped, plain, batch=batch, seq=seq)
        xt = _merge(o_a, o_b, o_c, gates, w_branch, w_out, layer, xt, tm=tm_tok)
        xt = _ffn(xt, norm_ffn[layer].reshape(1, d), wup, conv_w[layer],
                  conv_b[layer].reshape(1, -1), wdn, layer, norm_final.reshape(1, d),
                  seq=seq, tm=tm_ffn, chunk=1536, final_norm=(layer == depth - 1))
    return xt.reshape(batch, seq, d)
```

```python
import functools
import math

import jax
import jax.numpy as jnp
from jax import lax
from jax.experimental import pallas as pl
from jax.experimental.pallas import tpu as pltpu

F32 = jnp.float32
BF16 = jnp.bfloat16

HEAD_DIM = 64
LANES = 128
HEADS = 8
PAIRS = HEADS // 2
WIDTH = HEADS * HEAD_DIM
SWA_KV_HEADS = 2
SWA_WINDOW = 128
SWA_MAX_WINDOWS_PER_STEP = 16
SB_BLOCK = 256
SB_PAST = 128
SB_DONE = 88.0
MOBA_BLOCK = 256
MOBA_TOPK = 3
MOBA_MAX_BLOCKS = 8
N_BRANCHES = 3
ROPE_THETA = 10000.0
RMS_EPS = 1e-6
SCALE = HEAD_DIM ** -0.5
LOG2E = 1.4426950408889634
NEG = -1e30

VMEM_LIMIT = 56 * 1024 * 1024

ROPE_COLS = 3 * WIDTH + LANES
PLAIN_COLS = 4 * WIDTH + LANES
_KV = SWA_KV_HEADS * HEAD_DIM
_B0, _C0 = WIDTH + 2 * _KV, WIDTH + 2 * _KV + 3 * WIDTH
ROPE_SRC = ((0, WIDTH), (_C0, 2 * WIDTH), (WIDTH, _KV))
PLAIN_SRC = ((_B0, 3 * WIDTH), (_C0 + 2 * WIDTH, WIDTH), (WIDTH + _KV, _KV))
GATE_START = _C0 + 3 * WIDTH
QA, QC, KC = range(3)
KA = 3 * PAIRS
QB, KB, VB, VC = range(4)
VA = 4 * PAIRS


def _params(*sem):
    return pltpu.CompilerParams(dimension_semantics=sem, vmem_limit_bytes=VMEM_LIMIT)


def _dot(a, b):
    return jnp.dot(a, b, preferred_element_type=F32)


def _dot_nt(a, b):
    return lax.dot_general(a, b, (((1,), (1,)), ((), ())), preferred_element_type=F32)


def _lane():
    return lax.broadcasted_iota(jnp.int32, (1, LANES), 1)


def _rope(piece, cos, sin_signed):
    first_half = (_lane() & 32) == 0
    nxt = pltpu.roll(piece, LANES - 32, axis=1)
    prv = pltpu.roll(piece, 32, axis=1)
    return piece * cos + jnp.where(first_half, nxt, prv) * sin_signed


def _in_proj_kernel(x_ref, g_ref, w_ref, cos_ref, sin_ref, bias_ref, rope_ref, plain_ref, gate_ref, *, chunk):
    x = x_ref[...]
    ms = jnp.mean(x * x, axis=-1, keepdims=True)
    h = (x * lax.rsqrt(ms + RMS_EPS) * g_ref[...]).astype(BF16)
    cos, sin_signed = cos_ref[...], sin_ref[...]
    for out_ref, pieces in ((rope_ref, ROPE_SRC), (plain_ref, PLAIN_SRC), (gate_ref, ((GATE_START, gate_ref.shape[1]),))):
        dst = 0
        for src, width in pieces:
            for c0 in range(0, width, chunk):
                n = min(chunk, width - c0)
                acc = _dot(h, w_ref[:, src + c0:src + c0 + n].astype(BF16))
                lo = dst + c0
                if out_ref is rope_ref:
                    for c in range(0, n, LANES):
                        out_ref[:, lo + c:lo + c + LANES] = _rope(acc[:, c:c + LANES], cos, sin_signed).astype(out_ref.dtype)
                elif out_ref is gate_ref:
                    out_ref[:, lo:lo + n] = (acc + bias_ref[:, lo:lo + n]).astype(out_ref.dtype)
                else:
                    out_ref[:, lo:lo + n] = acc.astype(out_ref.dtype)
            dst += width


def _in_project(x, gain, w_layers, layer, cos, sin_signed, gate_bias, *, seq, tm, chunk):
    t, d = x.shape
    n_in = w_layers.shape[2]
    n_gate = n_in - ROPE_COLS - PLAIN_COLS
    assert t % tm == 0 and seq % tm == 0
    pos_tiles = seq // tm
    row = lambda i: (i, 0)
    return pl.pallas_call(
        functools.partial(_in_proj_kernel, chunk=chunk),
        grid=(t // tm,),
        in_specs=[
            pl.BlockSpec((tm, d), row),
            pl.BlockSpec((1, d), lambda i: (0, 0)),
            pl.BlockSpec((None, d, n_in), lambda i: (layer, 0, 0), pipeline_mode=pl.Buffered(1)),
            pl.BlockSpec((tm, LANES), lambda i: (i % pos_tiles, 0)),
            pl.BlockSpec((tm, LANES), lambda i: (i % pos_tiles, 0)),
            pl.BlockSpec((1, n_gate), lambda i: (0, 0)),
        ],
        out_specs=[pl.BlockSpec((tm, ROPE_COLS), row), pl.BlockSpec((tm, PLAIN_COLS), row),
                   pl.BlockSpec((tm, n_gate), row)],
        out_shape=[jax.ShapeDtypeStruct((t, ROPE_COLS), BF16), jax.ShapeDtypeStruct((t, PLAIN_COLS), BF16),
                   jax.ShapeDtypeStruct((t, n_gate), BF16)],
        compiler_params=_params("arbitrary"),
    )(x, gain, w_layers, cos, sin_signed, gate_bias)


def _swa_kernel(q_ref, k_ref, v_ref, sink_ref, o_ref, kat_ref, vt_ref):
    n = pl.program_id(1)
    w = SWA_WINDOW
    windows = q_ref.shape[0] // w
    low = _lane() < HEAD_DIM
    chains = [(g, side) for g in range(SWA_KV_HEADS) for side in range(2)]

    @pl.when(n == 0)
    def _():
        k = k_ref[...].astype(F32)
        v = v_ref[...].astype(F32)
        k_sw = pltpu.roll(k, HEAD_DIM, axis=1)
        v_sw = pltpu.roll(v, HEAD_DIM, axis=1)
        k_at = [jnp.where(low, k, 0.0), jnp.where(low, 0.0, k_sw), jnp.where(low, k_sw, 0.0), jnp.where(low, 0.0, k)]
        v_at = [jnp.where(low, v, 1.0), jnp.where(low, 1.0, v_sw), jnp.where(low, v_sw, 1.0), jnp.where(low, 1.0, v)]
        for c in range(len(chains)):
            kat_ref[c] = k_at[c].astype(BF16)
            vt_ref[c] = v_at[c].T.astype(BF16)

    key_i = lax.broadcasted_iota(jnp.int32, (2 * w, 2 * w), 0)
    query_i = lax.broadcasted_iota(jnp.int32, (2 * w, 2 * w), 1) & (w - 1)
    sinks = []
    for g, side in chains:
        h0 = 4 * g + side
        sinks.append(jnp.concatenate([sink_ref[h0:h0 + 1, :], sink_ref[h0 + 2:h0 + 3, :]], axis=1))
    starts, scores = [], []
    for win in range(windows):
        blk = n * windows + win
        start = pl.multiple_of(jnp.maximum(blk - 1, 0) * w, w)
        diff = (blk * w + query_i) - (start + key_i)
        valid = (diff >= 0) & (diff < w)
        rows = slice(win * w, (win + 1) * w)
        qs = [jnp.concatenate([q_ref[rows, (2 * g) * LANES:(2 * g + 1) * LANES],
                               q_ref[rows, (2 * g + 1) * LANES:(2 * g + 2) * LANES]], axis=0) * SCALE
              for g in range(SWA_KV_HEADS)]
        starts.append(start)
        scores.append([jnp.where(valid, _dot_nt(kat_ref[c, pl.ds(start, 2 * w), :], qs[g]), NEG)
                       for c, (g, side) in enumerate(chains)])
    maxes = [[jnp.maximum(jnp.max(s, axis=0, keepdims=True), sink) for s, sink in zip(win_scores, sinks)]
             for win_scores in scores]
    probs = [[jnp.exp(s - m).astype(BF16) for s, m in zip(win_scores, win_maxes)]
             for win_scores, win_maxes in zip(scores, maxes)]
    outs = [[_dot(vt_ref[c, :, pl.ds(start, 2 * w)], p) for c, p in enumerate(win_probs)]
            for start, win_probs in zip(starts, probs)]
    for win in range(windows):
        for g in range(SWA_KV_HEADS):
            o_lo, o_hi = outs[win][2 * g], outs[win][2 * g + 1]
            d_lo = o_lo[HEAD_DIM:] + jnp.exp(sinks[2 * g] - maxes[win][2 * g])
            d_hi = o_hi[:HEAD_DIM] + jnp.exp(sinks[2 * g + 1] - maxes[win][2 * g + 1])
            o = jnp.concatenate([o_lo[:HEAD_DIM] / d_lo, o_hi[HEAD_DIM:] / d_hi], axis=0).T.astype(o_ref.dtype)
            o_ref[win * w:(win + 1) * w, (2 * g) * LANES:(2 * g + 1) * LANES] = o[:w]
            o_ref[win * w:(win + 1) * w, (2 * g + 1) * LANES:(2 * g + 2) * LANES] = o[w:]


def _swa(roped, plain, sink_rows, *, batch, seq):
    tq = SWA_WINDOW * math.gcd(seq // SWA_WINDOW, SWA_MAX_WINDOWS_PER_STEP)
    steps = seq // tq
    return pl.pallas_call(
        _swa_kernel,
        grid=(batch, steps),
        in_specs=[
            pl.BlockSpec((tq, WIDTH), lambda b, n: (b * steps + n, QA)),
            pl.BlockSpec((seq, LANES), lambda b, n: (b, KA)),
            pl.BlockSpec((seq, LANES), lambda b, n: (b, VA)),
            pl.BlockSpec((HEADS, LANES), lambda b, n: (0, 0)),
        ],
        out_specs=pl.BlockSpec((tq, WIDTH), lambda b, n: (b * steps + n, 0)),
        out_shape=jax.ShapeDtypeStruct((batch * seq, WIDTH), BF16),
        scratch_shapes=[pltpu.VMEM((2 * SWA_KV_HEADS, seq, LANES), BF16), pltpu.VMEM((2 * SWA_KV_HEADS, LANES, seq), BF16)],
        compiler_params=_params("arbitrary", "arbitrary"),
    )(roped, roped, plain, sink_rows)


def _softplus(z):
    return jnp.maximum(z, 0.0) + jnp.log(1.0 + jnp.exp2(jnp.abs(z) * -LOG2E))


def _sb_kernel(q_ref, k_ref, v_ref, o_ref, kmask_ref, vt_ref, acc_ref):
    i = pl.program_id(1)
    t = SB_BLOCK
    nblocks = k_ref.shape[0] // t
    low = _lane() < HEAD_DIM
    mine = [low, jnp.logical_not(low)]
    chains = [(p, side) for p in range(PAIRS) for side in range(2)]

    @pl.when(i == 0)
    def _():
        dim_low = lax.broadcasted_iota(jnp.int32, (LANES, t), 0) < HEAD_DIM
        for j in range(nblocks):
            rows = slice(j * t, (j + 1) * t)
            for side in range(2):
                kmask_ref[side, rows, :] = jnp.where(jnp.tile(mine[side], (1, PAIRS)), k_ref[rows, :], jnp.zeros((1, WIDTH), BF16))
            for p in range(PAIRS):
                v_t = v_ref[rows, p * LANES:(p + 1) * LANES].astype(F32).T
                vt_ref[0, p * LANES:(p + 1) * LANES, rows] = jnp.where(dim_low, v_t, 0.0).astype(BF16)
                vt_ref[1, p * LANES:(p + 1) * LANES, rows] = jnp.where(dim_low, 0.0, v_t).astype(BF16)

    key = lax.broadcasted_iota(jnp.int32, (t, t), 0)
    query = lax.broadcasted_iota(jnp.int32, (t, t), 1)
    suffix = jnp.where(query >= key, 1.0, 0.0).astype(BF16)
    past = key < query
    qs = [q_ref[:, p * LANES:(p + 1) * LANES] * SCALE for p in range(PAIRS)]

    def block(off, size, remains, diagonal):
        zs = [_dot_nt(kmask_ref[side, pl.ds(off, size), p * LANES:(p + 1) * LANES], qs[p]) for p, side in chains]
        sps = []
        for z in zs:
            sp = _softplus(z)
            sps.append(jnp.where(past, sp, 0.0) if diagonal else sp)
        cs = [_dot(suffix[:size, :size], sp.astype(BF16)) for sp in sps]
        wgts = []
        for n, (z, c) in enumerate(zip(zs, cs)):
            logw = z - c - remains[n]
            if diagonal:
                logw = jnp.where(past, logw, NEG)
            wgts.append(jnp.exp(logw).astype(BF16))
        pvs = [_dot(vt_ref[side, p * LANES:(p + 1) * LANES, pl.ds(off, size)], wgts[n]) for n, (p, side) in enumerate(chains)]
        for p in range(PAIRS):
            both = pvs[2 * p] + pvs[2 * p + 1]
            acc_ref[p] = both if diagonal else acc_ref[p] + both
        return tuple(remains[n] + cs[n][0:1, :] for n in range(len(chains)))

    def smallest(remains):
        return jnp.min(functools.reduce(jnp.minimum, remains))

    remains = block(pl.multiple_of(i * t, t), t, tuple(jnp.zeros((1, t), F32) for _ in chains), True)

    runs = i * (t // SB_PAST)

    def cond(state):
        s, floor, _ = state
        return (s < runs) & (floor < SB_DONE)

    def body(state):
        s, _, remains = state
        remains = block(pl.multiple_of((runs - 1 - s) * SB_PAST, SB_PAST), SB_PAST, remains, False)
        return s + 1, smallest(remains), remains

    lax.while_loop(cond, body, (0, smallest(remains), remains))
    for p in range(PAIRS):
        o_ref[:, p * LANES:(p + 1) * LANES] = acc_ref[p].T.astype(o_ref.dtype)


def _stick_breaking(plain, *, batch, seq):
    t = SB_BLOCK
    nq = seq // t
    return pl.pallas_call(
        _sb_kernel,
        grid=(batch, nq),
        in_specs=[
            pl.BlockSpec((t, WIDTH), lambda b, i: (b * nq + i, QB)),
            pl.BlockSpec((seq, WIDTH), lambda b, i: (b, KB)),
            pl.BlockSpec((seq, WIDTH), lambda b, i: (b, VB)),
        ],
        out_specs=pl.BlockSpec((t, WIDTH), lambda b, i: (b * nq + i, 0)),
        out_shape=jax.ShapeDtypeStruct((batch * seq, WIDTH), BF16),
        scratch_shapes=[pltpu.VMEM((2, seq, WIDTH), BF16), pltpu.VMEM((2, WIDTH, seq), BF16),
                        pltpu.VMEM((PAIRS, LANES, t), F32)],
        compiler_params=_params("arbitrary", "arbitrary"),
    )(plain, plain, plain)


def _moba_kernel(q_ref, k_ref, v_ref, o_ref, kmean_ref, kmask_ref, vaug_ref, acc_ref, bias_ref, prob_ref, *, nblocks):
    i = pl.program_id(1)
    t = MOBA_BLOCK
    nbp = MOBA_MAX_BLOCKS
    low = _lane() < HEAD_DIM
    mine = [low, jnp.logical_not(low)]
    chains = [(p, side) for p in range(PAIRS) for side in range(2)]

    @pl.when(i == 0)
    def _():
        for p in range(PAIRS):
            rows = [jnp.mean(k_ref[j * t:(j + 1) * t, p * LANES:(p + 1) * LANES].astype(F32), axis=0, keepdims=True)
                    for j in range(nblocks)]
            rows += [jnp.zeros((1, LANES), F32)] * (nbp - nblocks)
            km = jnp.concatenate(rows, axis=0)
            kmean_ref[2 * p * nbp:(2 * p + 1) * nbp, :] = jnp.where(low, km, 0.0)
            kmean_ref[(2 * p + 1) * nbp:(2 * p + 2) * nbp, :] = jnp.where(low, 0.0, km)
        dim_low = lax.broadcasted_iota(jnp.int32, (LANES, t), 0) < HEAD_DIM
        for j in range(nblocks):
            rows = slice(j * t, (j + 1) * t)
            for side in range(2):
                kmask_ref[side, rows, :] = jnp.where(jnp.tile(mine[side], (1, PAIRS)), k_ref[rows, :], jnp.zeros((1, WIDTH), BF16))
            for p in range(PAIRS):
                v_t = v_ref[rows, p * LANES:(p + 1) * LANES].astype(F32).T
                vaug_ref[0, p * LANES:(p + 1) * LANES, rows] = jnp.where(dim_low, v_t, 1.0).astype(BF16)
                vaug_ref[1, p * LANES:(p + 1) * LANES, rows] = jnp.where(dim_low, 1.0, v_t).astype(BF16)

    gates = []
    for p in range(PAIRS):
        q = q_ref[:, p * LANES:(p + 1) * LANES]
        km = kmean_ref[2 * p * nbp:(2 * p + 2) * nbp, :]
        km_hi = km.astype(BF16)
        km_lo = (km - km_hi.astype(F32)).astype(BF16)
        gates.append((_dot_nt(km_hi, q) + _dot_nt(km_lo, q)).reshape(2, nbp, t))
    blk = lax.broadcasted_iota(jnp.int32, (HEADS, nbp, t), 1)
    gate = jnp.where(blk < i, jnp.concatenate(gates, axis=0), NEG)
    count = jnp.zeros((HEADS, nbp, t), F32)
    for other in range(nblocks):
        g_other = gate[:, other:other + 1, :]
        beats = (g_other > gate) | ((g_other == gate) & (other < blk))
        count = count + jnp.where(beats, 1.0, 0.0)
    bias_ref[...] = jnp.where((count < min(MOBA_TOPK, nblocks)) & (blk < i), 0.0, NEG)
    qs = [q_ref[:, p * LANES:(p + 1) * LANES] * SCALE for p in range(PAIRS)]

    key = lax.broadcasted_iota(jnp.int32, (t, t), 0)
    query = lax.broadcasted_iota(jnp.int32, (t, t), 1)
    causal = key <= query

    def score(j):
        off = pl.multiple_of(j * t, t)
        return [_dot_nt(kmask_ref[side, pl.ds(off, t), p * LANES:(p + 1) * LANES], qs[p]) for p, side in chains]

    def weigh(scores, shift, maxes):
        new_max = [jnp.maximum(m, jnp.max(s, axis=0, keepdims=True) + b) for m, s, b in zip(maxes, scores, shift)]
        for n, (s, m, b) in enumerate(zip(scores, new_max, shift)):
            prob_ref[n] = jnp.exp(s - (m - b)).astype(BF16)
        return tuple(new_max), tuple(jnp.exp(mo - mn) for mo, mn in zip(maxes, new_max))

    def absorb(j, decay):
        off = pl.multiple_of(j * t, t)
        pvs = [_dot(vaug_ref[side, p * LANES:(p + 1) * LANES, pl.ds(off, t)], prob_ref[n]) for n, (p, side) in enumerate(chains)]
        for n, pv in enumerate(pvs):
            acc_ref[n] = acc_ref[n] * decay[n] + pv

    def step(j, carry):
        prev, maxes, decay = carry
        scores = score(j)
        absorb(prev, decay)
        maxes, decay = weigh(scores, [bias_ref[n, pl.ds(j, 1), :] for n in range(len(chains))], maxes)
        return j, maxes, decay

    acc_ref[...] = jnp.zeros(acc_ref.shape, F32)
    own = [jnp.where(causal, s, NEG) for s in score(i)]
    start = tuple(jnp.full((1, t), NEG, F32) for _ in chains)
    maxes, decay = weigh(own, [jnp.zeros((1, t), F32)] * len(chains), start)
    last, _, decay = lax.fori_loop(0, i, step, (i, maxes, decay))
    absorb(last, decay)
    accs = [acc_ref[n] for n in range(len(chains))]
    for p in range(PAIRS):
        a_lo, a_hi = accs[2 * p], accs[2 * p + 1]
        o_t = jnp.concatenate([a_lo[:HEAD_DIM] / a_lo[HEAD_DIM:], a_hi[HEAD_DIM:] / a_hi[:HEAD_DIM]], axis=0)
        o_ref[:, p * LANES:(p + 1) * LANES] = o_t.T.astype(o_ref.dtype)


def _moba(roped, plain, *, batch, seq):
    t = MOBA_BLOCK
    nq = seq // t
    assert seq % t == 0 and nq <= MOBA_MAX_BLOCKS
    return pl.pallas_call(
        functools.partial(_moba_kernel, nblocks=nq),
        grid=(batch, nq),
        in_specs=[
            pl.BlockSpec((t, WIDTH), lambda b, i: (b * nq + i, QC)),
            pl.BlockSpec((seq, WIDTH), lambda b, i: (b, KC)),
            pl.BlockSpec((seq, WIDTH), lambda b, i: (b, VC)),
        ],
        out_specs=pl.BlockSpec((t, WIDTH), lambda b, i: (b * nq + i, 0)),
        out_shape=jax.ShapeDtypeStruct((batch * seq, WIDTH), BF16),
        scratch_shapes=[pltpu.VMEM((HEADS * MOBA_MAX_BLOCKS, LANES), F32), pltpu.VMEM((2, seq, WIDTH), BF16),
                        pltpu.VMEM((2, WIDTH, seq), BF16), pltpu.VMEM((HEADS, LANES, t), F32),
                        pltpu.VMEM((HEADS, MOBA_MAX_BLOCKS, t), F32), pltpu.VMEM((HEADS, t, t), BF16)],
        compiler_params=_params("arbitrary", "arbitrary"),
    )(roped, roped, plain)


def _merge_kernel(oa_ref, ob_ref, oc_ref, gate_ref, wb_ref, wo_ref, x_ref, o_ref):
    d = x_ref.shape[1]
    ys = [_dot(branch[...], wb_ref[n].astype(BF16)) for n, branch in enumerate((oa_ref, ob_ref, oc_ref))]
    merged = None
    for n, y in enumerate(ys):
        term = y / (1.0 + jnp.exp(-gate_ref[:, n * d:(n + 1) * d].astype(F32)))
        merged = term if merged is None else merged + term
    o_ref[...] = x_ref[...] + _dot(merged.astype(BF16), wo_ref[...].astype(BF16))


def _merge(oa, ob, oc, gates, wb_layers, wo_layers, layer, x, *, tm):
    t, d = x.shape
    bw = oa.shape[1]
    row = lambda i: (i, 0)
    return pl.pallas_call(
        _merge_kernel,
        grid=(t // tm,),
        in_specs=[
            pl.BlockSpec((tm, bw), row),
            pl.BlockSpec((tm, bw), row),
            pl.BlockSpec((tm, bw), row),
            pl.BlockSpec((tm, N_BRANCHES * d), row),
            pl.BlockSpec((None, N_BRANCHES, bw, d), lambda i: (layer, 0, 0, 0), pipeline_mode=pl.Buffered(1)),
            pl.BlockSpec((None, d, d), lambda i: (layer, 0, 0), pipeline_mode=pl.Buffered(1)),
            pl.BlockSpec((tm, d), row),
        ],
        out_specs=pl.BlockSpec((tm, d), row),
        out_shape=jax.ShapeDtypeStruct((t, d), F32),
        compiler_params=_params("arbitrary"),
    )(oa, ob, oc, gates, wb_layers, wo_layers, x)


def _shift_rows(u, tail, shift):
    rolled = pltpu.roll(u, shift, axis=0)
    head = jnp.where(lax.broadcasted_iota(jnp.int32, tail.shape, 0) < shift,
                     pltpu.roll(tail, shift, axis=0), rolled[:8])
    return jnp.concatenate([head, rolled[8:]], axis=0)


def _ffn_kernel(x_ref, g_ref, wup_ref, cw_ref, cb_ref, wdn_ref, gf_ref, o_ref, tail_ref, acc_ref,
                *, tiles_per_seq, chunk, final_norm):
    i = pl.program_id(0)
    tm = x_ref.shape[0]
    dff = wdn_ref.shape[0]

    @pl.when(i % tiles_per_seq == 0)
    def _():
        tail_ref[...] = jnp.zeros(tail_ref.shape, F32)

    x = x_ref[...]
    ms = jnp.mean(x * x, axis=-1, keepdims=True)
    h = (x * lax.rsqrt(ms + RMS_EPS) * g_ref[...]).astype(BF16)

    def up(c0, c1):
        return _dot(h, wup_ref[:, c0:c1]), _dot(h, wup_ref[:, dff + c0:dff + c1])

    def conv(u, c0, c1):
        tail = tail_ref[:, c0:c1]
        tail_ref[:, c0:c1] = u[tm - 8:, :]
        out = cb_ref[:, c0:c1] + cw_ref[0:1, c0:c1] * _shift_rows(u, tail, 2)
        out = out + cw_ref[1:2, c0:c1] * _shift_rows(u, tail, 1)
        return out + cw_ref[2:3, c0:c1] * u

    bounds = [(c0, min(c0 + chunk, dff)) for c0 in range(0, dff, chunk)]
    pre = up(*bounds[0])
    for n, (c0, c1) in enumerate(bounds):
        cur = pre
        if n + 1 < len(bounds):
            pre = up(*bounds[n + 1])
        ua = conv(cur[0], c0, c1)
        uv = conv(cur[1], dff + c0, dff + c1)
        acc_ref[:, c0:c1] = (ua / (1.0 + jnp.exp(-ua)) * uv).astype(BF16)

    y = x + _dot(acc_ref[...], wdn_ref[...])
    if final_norm:
        ms = jnp.mean(y * y, axis=-1, keepdims=True)
        y = y * lax.rsqrt(ms + RMS_EPS) * gf_ref[...]
    o_ref[...] = y


def _ffn(x, gain, wup_layers, cw, cb, wdn_layers, layer, gain_final, *, seq, tm, chunk, final_norm):
    t, d = x.shape
    dff = wdn_layers.shape[1]
    assert seq % tm == 0
    fixed = lambda i: (0, 0)
    picked = lambda i: (layer, 0, 0)
    once = pl.Buffered(1)
    return pl.pallas_call(
        functools.partial(_ffn_kernel, tiles_per_seq=seq // tm, chunk=chunk, final_norm=final_norm),
        grid=(t // tm,),
        in_specs=[
            pl.BlockSpec((tm, d), lambda i: (i, 0)),
            pl.BlockSpec((1, d), fixed),
            pl.BlockSpec((None, d, 2 * dff), picked, pipeline_mode=once),
            pl.BlockSpec((3, 2 * dff), fixed),
            pl.BlockSpec((1, 2 * dff), fixed),
            pl.BlockSpec((None, dff, d), picked, pipeline_mode=once),
            pl.BlockSpec((1, d), fixed),
        ],
        out_specs=pl.BlockSpec((tm, d), lambda i: (i, 0)),
        out_shape=jax.ShapeDtypeStruct((t, d), F32),
        scratch_shapes=[pltpu.VMEM((8, 2 * dff), F32), pltpu.VMEM((tm, dff), BF16)],
        compiler_params=_params("arbitrary"),
    )(x, gain, wup_layers, cw, cb, wdn_layers, gain_final)


def _rope_tables(seq):
    inv = ROPE_THETA ** (-jnp.arange(0, HEAD_DIM, 2, dtype=F32) / HEAD_DIM)
    ang = jnp.arange(seq, dtype=F32)[:, None] * inv[None, :]
    cos, sin = jnp.cos(ang), jnp.sin(ang)
    reps = LANES // HEAD_DIM
    return jnp.tile(cos, (1, 2 * reps)), jnp.tile(jnp.concatenate([-sin, sin], axis=1), (1, reps))


def kernel(x, norm_mix, w_in, b_gate, sinks, w_branch, w_out, norm_ffn, w_up, conv_w, conv_b, w_down, norm_final):
    batch, seq, d = x.shape
    depth = w_in.shape[0]
    tokens = batch * seq
    cos, sin_signed = _rope_tables(seq)
    xt = x.reshape(tokens, d)
    tm_tok, tm_ffn = min(512, seq), min(1024, seq)
    wup, wdn = w_up.astype(BF16), w_down.astype(BF16)
    for layer in range(depth):
        roped, plain, gates = _in_project(xt, norm_mix[layer].reshape(1, d), w_in, layer,
                                          cos, sin_signed, b_gate[layer].reshape(1, N_BRANCHES * d),
                                          seq=seq, tm=tm_tok, chunk=512)
        sink_rows = jnp.broadcast_to(sinks[layer].astype(F32)[:, None], (HEADS, LANES))
        o_a = _swa(roped, plain, sink_rows, batch=batch, seq=seq)
        o_b = _stick_breaking(plain, batch=batch, seq=seq)
        o_c = _moba(roped, plain, batch=batch, seq=seq)
        xt = _merge(o_a, o_b, o_c, gates, w_branch, w_out, layer, xt, tm=tm_tok)
        xt = _ffn(xt, norm_ffn[layer].reshape(1, d), wup, conv_w[layer],
                  conv_b[layer].reshape(1, -1), wdn, layer, norm_final.reshape(1, d),
                  seq=seq, tm=tm_ffn, chunk=1536, final_norm=(layer == depth - 1))
    return xt.reshape(batch, seq, d)
```
